```python
import math
import jax, jax.numpy as jnp
from jax import lax
import numpy as np

D_MODEL = 1024
BATCH = 8
SEQ = 4096
DEPTH = 2
DEC_BATCH = 1
DEC_SEQ = 16384
PAST_LEN = 128

HEAD_DIM = 64
N_Q_HEADS = 8
N_KV_HEADS = 2
GQA_GROUP = N_Q_HEADS // N_KV_HEADS
D_ATTN = N_Q_HEADS * HEAD_DIM
D_KV = N_KV_HEADS * HEAD_DIM
WINDOW = 128
BLOCK = 128
POOL_WINDOWS = (2, 4, 8, 16)
N_POOL_GROUPS = len(POOL_WINDOWS)
D_POOL = 256
POOL_GROUP_DIM = D_POOL // N_POOL_GROUPS
D_HYENA = 256
HYENA_ORDER = 2
FILTER_BANDS = 16
FILTER_EMB = 1 + 2 * FILTER_BANDS
FILTER_HIDDEN = 64
N_FILTERS = 2 * HYENA_ORDER
DECAY_FAST_PCT = 0.3
DECAY_SLOW_PCT = 1.5
DECAY_TARGET = 1e-2
SHORT_CONV = 3
D_IN_PROJ = D_ATTN + 2 * D_KV + D_POOL + (HYENA_ORDER + 1) * D_HYENA
D_CAT = D_ATTN + D_POOL + D_HYENA
D_FF = 2816
FFN_CONV = 3
EPS = 1e-6
NEG_INF = -1e30

kernel_name = "hymba_style_bidir_hybrid_encoder"


def rms_norm(x, g):
    xf = x.astype(jnp.float32)
    y = xf * lax.rsqrt(jnp.mean(xf * xf, axis=-1, keepdims=True) + EPS) * g.astype(jnp.float32)
    return y.astype(x.dtype)


def dwconv3(x, w, b):
    xp = jnp.pad(x, ((0, 0), (1, 1), (0, 0)))
    return xp[:, :-2] * w[0] + xp[:, 1:-1] * w[1] + xp[:, 2:] * w[2] + b


def alibi_slopes():
    h = jnp.arange(N_Q_HEADS, dtype=jnp.float32)
    return 2.0 ** (-8.0 * (h + 1.0) / N_Q_HEADS)


def windowed_attention(q, k, v, q_norm_g, k_norm_g, sink):
    B, L = q.shape[0], q.shape[1]
    nb = L // BLOCK
    q = rms_norm(q.reshape(B, L, N_Q_HEADS, HEAD_DIM), q_norm_g)
    k = rms_norm(k.reshape(B, L, N_KV_HEADS, HEAD_DIM), k_norm_g)
    v = v.reshape(B, L, N_KV_HEADS, HEAD_DIM)
    pad = ((0, 0), (BLOCK, BLOCK), (0, 0), (0, 0))
    kp = jnp.pad(k, pad).reshape(B, nb + 2, BLOCK, N_KV_HEADS, HEAD_DIM)
    vp = jnp.pad(v, pad).reshape(B, nb + 2, BLOCK, N_KV_HEADS, HEAD_DIM)
    kb = jnp.concatenate([kp[:, :-2], kp[:, 1:-1], kp[:, 2:]], axis=2)
    vb = jnp.concatenate([vp[:, :-2], vp[:, 1:-1], vp[:, 2:]], axis=2)
    qb = q.reshape(B, nb, BLOCK, N_KV_HEADS, GQA_GROUP, HEAD_DIM)
    scale = 1.0 / math.sqrt(HEAD_DIM)
    s = jnp.einsum('bnqkgd,bnskd->bnkgqs', qb, kb, preferred_element_type=jnp.float32) * scale
    blk = jnp.arange(nb)[:, None]
    qpos = blk * BLOCK + jnp.arange(BLOCK)[None, :]
    kpos = (blk - 1) * BLOCK + jnp.arange(3 * BLOCK)[None, :]
    dist = jnp.abs(qpos[:, :, None] - kpos[:, None, :])
    valid = (dist <= WINDOW) & (kpos[:, None, :] >= 0) & (kpos[:, None, :] < L)
    slopes = alibi_slopes().reshape(N_KV_HEADS, GQA_GROUP)
    s = s - slopes[None, None, :, :, None, None] * dist.astype(jnp.float32)[None, :, None, None, :, :]
    s = jnp.where(valid[None, :, None, None, :, :], s, NEG_INF)
    sink_col = jnp.broadcast_to(sink.astype(jnp.float32).reshape(1, 1, N_KV_HEADS, GQA_GROUP, 1, 1),
                                s.shape[:-1] + (1,))
    p = jax.nn.softmax(jnp.concatenate([s, sink_col], axis=-1), axis=-1)[..., :-1]
    o = jnp.einsum('bnkgqs,bnskd->bnqkgd', p.astype(v.dtype), vb)
    return o.reshape(B, L, D_ATTN)


def multiscale_pool(u, pool_w, pool_scale):
    B, L, _ = u.shape
    uf = u.astype(jnp.float32)
    cs = jnp.concatenate([jnp.zeros((B, 1, D_POOL), jnp.float32), jnp.cumsum(uf, axis=1)], axis=1)
    t = jnp.arange(L)
    outs = []
    for gi, w in enumerate(POOL_WINDOWS):
        sl = slice(gi * POOL_GROUP_DIM, (gi + 1) * POOL_GROUP_DIM)
        lo = jnp.clip(t - w // 2, 0, L)
        hi = jnp.clip(t - w // 2 + w, 0, L)
        csg = cs[:, :, sl]
        mean = (jnp.take(csg, hi, axis=1) - jnp.take(csg, lo, axis=1)) / (hi - lo).astype(jnp.float32)[None, :, None]
        d = mean - uf[:, :, sl]
        outs.append(d @ pool_w[gi].astype(jnp.float32))
    y = jnp.concatenate(outs, axis=-1) * pool_scale.astype(jnp.float32)
    return y.astype(u.dtype)


def hyena_filters(L, w1, b1, freq1, w2, b2, freq2, w3):
    t_norm = jnp.linspace(0.0, 1.0, L, dtype=jnp.float32)[:, None]
    n = jnp.arange(L, dtype=jnp.float32)[:, None]
    bands = jnp.linspace(1e-4, FILTER_BANDS - 1, FILTER_BANDS, dtype=jnp.float32)[None, :]
    ang = 2.0 * math.pi * n * bands / L
    z = jnp.concatenate([t_norm, jnp.cos(ang), -jnp.sin(ang)], axis=-1)
    f32 = jnp.float32
    h = jnp.sin(freq1.astype(f32) * (z @ w1.astype(f32) + b1.astype(f32)))
    h = jnp.sin(freq2.astype(f32) * (h @ w2.astype(f32) + b2.astype(f32)))
    h = (h @ w3.astype(f32)).reshape(L, N_FILTERS, D_HYENA)
    max_decay = math.log(DECAY_TARGET) / DECAY_FAST_PCT
    min_decay = math.log(DECAY_TARGET) / DECAY_SLOW_PCT
    deltas = jnp.abs(jnp.linspace(min_decay, max_decay, D_HYENA, dtype=jnp.float32))
    decay = jnp.exp(-t_norm[:, :, None] * deltas[None, None, :])
    return h * decay


def bidir_fftconv(u, h_fwd, h_bwd, bias):
    L = u.shape[1]
    C = u.shape[2]
    k = jnp.concatenate([h_fwd, jnp.zeros((1, C), jnp.float32), h_bwd[1:][::-1]], axis=0)
    K = jnp.fft.rfft(k, axis=0)
    uf = u.astype(jnp.float32)
    U = jnp.fft.rfft(uf, n=2 * L, axis=1)
    y = jnp.fft.irfft(U * K[None], n=2 * L, axis=1)[:, :L]
    return (y + bias.astype(jnp.float32) * uf).astype(u.dtype)


def hyena_mixer(u, conv_w, conv_b, filters, hy_bias):
    u = dwconv3(u, conv_w, conv_b)
    v = u[..., :D_HYENA]
    gates = (u[..., D_HYENA:2 * D_HYENA], u[..., 2 * D_HYENA:])
    z = v
    for o in range(HYENA_ORDER):
        z = gates[o] * bidir_fftconv(z, filters[:, 2 * o], filters[:, 2 * o + 1], hy_bias[o])
    return z


def encoder_layer(x, norm1_g, w_in, q_norm_g, k_norm_g, attn_sink, pool_w, pool_scale,
                  hy_conv_w, hy_conv_b, filt_w1, filt_b1, filt_freq1, filt_w2, filt_b2, filt_freq2,
                  filt_w3, hy_bias, out_norm_g, w_out, norm2_g, w_ffn_in, ffn_conv_w, ffn_conv_b, w_ffn_out):
    L = x.shape[1]
    h = rms_norm(x, norm1_g)
    p = h @ w_in
    o0 = D_ATTN
    o1 = o0 + D_KV
    o2 = o1 + D_KV
    o3 = o2 + D_POOL
    q, k, v = p[..., :o0], p[..., o0:o1], p[..., o1:o2]
    pool_in, hy_in = p[..., o2:o3], p[..., o3:]
    a = windowed_attention(q, k, v, q_norm_g, k_norm_g, attn_sink)
    b = multiscale_pool(pool_in, pool_w, pool_scale)
    filters = hyena_filters(L, filt_w1, filt_b1, filt_freq1, filt_w2, filt_b2, filt_freq2, filt_w3)
    c = hyena_mixer(hy_in, hy_conv_w, hy_conv_b, filters, hy_bias)
    cat = jnp.concatenate([rms_norm(a, out_norm_g[:D_ATTN]),
                           rms_norm(b, out_norm_g[D_ATTN:D_ATTN + D_POOL]),
                           rms_norm(c, out_norm_g[D_ATTN + D_POOL:])], axis=-1)
    x = x + cat @ w_out
    h = rms_norm(x, norm2_g)
    u = dwconv3(h @ w_ffn_in, ffn_conv_w, ffn_conv_b)
    act = jax.nn.gelu(u[..., :D_FF], approximate=False) * u[..., D_FF:]
    return x + act @ w_ffn_out


def run_trunk(x, norm1_g, w_in, q_norm_g, k_norm_g, attn_sink, pool_w, pool_scale,
              hy_conv_w, hy_conv_b, filt_w1, filt_b1, filt_freq1, filt_w2, filt_b2, filt_freq2,
              filt_w3, hy_bias, out_norm_g, w_out, norm2_g, w_ffn_in, ffn_conv_w, ffn_conv_b, w_ffn_out):
    for l in range(DEPTH):
        x = encoder_layer(x, norm1_g[l], w_in[l], q_norm_g[l], k_norm_g[l], attn_sink[l], pool_w[l],
                          pool_scale[l], hy_conv_w[l], hy_conv_b[l], filt_w1[l], filt_b1[l], filt_freq1[l],
                          filt_w2[l], filt_b2[l], filt_freq2[l], filt_w3[l], hy_bias[l], out_norm_g[l],
                          w_out[l], norm2_g[l], w_ffn_in[l], ffn_conv_w[l], ffn_conv_b[l], w_ffn_out[l])
    return x


def setup_inputs(seed: int = 0) -> dict:
    key = jax.random.key(seed)
    ks = jax.random.split(key, 32)
    f32 = jnp.float32

    def nrm(k, shape, scale):
        return jax.random.normal(k, shape, f32) * scale

    return {
        "x_prompt": nrm(ks[0], (BATCH, SEQ, D_MODEL), 1.0),
        "x_sample": nrm(ks[1], (DEC_BATCH, DEC_SEQ, D_MODEL), 1.0),
        "norm1_g": 1.0 + nrm(ks[2], (DEPTH, D_MODEL), 0.05),
        "w_in": nrm(ks[3], (DEPTH, D_MODEL, D_IN_PROJ), D_MODEL ** -0.5),
        "q_norm_g": 1.0 + nrm(ks[4], (DEPTH, HEAD_DIM), 0.05),
        "k_norm_g": 1.0 + nrm(ks[5], (DEPTH, HEAD_DIM), 0.05),
        "attn_sink": nrm(ks[6], (DEPTH, N_Q_HEADS), 0.5),
        "pool_w": nrm(ks[7], (DEPTH, N_POOL_GROUPS, POOL_GROUP_DIM, POOL_GROUP_DIM), POOL_GROUP_DIM ** -0.5),
        "pool_scale": 1.0 + nrm(ks[8], (DEPTH, D_POOL), 0.1),
        "hy_conv_w": nrm(ks[9], (DEPTH, SHORT_CONV, (HYENA_ORDER + 1) * D_HYENA), SHORT_CONV ** -0.5),
        "hy_conv_b": nrm(ks[10], (DEPTH, (HYENA_ORDER + 1) * D_HYENA), 0.02),
        "filt_w1": nrm(ks[11], (DEPTH, FILTER_EMB, FILTER_HIDDEN), FILTER_EMB ** -0.5),
        "filt_b1": nrm(ks[12], (DEPTH, FILTER_HIDDEN), 0.02),
        "filt_freq1": 1.0 + nrm(ks[13], (DEPTH, FILTER_HIDDEN), 0.1),
        "filt_w2": nrm(ks[14], (DEPTH, FILTER_HIDDEN, FILTER_HIDDEN), FILTER_HIDDEN ** -0.5),
        "filt_b2": nrm(ks[15], (DEPTH, FILTER_HIDDEN), 0.02),
        "filt_freq2": 1.0 + nrm(ks[16], (DEPTH, FILTER_HIDDEN), 0.1),
        "filt_w3": nrm(ks[17], (DEPTH, FILTER_HIDDEN, N_FILTERS * D_HYENA), 0.05 * FILTER_HIDDEN ** -0.5),
        "hy_bias": nrm(ks[18], (DEPTH, HYENA_ORDER, D_HYENA), 0.1),
        "out_norm_g": 1.0 + nrm(ks[19], (DEPTH, D_CAT), 0.05),
        "w_out": nrm(ks[20], (DEPTH, D_CAT, D_MODEL), (2.0 * DEPTH * D_CAT) ** -0.5),
        "norm2_g": 1.0 + nrm(ks[21], (DEPTH, D_MODEL), 0.05),
        "w_ffn_in": nrm(ks[22], (DEPTH, D_MODEL, 2 * D_FF), D_MODEL ** -0.5),
        "ffn_conv_w": nrm(ks[23], (DEPTH, FFN_CONV, 2 * D_FF), FFN_CONV ** -0.5),
        "ffn_conv_b": nrm(ks[24], (DEPTH, 2 * D_FF), 0.02),
        "w_ffn_out": nrm(ks[25], (DEPTH, D_FF, D_MODEL), (2.0 * DEPTH * D_FF) ** -0.5),
    }


def reference(x_prompt, x_sample, norm1_g, w_in, q_norm_g, k_norm_g, attn_sink, pool_w, pool_scale,
              hy_conv_w, hy_conv_b, filt_w1, filt_b1, filt_freq1, filt_w2, filt_b2, filt_freq2,
              filt_w3, hy_bias, out_norm_g, w_out, norm2_g, w_ffn_in, ffn_conv_w, ffn_conv_b, w_ffn_out):
    y_prompt = run_trunk(x_prompt, norm1_g, w_in, q_norm_g, k_norm_g, attn_sink, pool_w, pool_scale,
                         hy_conv_w, hy_conv_b, filt_w1, filt_b1, filt_freq1, filt_w2, filt_b2, filt_freq2,
                         filt_w3, hy_bias, out_norm_g, w_out, norm2_g, w_ffn_in, ffn_conv_w, ffn_conv_b, w_ffn_out)
    y_sample = run_trunk(x_sample, norm1_g, w_in, q_norm_g, k_norm_g, attn_sink, pool_w, pool_scale,
                         hy_conv_w, hy_conv_b, filt_w1, filt_b1, filt_freq1, filt_w2, filt_b2, filt_freq2,
                         filt_w3, hy_bias, out_norm_g, w_out, norm2_g, w_ffn_in, ffn_conv_w, ffn_conv_b, w_ffn_out)
    return (y_prompt, y_sample)
```

```python
import functools
import math

import jax
import jax.numpy as jnp
from jax import lax
from jax.experimental import pallas as pl
from jax.experimental.pallas import tpu as pltpu

F32 = jnp.float32
BF16 = jnp.bfloat16

D_MODEL = 1024
DEPTH = 2
HEAD_DIM = 64
N_Q_HEADS = 8
N_KV_HEADS = 2
GQA_GROUP = N_Q_HEADS // N_KV_HEADS
D_ATTN = N_Q_HEADS * HEAD_DIM
D_KV = N_KV_HEADS * HEAD_DIM
WINDOW = 128
BLOCK = 128
POOL_WINDOWS = (2, 4, 8, 16)
D_POOL = 256
POOL_GROUP_DIM = D_POOL // len(POOL_WINDOWS)
D_HYENA = 256
HYENA_ORDER = 2
FILTER_BANDS = 16
FILTER_EMB = 1 + 2 * FILTER_BANDS
FILTER_HIDDEN = 64
N_FILTERS = 2 * HYENA_ORDER
DECAY_FAST_PCT = 0.3
DECAY_SLOW_PCT = 1.5
DECAY_TARGET = 1e-2
D_HY_IN = (HYENA_ORDER + 1) * D_HYENA
D_PH = D_POOL + D_HY_IN
D_IN_PROJ = D_ATTN + 2 * D_KV + D_PH
D_CAT = D_ATTN + D_POOL + D_HYENA
D_AB = D_ATTN + D_POOL
D_FF = 2816
EPS = 1e-6
NEG_INF = -1e30

LANES = 128
SUBLANES = 8
BF16_ROWS = 16
FFT_N2 = 128
FFT_J = SUBLANES
FFT_TF = 8
FFN_CHUNK = 256
FFN_IN_CHUNK = 512
ROW_TILE = 256
VMEM_LIMIT = 48 * 1024 * 1024


def _params(n_axes):
    return pltpu.CompilerParams(dimension_semantics=("arbitrary",) * n_axes, vmem_limit_bytes=VMEM_LIMIT)


def _rms(x, g):
    return x * lax.rsqrt(jnp.mean(x * x, axis=-1, keepdims=True) + EPS) * g


def _dot(a, b):
    return jnp.dot(a, b, preferred_element_type=F32)


def _in_proj_kernel(x_ref, g_ref, w_ref, eq_ref, ek_ref, qg_ref, kg_ref, q_ref, kv_ref, ph_ref):
    h = _rms(x_ref[...], g_ref[...]).astype(BF16)
    p = _dot(h, w_ref[...])

    def head_norm(t, e_ref, gain):
        tt = t * t
        hi = tt.astype(BF16)
        lo = (tt - hi.astype(F32)).astype(BF16)
        ss = _dot(hi, e_ref[...]) + _dot(lo, e_ref[...])
        return t * lax.rsqrt(ss * (1.0 / HEAD_DIM) + EPS) * gain

    o1 = D_ATTN + D_KV
    o2 = o1 + D_KV
    q = head_norm(p[:, :D_ATTN], eq_ref, qg_ref[...]) * (1.0 / math.sqrt(HEAD_DIM))
    k = head_norm(p[:, D_ATTN:o1], ek_ref, kg_ref[...])
    q_ref[...] = q.astype(BF16)
    kv_ref[:, :D_KV] = k.astype(BF16)
    kv_ref[:, D_KV:] = p[:, o1:o2].astype(BF16)
    ph_ref[...] = p[:, o2:]


def _in_proj(x2, g, w, eq, ek, qg, kg):
    T = x2.shape[0]
    tm = ROW_TILE
    row = lambda i: (i, 0)
    fix = lambda i: (0, 0)
    return pl.pallas_call(
        _in_proj_kernel,
        out_shape=[jax.ShapeDtypeStruct((T, D_ATTN), BF16),
                   jax.ShapeDtypeStruct((T, 2 * D_KV), BF16),
                   jax.ShapeDtypeStruct((T, D_PH), F32)],
        grid=(T // tm,),
        in_specs=[pl.BlockSpec((tm, D_MODEL), row),
                  pl.BlockSpec((1, D_MODEL), fix),
                  pl.BlockSpec((D_MODEL, D_IN_PROJ), fix),
                  pl.BlockSpec((D_ATTN, D_ATTN), fix),
                  pl.BlockSpec((D_KV, D_KV), fix),
                  pl.BlockSpec((1, D_ATTN), fix),
                  pl.BlockSpec((1, D_KV), fix)],
        out_specs=[pl.BlockSpec((tm, D_ATTN), row),
                   pl.BlockSpec((tm, 2 * D_KV), row),
                   pl.BlockSpec((tm, D_PH), row)],
        compiler_params=_params(1),
        name="in_proj",
    )(x2, g, w, eq, ek, qg, kg)


def _mix_kernel(sink_ref, q_ref, kvp_ref, kvc_ref, kvn_ref, phc_ref, php_ref, phn_ref,
                pw_ref, ps_ref, hw_ref, hb_ref, ga_ref, gb_ref,
                cat_ref, hy_ref, ext_ref, att_ref, *, seq_len):
    l = pl.program_id(1)
    is_first = l == 0
    is_last = l == pl.num_programs(1) - 1

    row = lax.broadcasted_iota(jnp.int32, (BLOCK, 3 * BLOCK), 0)
    col = lax.broadcasted_iota(jnp.int32, (BLOCK, 3 * BLOCK), 1)
    dist = jnp.abs(row + BLOCK - col)
    valid = dist <= WINDOW
    valid = valid & ((col >= BLOCK) | jnp.logical_not(is_first))
    valid = valid & ((col < 2 * BLOCK) | jnp.logical_not(is_last))
    distf = dist.astype(F32)
    for h in range(N_Q_HEADS):
        g = h // GQA_GROUP
        ks = slice(g * HEAD_DIM, (g + 1) * HEAD_DIM)
        vs = slice(D_KV + g * HEAD_DIM, D_KV + (g + 1) * HEAD_DIM)
        qh = q_ref[:, h * HEAD_DIM:(h + 1) * HEAD_DIM]
        kh = jnp.concatenate([kvp_ref[:, ks], kvc_ref[:, ks], kvn_ref[:, ks]], axis=0)
        vh = jnp.concatenate([kvp_ref[:, vs], kvc_ref[:, vs], kvn_ref[:, vs]], axis=0)
        s = lax.dot_general(qh, kh, (((1,), (1,)), ((), ())), preferred_element_type=F32)
        s = s - (2.0 ** (-(h + 1))) * distf
        s = jnp.where(valid, s, NEG_INF)
        sk = sink_ref[h]
        m = jnp.maximum(jnp.max(s, axis=-1, keepdims=True), sk)
        e = jnp.exp(s - m)
        den = jnp.sum(e, axis=-1, keepdims=True) + jnp.exp(sk - m)
        att_ref[:, h * HEAD_DIM:(h + 1) * HEAD_DIM] = _dot(e.astype(BF16), vh) / den
    cat_ref[:, :D_ATTN] = _rms(att_ref[...], ga_ref[...]).astype(BF16)

    ext_ref[0:SUBLANES, :] = jnp.where(is_first, 0.0, php_ref[...])
    ext_ref[SUBLANES:SUBLANES + BLOCK, :] = phc_ref[...]
    ext_ref[SUBLANES + BLOCK:, :] = jnp.where(is_last, 0.0, phn_ref[...])

    def sh(d):
        return ext_ref[SUBLANES + d:SUBLANES + d + BLOCK, 0:D_POOL]

    u = sh(0)
    s2 = sh(-1) + u
    s4 = s2 + sh(-2) + sh(1)
    s8 = s4 + sh(-4) + sh(-3) + sh(2) + sh(3)
    s16 = s8 + sh(-8) + sh(-7) + sh(-6) + sh(-5) + sh(4) + sh(5) + sh(6) + sh(7)
    t = l * BLOCK + lax.broadcasted_iota(jnp.int32, (BLOCK, 1), 0)

    def count(w):
        lo = jnp.clip(t - w // 2, 0, seq_len)
        hi = jnp.clip(t - w // 2 + w, 0, seq_len)
        return (hi - lo).astype(F32)

    lane = lax.broadcasted_iota(jnp.int32, (1, D_POOL), 1)
    mean = jnp.where(lane < POOL_GROUP_DIM, s2 / count(2),
                     jnp.where(lane < 2 * POOL_GROUP_DIM, s4 / count(4),
                               jnp.where(lane < 3 * POOL_GROUP_DIM, s8 / count(8), s16 / count(16))))
    pooled = _dot((mean - u).astype(BF16), pw_ref[...]) * ps_ref[...]
    cat_ref[:, D_ATTN:] = _rms(pooled, gb_ref[...]).astype(BF16)

    hp = ext_ref[SUBLANES - 1:SUBLANES - 1 + BLOCK, D_POOL:]
    hc = ext_ref[SUBLANES:SUBLANES + BLOCK, D_POOL:]
    hn = ext_ref[SUBLANES + 1:SUBLANES + 1 + BLOCK, D_POOL:]
    hy_ref[...] = hp * hw_ref[0:1, :] + hc * hw_ref[1:2, :] + hn * hw_ref[2:3, :] + hb_ref[...]


def _mix(q, kv, ph, sink, pw_bd, pool_scale, hy_w, hy_b, g_a, g_b, B, L):
    T = B * L
    nb = L // BLOCK
    rpb = BLOCK // SUBLANES
    n8 = T // SUBLANES
    cur = lambda b, l: (b * nb + l, 0)
    prev = lambda b, l: (b * nb + jnp.maximum(l - 1, 0), 0)
    nxt = lambda b, l: (b * nb + jnp.minimum(l + 1, nb - 1), 0)
    prev8 = lambda b, l: (jnp.maximum((b * nb + l) * rpb - 1, 0), 0)
    next8 = lambda b, l: (jnp.minimum((b * nb + l + 1) * rpb, n8 - 1), 0)
    fix = lambda b, l: (0, 0)
    return pl.pallas_call(
        functools.partial(_mix_kernel, seq_len=L),
        out_shape=[jax.ShapeDtypeStruct((T, D_AB), BF16),
                   jax.ShapeDtypeStruct((T, D_HY_IN), F32)],
        grid=(B, nb),
        in_specs=[pl.BlockSpec(memory_space=pltpu.SMEM),
                  pl.BlockSpec((BLOCK, D_ATTN), cur),
                  pl.BlockSpec((BLOCK, 2 * D_KV), prev),
                  pl.BlockSpec((BLOCK, 2 * D_KV), cur),
                  pl.BlockSpec((BLOCK, 2 * D_KV), nxt),
                  pl.BlockSpec((BLOCK, D_PH), cur),
                  pl.BlockSpec((SUBLANES, D_PH), prev8),
                  pl.BlockSpec((SUBLANES, D_PH), next8),
                  pl.BlockSpec((D_POOL, D_POOL), fix),
                  pl.BlockSpec((1, D_POOL), fix),
                  pl.BlockSpec((3, D_HY_IN), fix),
                  pl.BlockSpec((1, D_HY_IN), fix),
                  pl.BlockSpec((1, D_ATTN), fix),
                  pl.BlockSpec((1, D_POOL), fix)],
        out_specs=[pl.BlockSpec((BLOCK, D_AB), cur),
                   pl.BlockSpec((BLOCK, D_HY_IN), cur)],
        scratch_shapes=[pltpu.VMEM((BLOCK + 2 * SUBLANES, D_PH), F32),
                        pltpu.VMEM((BLOCK, D_ATTN), F32)],
        compiler_params=_params(2),
        name="mix",
    )(sink, q, kv, kv, kv, ph, ph, ph, pw_bd, pool_scale, hy_w, hy_b, g_a, g_b)


def _filter_kernel(z_ref, w1_ref, b1_ref, f1_ref, w2_ref, b2_ref, f2_ref, w3_ref, dl_ref, o_ref):
    hp = lax.Precision.HIGHEST
    z = z_ref[...]
    h = jnp.sin(f1_ref[...] * (jnp.dot(z, w1_ref[...], precision=hp, preferred_element_type=F32) + b1_ref[...]))
    h = jnp.sin(f2_ref[...] * (jnp.dot(h, w2_ref[...], precision=hp, preferred_element_type=F32) + b2_ref[...]))
    h = jnp.dot(h, w3_ref[...], precision=hp, preferred_element_type=F32)
    o_ref[...] = h * jnp.exp(-z[:, 0:1] * dl_ref[...])


def _filters(L, w1, b1, fr1, w2, b2, fr2, w3):
    t_norm = jnp.linspace(0.0, 1.0, L, dtype=F32)[:, None]
    n = jnp.arange(L, dtype=F32)[:, None]
    bands = jnp.linspace(1e-4, FILTER_BANDS - 1, FILTER_BANDS, dtype=F32)[None, :]
    ang = 2.0 * math.pi * n * bands / L
    z = jnp.concatenate([t_norm, jnp.cos(ang), -jnp.sin(ang)], axis=-1)
    zp = jnp.pad(z, ((0, 0), (0, LANES - FILTER_EMB)))
    w1p = jnp.pad(w1, ((0, LANES - FILTER_EMB), (0, 0)))
    max_decay = math.log(DECAY_TARGET) / DECAY_FAST_PCT
    min_decay = math.log(DECAY_TARGET) / DECAY_SLOW_PCT
    deltas = jnp.abs(jnp.linspace(min_decay, max_decay, D_HYENA, dtype=F32))
    dl = jnp.tile(deltas, N_FILTERS)[None, :]
    tl = 512
    row = lambda i: (i, 0)
    fix = lambda i: (0, 0)
    nf = N_FILTERS * D_HYENA
    return pl.pallas_call(
        _filter_kernel,
        out_shape=jax.ShapeDtypeStruct((L, nf), F32),
        grid=(L // tl,),
        in_specs=[pl.BlockSpec((tl, LANES), row),
                  pl.BlockSpec((LANES, FILTER_HIDDEN), fix),
                  pl.BlockSpec((1, FILTER_HIDDEN), fix),
                  pl.BlockSpec((1, FILTER_HIDDEN), fix),
                  pl.BlockSpec((FILTER_HIDDEN, FILTER_HIDDEN), fix),
                  pl.BlockSpec((1, FILTER_HIDDEN), fix),
                  pl.BlockSpec((1, FILTER_HIDDEN), fix),
                  pl.BlockSpec((FILTER_HIDDEN, nf), fix),
                  pl.BlockSpec((1, nf), fix)],
        out_specs=pl.BlockSpec((tl, nf), row),
        compiler_params=_params(1),
        name="hyena_filters",
    )(zp, w1p, b1[None, :], fr1[None, :], w2, b2[None, :], fr2[None, :], w3, dl)


def _dft_mats(L):
    N = 2 * L
    n2n = FFT_N2
    n1n = N // n2n
    r = n1n // 2
    a = jnp.arange(n1n, dtype=jnp.int32)
    th = (2.0 * math.pi / n1n) * ((a[:, None] * a[None, :]) % n1n).astype(F32)
    c1, s1 = jnp.cos(th), jnp.sin(th)
    fwd_full = jnp.concatenate([c1, -s1], axis=0).astype(BF16)
    fwd_half = fwd_full[:, :r]
    inv_half = jnp.concatenate([c1[:r, :], -s1[:r, :]], axis=1).astype(BF16)
    f = a[:, None, None] + n1n * jnp.arange(n2n, dtype=jnp.int32)[None, :, None]
    m = (f * jnp.arange(n2n, dtype=jnp.int32)[None, None, :]) % N
    ph = (2.0 * math.pi / N) * m.astype(F32)
    gr, gi = jnp.cos(ph), -jnp.sin(ph)
    g = jnp.concatenate([jnp.concatenate([gr, -gi], axis=2),
                         jnp.concatenate([gi, gr], axis=2)], axis=1).astype(BF16)
    return dict(n1=n1n, r=r, fwd_full=fwd_full, fwd_half=fwd_half, inv_half=inv_half,
                g=g, ginv=jnp.swapaxes(g, 1, 2))


def _fft_a_kernel(x_ref, f_ref, ar_ref, ai_ref, *, tb, n1):
    f = f_ref[...]
    for b in range(tb):
        for j in range(FFT_J):
            r = _dot(f, x_ref[b, :, j, :].astype(BF16))
            ar_ref[b, :, j, :] = r[:n1]
            ai_ref[b, :, j, :] = r[n1:]


def _fft_a(x4, coff, fmat, n1, tb):
    B, R = x4.shape[0], x4.shape[1]
    blk = lambda b, j, c: (b, 0, j, c)
    out = jax.ShapeDtypeStruct((B, n1, FFT_N2, D_HYENA), F32)
    return pl.pallas_call(
        functools.partial(_fft_a_kernel, tb=tb, n1=n1),
        out_shape=[out, out],
        grid=(B // tb, FFT_N2 // FFT_J, D_HYENA // LANES),
        in_specs=[pl.BlockSpec((tb, R, FFT_J, LANES), lambda b, j, c: (b, 0, j, c + coff)),
                  pl.BlockSpec((2 * n1, R), lambda b, j, c: (0, 0))],
        out_specs=[pl.BlockSpec((tb, n1, FFT_J, LANES), blk)] * 2,
        compiler_params=_params(3),
        name="fft_outer_fwd",
    )(x4, fmat)


def _fft_spec_kernel(ar_ref, ai_ref, g_ref, kr_ref, ki_ref, *, scale):
    for f in range(FFT_TF):
        a = jnp.concatenate([ar_ref[0, f], ai_ref[0, f]], axis=0).astype(BF16)
        x = _dot(g_ref[f], a) * scale
        kr_ref[0, f] = x[:FFT_N2]
        ki_ref[0, f] = x[FFT_N2:]


def _fft_spec(ar, ai, g, scale):
    B, n1 = ar.shape[0], ar.shape[1]
    dat = pl.BlockSpec((1, FFT_TF, FFT_N2, D_HYENA), lambda f, b: (b, f, 0, 0))
    out = jax.ShapeDtypeStruct(ar.shape, F32)
    return pl.pallas_call(
        functools.partial(_fft_spec_kernel, scale=scale),
        out_shape=[out, out],
        grid=(n1 // FFT_TF, B),
        in_specs=[dat, dat, pl.BlockSpec((FFT_TF, 2 * FFT_N2, 2 * FFT_N2), lambda f, b: (f, 0, 0))],
        out_specs=[dat, dat],
        compiler_params=_params(2),
        name="fft_inner_fwd",
    )(ar, ai, g)


def _fft_b_kernel(ar_ref, ai_ref, g_ref, gi_ref, kr_ref, ki_ref, yr_ref, yi_ref):
    for f in range(FFT_TF):
        a = jnp.concatenate([ar_ref[0, f], ai_ref[0, f]], axis=0).astype(BF16)
        x = _dot(g_ref[f], a)
        xr, xi = x[:FFT_N2], x[FFT_N2:]
        kr, ki = kr_ref[0, f], ki_ref[0, f]
        z = jnp.concatenate([xr * kr - xi * ki, xr * ki + xi * kr], axis=0).astype(BF16)
        y = _dot(gi_ref[f], z)
        yr_ref[0, f] = y[:FFT_N2]
        yi_ref[0, f] = y[FFT_N2:]


def _fft_b(ar, ai, g, ginv, kr, ki, o):
    B, n1 = ar.shape[0], ar.shape[1]
    dat = pl.BlockSpec((1, FFT_TF, FFT_N2, D_HYENA), lambda f, b: (b, f, 0, 0))
    mat = pl.BlockSpec((FFT_TF, 2 * FFT_N2, 2 * FFT_N2), lambda f, b: (f, 0, 0))
    spec = pl.BlockSpec((1, FFT_TF, FFT_N2, D_HYENA), lambda f, b: (o, f, 0, 0))
    out = jax.ShapeDtypeStruct(ar.shape, F32)
    return pl.pallas_call(
        _fft_b_kernel,
        out_shape=[out, out],
        grid=(n1 // FFT_TF, B),
        in_specs=[dat, dat, mat, mat, spec, spec],
        out_specs=[dat, dat],
        compiler_params=_params(2),
        name="fft_inner",
    )(ar, ai, g, ginv, kr, ki)


def _fft_c_kernel(yr_ref, yi_ref, c_ref, u_ref, gate_ref, bias_ref, o_ref, *, tb):
    cm = c_ref[...]
    bias = bias_ref[...]
    for b in range(tb):
        for j in range(FFT_J):
            yy = jnp.concatenate([yr_ref[b, :, j, :], yi_ref[b, :, j, :]], axis=0).astype(BF16)
            y = _dot(cm, yy)
            o_ref[b, :, j, :] = gate_ref[b, :, j, :] * (y + bias * u_ref[b, :, j, :])


def _fft_c(yr, yi, cmat, u4, uoff, gate4, goff, bias, tb):
    B, n1 = yr.shape[0], yr.shape[1]
    R = n1 // 2
    blk = lambda b, j, c: (b, 0, j, c)
    return pl.pallas_call(
        functools.partial(_fft_c_kernel, tb=tb),
        out_shape=jax.ShapeDtypeStruct((B, R, FFT_N2, D_HYENA), F32),
        grid=(B // tb, FFT_N2 // FFT_J, D_HYENA // LANES),
        in_specs=[pl.BlockSpec((tb, n1, FFT_J, LANES), blk),
                  pl.BlockSpec((tb, n1, FFT_J, LANES), blk),
                  pl.BlockSpec((R, 2 * n1), lambda b, j, c: (0, 0)),
                  pl.BlockSpec((tb, R, FFT_J, LANES), lambda b, j, c: (b, 0, j, c + uoff)),
                  pl.BlockSpec((tb, R, FFT_J, LANES), lambda b, j, c: (b, 0, j, c + goff)),
                  pl.BlockSpec((1, LANES), lambda b, j, c: (0, c))],
        out_specs=pl.BlockSpec((tb, R, FFT_J, LANES), blk),
        compiler_params=_params(3),
        name="fft_outer_inv",
    )(yr, yi, cmat, u4, gate4, bias)


def _filter_spectra(filt, mats, L):
    n1 = mats["n1"]
    h = filt.reshape(L, N_FILTERS, D_HYENA)
    zero = jnp.zeros((1, D_HYENA), F32)
    ks = [jnp.concatenate([h[:, 2 * o], zero, h[1:, 2 * o + 1][::-1]], axis=0) for o in range(HYENA_ORDER)]
    k4 = jnp.stack(ks).reshape(HYENA_ORDER, n1, FFT_N2, D_HYENA)
    ar, ai = _fft_a(k4, 0, mats["fwd_full"], n1, 1)
    return _fft_spec(ar, ai, mats["g"], 1.0 / (2 * L))


def _hyena(hy4, kr, ki, hy_bias, mats, tb):
    n1 = mats["n1"]
    nc = D_HYENA // LANES
    z4, zoff = hy4, 0
    for o in range(HYENA_ORDER):
        ar, ai = _fft_a(z4, zoff, mats["fwd_half"], n1, tb)
        yr, yi = _fft_b(ar, ai, mats["g"], mats["ginv"], kr, ki, o)
        z4 = _fft_c(yr, yi, mats["inv_half"], z4, zoff, hy4, (o + 1) * nc, hy_bias[o][None, :], tb)
        zoff = 0
    return z4


def _out_ffn_in_kernel(x_ref, cab_ref, c_ref, gc_ref, wab_ref, wc_ref, g2_ref, wi_ref, xn_ref, u_ref):
    cn = _rms(c_ref[...], gc_ref[...]).astype(BF16)
    xn = x_ref[...] + _dot(cab_ref[...], wab_ref[...]) + _dot(cn, wc_ref[...])
    xn_ref[...] = xn
    h = _rms(xn, g2_ref[...]).astype(BF16)
    for j in range(0, 2 * D_FF, FFN_IN_CHUNK):
        u_ref[:, j:j + FFN_IN_CHUNK] = _dot(h, wi_ref[:, j:j + FFN_IN_CHUNK]).astype(BF16)


def _out_ffn_in(x2, cab, c2, g_c, w_ab, w_c, g2, w_in):
    T = x2.shape[0]
    tm = ROW_TILE
    row = lambda i: (i, 0)
    fix = lambda i: (0, 0)
    return pl.pallas_call(
        _out_ffn_in_kernel,
        out_shape=[jax.ShapeDtypeStruct((T, D_MODEL), F32),
                   jax.ShapeDtypeStruct((T, 2 * D_FF), BF16)],
        grid=(T // tm,),
        in_specs=[pl.BlockSpec((tm, D_MODEL), row),
                  pl.BlockSpec((tm, D_AB), row),
                  pl.BlockSpec((tm, D_HYENA), row),
                  pl.BlockSpec((1, D_HYENA), fix),
                  pl.BlockSpec((D_AB, D_MODEL), fix),
                  pl.BlockSpec((D_HYENA, D_MODEL), fix),
                  pl.BlockSpec((1, D_MODEL), fix),
                  pl.BlockSpec((D_MODEL, 2 * D_FF), fix)],
        out_specs=[pl.BlockSpec((tm, D_MODEL), row),
                   pl.BlockSpec((tm, 2 * D_FF), row)],
        compiler_params=_params(1),
        name="out_ffn_in",
    )(x2, cab, c2, g_c, w_ab, w_c, g2, w_in)


def _ffn_out_kernel(u_ref, up_ref, un_ref, xn_ref, cw_ref, cb_ref, wo_ref, o_ref, acc_ref, *, tiles_per_seq):
    i = pl.program_id(0)
    is_first = (i % tiles_per_seq) == 0
    is_last = (i % tiles_per_seq) == tiles_per_seq - 1
    tm = u_ref.shape[0]
    row = lax.broadcasted_iota(jnp.int32, (tm, 1), 0)

    def conv(c0):
        cs = slice(c0, c0 + FFN_CHUNK)
        uc = u_ref[:, cs].astype(F32)
        before = jnp.where(is_first, 0.0, up_ref[:, cs].astype(F32)[BF16_ROWS - 1:BF16_ROWS, :])
        after = jnp.where(is_last, 0.0, un_ref[:, cs].astype(F32)[0:1, :])
        um = jnp.where(row == 0, before, pltpu.roll(uc, 1, axis=0))
        up = jnp.where(row == tm - 1, after, pltpu.roll(uc, tm - 1, axis=0))
        return um * cw_ref[0:1, cs] + uc * cw_ref[1:2, cs] + up * cw_ref[2:3, cs] + cb_ref[:, cs]

    for j in range(D_FF // FFN_CHUNK):
        gate = conv(j * FFN_CHUNK)
        val = conv(D_FF + j * FFN_CHUNK)
        act = 0.5 * gate * (1.0 + lax.erf(gate * math.sqrt(0.5))) * val
        part = _dot(act.astype(BF16), wo_ref[j * FFN_CHUNK:(j + 1) * FFN_CHUNK, :])
        if j == 0:
            acc_ref[...] = part
        else:
            acc_ref[...] += part
    o_ref[...] = xn_ref[...] + acc_ref[...]


def _ffn_out(u, xn, cw, cb, w_out, L):
    T = xn.shape[0]
    tm = ROW_TILE
    hpt = tm // BF16_ROWS
    n16 = T // BF16_ROWS
    row = lambda i: (i, 0)
    fix = lambda i: (0, 0)
    return pl.pallas_call(
        functools.partial(_ffn_out_kernel, tiles_per_seq=L // tm),
        out_shape=jax.ShapeDtypeStruct((T, D_MODEL), F32),
        grid=(T // tm,),
        in_specs=[pl.BlockSpec((tm, 2 * D_FF), row),
                  pl.BlockSpec((BF16_ROWS, 2 * D_FF), lambda i: (jnp.maximum(i * hpt - 1, 0), 0)),
                  pl.BlockSpec((BF16_ROWS, 2 * D_FF), lambda i: (jnp.minimum((i + 1) * hpt, n16 - 1), 0)),
                  pl.BlockSpec((tm, D_MODEL), row),
                  pl.BlockSpec((3, 2 * D_FF), fix),
                  pl.BlockSpec((1, 2 * D_FF), fix),
                  pl.BlockSpec((D_FF, D_MODEL), fix)],
        out_specs=pl.BlockSpec((tm, D_MODEL), row),
        scratch_shapes=[pltpu.VMEM((tm, D_MODEL), F32)],
        compiler_params=_params(1),
        name="ffn_out",
    )(u, u, u, xn, cw, cb, w_out)


def _head_ones(n_heads):
    return jnp.kron(jnp.eye(n_heads, dtype=F32), jnp.ones((HEAD_DIM, HEAD_DIM), F32)).astype(BF16)


def _layer(x2, B, L, mats, spectra, lw):
    q, kv, ph = _in_proj(x2, lw["norm1_g"], lw["w_in"], lw["eq"], lw["ek"], lw["qg"], lw["kg"])
    cab, hy = _mix(q, kv, ph, lw["sink"], lw["pw_bd"], lw["pool_scale"], lw["hy_w"], lw["hy_b"],
                   lw["g_a"], lw["g_b"], B, L)
    hy4 = hy.reshape(B, L // FFT_N2, FFT_N2, D_HY_IN)
    tb = math.gcd(B, 4)
    c4 = _hyena(hy4, spectra[0], spectra[1], lw["hy_bias"], mats, tb)
    xn, u = _out_ffn_in(x2, cab, c4.reshape(B * L, D_HYENA), lw["g_c"], lw["w_ab"], lw["w_c"],
                        lw["norm2_g"], lw["w_ffn_in"])
    return _ffn_out(u, xn, lw["ffn_cw"], lw["ffn_cb"], lw["w_ffn_out"], L)


def kernel(x_prompt, x_sample, norm1_g, w_in, q_norm_g, k_norm_g, attn_sink, pool_w, pool_scale, hy_conv_w,
           hy_conv_b, filt_w1, filt_b1, filt_freq1, filt_w2, filt_b2, filt_freq2, filt_w3, hy_bias, out_norm_g,
           w_out, norm2_g, w_ffn_in, ffn_conv_w, ffn_conv_b, w_ffn_out):
    groups = [x_prompt, x_sample]
    lens = sorted({g.shape[1] for g in groups})
    mats = {L: _dft_mats(L) for L in lens}
    eq, ek = _head_ones(N_Q_HEADS), _head_ones(N_KV_HEADS)
    xs = [g.reshape(-1, D_MODEL) for g in groups]
    for l in range(DEPTH):
        lw = dict(
            norm1_g=norm1_g[l][None, :], w_in=w_in[l].astype(BF16), eq=eq, ek=ek,
            qg=jnp.tile(q_norm_g[l], N_Q_HEADS)[None, :], kg=jnp.tile(k_norm_g[l], N_KV_HEADS)[None, :],
            sink=attn_sink[l],
            pw_bd=jax.scipy.linalg.block_diag(*[pool_w[l, g] for g in range(len(POOL_WINDOWS))]).astype(BF16),
            pool_scale=pool_scale[l][None, :], hy_w=hy_conv_w[l], hy_b=hy_conv_b[l][None, :],
            g_a=out_norm_g[l, :D_ATTN][None, :], g_b=out_norm_g[l, D_ATTN:D_AB][None, :],
            g_c=out_norm_g[l, D_AB:][None, :], hy_bias=hy_bias[l],
            w_ab=w_out[l, :D_AB].astype(BF16), w_c=w_out[l, D_AB:].astype(BF16),
            norm2_g=norm2_g[l][None, :], w_ffn_in=w_ffn_in[l].astype(BF16),
            ffn_cw=ffn_conv_w[l], ffn_cb=ffn_conv_b[l][None, :], w_ffn_out=w_ffn_out[l].astype(BF16),
        )
        spectra = {}
        for L in lens:
            filt = _filters(L, filt_w1[l], filt_b1[l], filt_freq1[l], filt_w2[l], filt_b2[l], filt_freq2[l],
                            filt_w3[l])
            spectra[L] = _filter_spectra(filt, mats[L], L)
        xs = [_layer(x2, g.shape[0], g.shape[1], mats[g.shape[1]], spectra[g.shape[1]], lw)
              for x2, g in zip(xs, groups)]
    return tuple(x2.reshape(g.shape) for x2, g in zip(xs, groups))
```

```python
import functools
import math

import jax
import jax.numpy as jnp
from jax import lax
from jax.experimental import pallas as pl
from jax.experimental.pallas import tpu as pltpu

F32 = jnp.float32
BF16 = jnp.bfloat16

D_MODEL = 1024
DEPTH = 2
HEAD_DIM = 64
N_Q_HEADS = 8
N_KV_HEADS = 2
GQA_GROUP = N_Q_HEADS // N_KV_HEADS
D_ATTN = N_Q_HEADS * HEAD_DIM
D_KV = N_KV_HEADS * HEAD_DIM
WINDOW = 128
BLOCK = 128
POOL_WINDOWS = (2, 4, 8, 16)
D_POOL = 256
POOL_GROUP_DIM = D_POOL // len(POOL_WINDOWS)
D_HYENA = 256
HYENA_ORDER = 2
FILTER_BANDS = 16
FILTER_EMB = 1 + 2 * FILTER_BANDS
FILTER_HIDDEN = 64
N_FILTERS = 2 * HYENA_ORDER
DECAY_FAST_PCT = 0.3
DECAY_SLOW_PCT = 1.5
DECAY_TARGET = 1e-2
D_HY_IN = (HYENA_ORDER + 1) * D_HYENA
D_PH = D_POOL + D_HY_IN
D_IN_PROJ = D_ATTN + 2 * D_KV + D_PH
D_CAT = D_ATTN + D_POOL + D_HYENA
D_AB = D_ATTN + D_POOL
D_FF = 2816
EPS = 1e-6
NEG_INF = -1e30

LANES = 128
SUBLANES = 8
BF16_ROWS = 16
FFT_N2 = 128
FFT_J = SUBLANES
FFT_TF = 8
FFT_JH = FFT_N2 // FFT_J
HY_CT = D_HYENA // LANES
FFN_CHUNK = 256
FFN_IN_CHUNK = 512
ROW_TILE = 256
VMEM_LIMIT = 48 * 1024 * 1024


def _params(n_axes):
    return pltpu.CompilerParams(dimension_semantics=("arbitrary",) * n_axes, vmem_limit_bytes=VMEM_LIMIT)


def _rms(x, g):
    return x * lax.rsqrt(jnp.mean(x * x, axis=-1, keepdims=True) + EPS) * g


def _dot(a, b):
    return jnp.dot(a, b, preferred_element_type=F32)


def _store_tiles(ref, lead, row0, val):
    for nl in range(val.shape[0] // FFT_N2):
        for jh in range(FFT_JH):
            for ct in range(val.shape[1] // LANES):
                r0 = nl * FFT_N2 + jh * FFT_J
                ref[lead + (jh, ct, pl.ds(row0 + nl * FFT_J, FFT_J), slice(None))] = (
                    val[r0:r0 + FFT_J, ct * LANES:(ct + 1) * LANES])


def _load_tiles(ref, lead, row0, n_local, n_ct):
    return jnp.concatenate(
        [jnp.concatenate([ref[lead + (jh, ct, pl.ds(row0 + nl * FFT_J, FFT_J), slice(None))]
                          for nl in range(n_local) for jh in range(FFT_JH)], axis=0)
         for ct in range(n_ct)], axis=1)


def _in_proj_kernel(x_ref, g_ref, w_ref, eq_ref, ek_ref, qg_ref, kg_ref, q_ref, kv_ref, ph_ref):
    h = _rms(x_ref[...], g_ref[...]).astype(BF16)
    p = _dot(h, w_ref[...])

    def head_norm(t, e_ref, gain):
        tt = t * t
        hi = tt.astype(BF16)
        lo = (tt - hi.astype(F32)).astype(BF16)
        ss = _dot(hi, e_ref[...]) + _dot(lo, e_ref[...])
        return t * lax.rsqrt(ss * (1.0 / HEAD_DIM) + EPS) * gain

    o1 = D_ATTN + D_KV
    o2 = o1 + D_KV
    q = head_norm(p[:, :D_ATTN], eq_ref, qg_ref[...]) * (1.0 / math.sqrt(HEAD_DIM))
    k = head_norm(p[:, D_ATTN:o1], ek_ref, kg_ref[...])
    q_ref[...] = q.astype(BF16)
    kv_ref[:, :D_KV] = k.astype(BF16)
    kv_ref[:, D_KV:] = p[:, o1:o2].astype(BF16)
    ph_ref[...] = p[:, o2:]


def _in_proj(x2, g, w, eq, ek, qg, kg):
    T = x2.shape[0]
    tm = ROW_TILE
    row = lambda i: (i, 0)
    fix = lambda i: (0, 0)
    return pl.pallas_call(
        _in_proj_kernel,
        out_shape=[jax.ShapeDtypeStruct((T, D_ATTN), BF16),
                   jax.ShapeDtypeStruct((T, 2 * D_KV), BF16),
                   jax.ShapeDtypeStruct((T, D_PH), F32)],
        grid=(T // tm,),
        in_specs=[pl.BlockSpec((tm, D_MODEL), row),
                  pl.BlockSpec((1, D_MODEL), fix),
                  pl.BlockSpec((D_MODEL, D_IN_PROJ), fix),
                  pl.BlockSpec((D_ATTN, D_ATTN), fix),
                  pl.BlockSpec((D_KV, D_KV), fix),
                  pl.BlockSpec((1, D_ATTN), fix),
                  pl.BlockSpec((1, D_KV), fix)],
        out_specs=[pl.BlockSpec((tm, D_ATTN), row),
                   pl.BlockSpec((tm, 2 * D_KV), row),
                   pl.BlockSpec((tm, D_PH), row)],
        compiler_params=_params(1),
        name="in_proj",
    )(x2, g, w, eq, ek, qg, kg)


def _mix_kernel(sink_ref, q_ref, kvp_ref, kvc_ref, kvn_ref, phc_ref, php_ref, phn_ref,
                pw_ref, ps_ref, hw_ref, hb_ref, ga_ref, gb_ref,
                cat_ref, hy_ref, ext_ref, att_ref, *, seq_len):
    l = pl.program_id(1)
    is_first = l == 0
    is_last = l == pl.num_programs(1) - 1

    row = lax.broadcasted_iota(jnp.int32, (BLOCK, 3 * BLOCK), 0)
    col = lax.broadcasted_iota(jnp.int32, (BLOCK, 3 * BLOCK), 1)
    dist = jnp.abs(row + BLOCK - col)
    valid = dist <= WINDOW
    valid = valid & ((col >= BLOCK) | jnp.logical_not(is_first))
    valid = valid & ((col < 2 * BLOCK) | jnp.logical_not(is_last))
    distf = dist.astype(F32)
    for h in range(N_Q_HEADS):
        g = h // GQA_GROUP
        ks = slice(g * HEAD_DIM, (g + 1) * HEAD_DIM)
        vs = slice(D_KV + g * HEAD_DIM, D_KV + (g + 1) * HEAD_DIM)
        qh = q_ref[:, h * HEAD_DIM:(h + 1) * HEAD_DIM]
        kh = jnp.concatenate([kvp_ref[:, ks], kvc_ref[:, ks], kvn_ref[:, ks]], axis=0)
        vh = jnp.concatenate([kvp_ref[:, vs], kvc_ref[:, vs], kvn_ref[:, vs]], axis=0)
        s = lax.dot_general(qh, kh, (((1,), (1,)), ((), ())), preferred_element_type=F32)
        s = s - (2.0 ** (-(h + 1))) * distf
        s = jnp.where(valid, s, NEG_INF)
        sk = sink_ref[h]
        m = jnp.maximum(jnp.max(s, axis=-1, keepdims=True), sk)
        e = jnp.exp(s - m)
        den = jnp.sum(e, axis=-1, keepdims=True) + jnp.exp(sk - m)
        att_ref[:, h * HEAD_DIM:(h + 1) * HEAD_DIM] = _dot(e.astype(BF16), vh) / den
    cat_ref[:, :D_ATTN] = _rms(att_ref[...], ga_ref[...]).astype(BF16)

    ext_ref[0:SUBLANES, :] = jnp.where(is_first, 0.0, php_ref[...])
    ext_ref[SUBLANES:SUBLANES + BLOCK, :] = phc_ref[...]
    ext_ref[SUBLANES + BLOCK:, :] = jnp.where(is_last, 0.0, phn_ref[...])

    def sh(d):
        return ext_ref[SUBLANES + d:SUBLANES + d + BLOCK, 0:D_POOL]

    u = sh(0)
    s2 = sh(-1) + u
    s4 = s2 + sh(-2) + sh(1)
    s8 = s4 + sh(-4) + sh(-3) + sh(2) + sh(3)
    s16 = s8 + sh(-8) + sh(-7) + sh(-6) + sh(-5) + sh(4) + sh(5) + sh(6) + sh(7)
    t = l * BLOCK + lax.broadcasted_iota(jnp.int32, (BLOCK, 1), 0)

    def count(w):
        lo = jnp.clip(t - w // 2, 0, seq_len)
        hi = jnp.clip(t - w // 2 + w, 0, seq_len)
        return (hi - lo).astype(F32)

    lane = lax.broadcasted_iota(jnp.int32, (1, D_POOL), 1)
    mean = jnp.where(lane < POOL_GROUP_DIM, s2 / count(2),
                     jnp.where(lane < 2 * POOL_GROUP_DIM, s4 / count(4),
                               jnp.where(lane < 3 * POOL_GROUP_DIM, s8 / count(8), s16 / count(16))))
    pooled = _dot((mean - u).astype(BF16), pw_ref[...]) * ps_ref[...]
    cat_ref[:, D_ATTN:] = _rms(pooled, gb_ref[...]).astype(BF16)

    hp = ext_ref[SUBLANES - 1:SUBLANES - 1 + BLOCK, D_POOL:]
    hc = ext_ref[SUBLANES:SUBLANES + BLOCK, D_POOL:]
    hn = ext_ref[SUBLANES + 1:SUBLANES + 1 + BLOCK, D_POOL:]
    _store_tiles(hy_ref, (), 0, hp * hw_ref[0:1, :] + hc * hw_ref[1:2, :] + hn * hw_ref[2:3, :] + hb_ref[...])


def _mix(q, kv, ph, sink, pw_bd, pool_scale, hy_w, hy_b, g_a, g_b, B, L):
    T = B * L
    nb = L // BLOCK
    rpb = BLOCK // SUBLANES
    n8 = T // SUBLANES
    cur = lambda b, l: (b * nb + l, 0)
    prev = lambda b, l: (b * nb + jnp.maximum(l - 1, 0), 0)
    nxt = lambda b, l: (b * nb + jnp.minimum(l + 1, nb - 1), 0)
    prev8 = lambda b, l: (jnp.maximum((b * nb + l) * rpb - 1, 0), 0)
    next8 = lambda b, l: (jnp.minimum((b * nb + l + 1) * rpb, n8 - 1), 0)
    fix = lambda b, l: (0, 0)
    return pl.pallas_call(
        functools.partial(_mix_kernel, seq_len=L),
        out_shape=[jax.ShapeDtypeStruct((T, D_AB), BF16),
                   jax.ShapeDtypeStruct((B, FFT_JH, D_HY_IN // LANES, nb * FFT_J, LANES), F32)],
        grid=(B, nb),
        in_specs=[pl.BlockSpec(memory_space=pltpu.SMEM),
                  pl.BlockSpec((BLOCK, D_ATTN), cur),
                  pl.BlockSpec((BLOCK, 2 * D_KV), prev),
                  pl.BlockSpec((BLOCK, 2 * D_KV), cur),
                  pl.BlockSpec((BLOCK, 2 * D_KV), nxt),
                  pl.BlockSpec((BLOCK, D_PH), cur),
                  pl.BlockSpec((SUBLANES, D_PH), prev8),
                  pl.BlockSpec((SUBLANES, D_PH), next8),
                  pl.BlockSpec((D_POOL, D_POOL), fix),
                  pl.BlockSpec((1, D_POOL), fix),
                  pl.BlockSpec((3, D_HY_IN), fix),
                  pl.BlockSpec((1, D_HY_IN), fix),
                  pl.BlockSpec((1, D_ATTN), fix),
                  pl.BlockSpec((1, D_POOL), fix)],
        out_specs=[pl.BlockSpec((BLOCK, D_AB), cur),
                   pl.BlockSpec((None, FFT_JH, D_HY_IN // LANES, FFT_J, LANES), lambda b, l: (b, 0, 0, l, 0))],
        scratch_shapes=[pltpu.VMEM((BLOCK + 2 * SUBLANES, D_PH), F32),
                        pltpu.VMEM((BLOCK, D_ATTN), F32)],
        compiler_params=_params(2),
        name="mix",
    )(sink, q, kv, kv, kv, ph, ph, ph, pw_bd, pool_scale, hy_w, hy_b, g_a, g_b)


def _filter_kernel(z_ref, w1_ref, b1_ref, f1_ref, w2_ref, b2_ref, f2_ref, w3_ref, dl_ref, o_ref, *, seq_len):
    hp = lax.Precision.HIGHEST
    z = z_ref[...]
    h = jnp.sin(f1_ref[...] * (jnp.dot(z, w1_ref[...], precision=hp, preferred_element_type=F32) + b1_ref[...]))
    h = jnp.sin(f2_ref[...] * (jnp.dot(h, w2_ref[...], precision=hp, preferred_element_type=F32) + b2_ref[...]))
    h = jnp.dot(h, w3_ref[...], precision=hp, preferred_element_type=F32)
    n = pl.program_id(0) * z.shape[0] + lax.broadcasted_iota(jnp.int32, (z.shape[0], 1), 0)
    decay = jnp.where(n == seq_len, 0.0, jnp.exp(-z[:, 0:1] * dl_ref[...]))
    for o in range(HYENA_ORDER):
        _store_tiles(o_ref, (o,), 0, h[:, o * D_HYENA:(o + 1) * D_HYENA] * decay)


def _filter_kernels(L, w1, b1, fr1, w2, b2, fr2, w3):
    t_norm = jnp.linspace(0.0, 1.0, L, dtype=F32)[:, None]
    n = jnp.arange(L, dtype=F32)[:, None]
    bands = jnp.linspace(1e-4, FILTER_BANDS - 1, FILTER_BANDS, dtype=F32)[None, :]
    ang = 2.0 * math.pi * n * bands / L
    z = jnp.concatenate([t_norm, jnp.cos(ang), -jnp.sin(ang)], axis=-1)
    zp = jnp.pad(z, ((0, 0), (0, LANES - FILTER_EMB)))
    pos = jnp.arange(2 * L, dtype=jnp.int32)
    z2 = zp[jnp.where(pos < L, pos, (2 * L - pos) % L)]
    w1p = jnp.pad(w1, ((0, LANES - FILTER_EMB), (0, 0)))
    w3d = w3.reshape(FILTER_HIDDEN, HYENA_ORDER, 2, D_HYENA).transpose(2, 0, 1, 3).reshape(
        2, FILTER_HIDDEN, HYENA_ORDER * D_HYENA)
    max_decay = math.log(DECAY_TARGET) / DECAY_FAST_PCT
    min_decay = math.log(DECAY_TARGET) / DECAY_SLOW_PCT
    dl = jnp.abs(jnp.linspace(min_decay, max_decay, D_HYENA, dtype=F32))[None, :]
    tl = 512
    half = L // tl
    row = lambda i: (i, 0)
    fix = lambda i: (0, 0)
    return pl.pallas_call(
        functools.partial(_filter_kernel, seq_len=L),
        out_shape=jax.ShapeDtypeStruct((HYENA_ORDER, FFT_JH, HY_CT, 2 * L // FFT_N2 * FFT_J, LANES), F32),
        grid=(2 * L // tl,),
        in_specs=[pl.BlockSpec((tl, LANES), row),
                  pl.BlockSpec((LANES, FILTER_HIDDEN), fix),
                  pl.BlockSpec((1, FILTER_HIDDEN), fix),
                  pl.BlockSpec((1, FILTER_HIDDEN), fix),
                  pl.BlockSpec((FILTER_HIDDEN, FILTER_HIDDEN), fix),
                  pl.BlockSpec((1, FILTER_HIDDEN), fix),
                  pl.BlockSpec((1, FILTER_HIDDEN), fix),
                  pl.BlockSpec((None, FILTER_HIDDEN, HYENA_ORDER * D_HYENA), lambda i: (i // half, 0, 0)),
                  pl.BlockSpec((1, D_HYENA), fix)],
        out_specs=pl.BlockSpec((HYENA_ORDER, FFT_JH, HY_CT, tl // FFT_N2 * FFT_J, LANES), lambda i: (0, 0, 0, i, 0)),
        compiler_params=_params(1),
        name="hyena_filters",
    )(z2, w1p, b1[None, :], fr1[None, :], w2, b2[None, :], fr2[None, :], w3d, dl)


def _dft_mats(L):
    N = 2 * L
    n1n = N // FFT_N2
    r = n1n // 2
    assert r % FFT_TF == 0, "sequence length must be a multiple of FFT_TF * FFT_N2 / 2"
    mh = r + FFT_TF
    f1 = jnp.arange(mh, dtype=jnp.int32)[:, None]
    n1 = jnp.arange(n1n, dtype=jnp.int32)[None, :]
    keep = (f1 <= r).astype(F32)
    th = (2.0 * math.pi / n1n) * ((f1 * n1) % n1n).astype(F32)
    c1, s1 = keep * jnp.cos(th), keep * jnp.sin(th)
    fwd_full = jnp.concatenate([c1, -s1], axis=0).astype(BF16)
    fwd_half = fwd_full[:, :r]
    w = jnp.where((f1 == 0) | (f1 == r), 1.0, 2.0)
    inv_half = jnp.concatenate([(w * c1)[:, :r].T, (-w * s1)[:, :r].T], axis=1).astype(BF16)
    i2 = jnp.arange(2 * FFT_N2, dtype=jnp.int32)
    part, idx = i2 // FFT_N2, i2 % FFT_N2
    fq = jnp.arange(mh, dtype=jnp.int32)[:, None, None]

    def rot(f2, n2, dpart):
        m = ((fq + n1n * f2) * n2 + dpart * (N // 4)) % N
        return jnp.cos((2.0 * math.pi / N) * m.astype(F32)).astype(BF16)

    g = rot(idx[None, :, None], idx[None, None, :], part[None, :, None] - part[None, None, :])
    ginv = rot(idx[None, None, :], idx[None, :, None], part[None, None, :] - part[None, :, None])
    return dict(n1=n1n, r=r, mh=mh, fwd_full=fwd_full, fwd_half=fwd_half, inv_half=inv_half, g=g, ginv=ginv)


def _fft_a_kernel(x_ref, f_ref, ar_ref, ai_ref, *, tb, mh):
    f = f_ref[...]
    r = x_ref.shape[1] // FFT_J
    for b in range(tb):
        for j in range(FFT_J):
            a = _dot(f, x_ref[b, pl.ds(j, r, stride=FFT_J), :].astype(BF16))
            ar_ref[b, pl.ds(j, mh, stride=FFT_J), :] = a[:mh]
            ai_ref[b, pl.ds(j, mh, stride=FFT_J), :] = a[mh:]


def _fft_a(x5, coff, fmat, mh, tb):
    B, rj = x5.shape[0], x5.shape[3]
    blk = lambda b, j, c: (b, j, c, 0, 0)
    out = jax.ShapeDtypeStruct((B, FFT_JH, HY_CT, mh * FFT_J, LANES), F32)
    return pl.pallas_call(
        functools.partial(_fft_a_kernel, tb=tb, mh=mh),
        out_shape=[out, out],
        grid=(B // tb, FFT_JH, HY_CT),
        in_specs=[pl.BlockSpec((tb, None, None, rj, LANES), lambda b, j, c: (b, j, c + coff, 0, 0)),
                  pl.BlockSpec((2 * mh, rj // FFT_J), lambda b, j, c: (0, 0))],
        out_specs=[pl.BlockSpec((tb, None, None, mh * FFT_J, LANES), blk)] * 2,
        compiler_params=_params(3),
        name="fft_outer_fwd",
    )(x5, fmat)


def _fft_spec_kernel(ar_ref, ai_ref, g_ref, kr_ref, ki_ref, *, scale):
    for f in range(FFT_TF):
        a = jnp.concatenate([_load_tiles(ar_ref, (), f * FFT_J, 1, HY_CT),
                             _load_tiles(ai_ref, (), f * FFT_J, 1, HY_CT)], axis=0).astype(BF16)
        x = _dot(g_ref[f], a) * scale
        kr_ref[0, f] = x[:FFT_N2]
        ki_ref[0, f] = x[FFT_N2:]


def _tile_spec():
    return pl.BlockSpec((None, FFT_JH, HY_CT, FFT_TF * FFT_J, LANES), lambda f, b: (b, 0, 0, f, 0))


def _fft_spec(ar, ai, g, scale):
    B, mh = ar.shape[0], ar.shape[3] // FFT_J
    nat = pl.BlockSpec((1, FFT_TF, FFT_N2, D_HYENA), lambda f, b: (b, f, 0, 0))
    out = jax.ShapeDtypeStruct((B, mh, FFT_N2, D_HYENA), F32)
    return pl.pallas_call(
        functools.partial(_fft_spec_kernel, scale=scale),
        out_shape=[out, out],
        grid=(mh // FFT_TF, B),
        in_specs=[_tile_spec(), _tile_spec(),
                  pl.BlockSpec((FFT_TF, 2 * FFT_N2, 2 * FFT_N2), lambda f, b: (f, 0, 0))],
        out_specs=[nat, nat],
        compiler_params=_params(2),
        name="fft_inner_fwd",
    )(ar, ai, g)


def _fft_b_kernel(ar_ref, ai_ref, g_ref, gi_ref, kr_ref, ki_ref, yr_ref, yi_ref):
    for f in range(FFT_TF):
        a = jnp.concatenate([_load_tiles(ar_ref, (), f * FFT_J, 1, HY_CT),
                             _load_tiles(ai_ref, (), f * FFT_J, 1, HY_CT)], axis=0).astype(BF16)
        x = _dot(g_ref[f], a)
        xr, xi = x[:FFT_N2], x[FFT_N2:]
        kr, ki = kr_ref[0, f], ki_ref[0, f]
        z = jnp.concatenate([xr * kr - xi * ki, xr * ki + xi * kr], axis=0).astype(BF16)
        y = _dot(gi_ref[f], z)
        _store_tiles(yr_ref, (), f * FFT_J, y[:FFT_N2])
        _store_tiles(yi_ref, (), f * FFT_J, y[FFT_N2:])


def _fft_b(ar, ai, g, ginv, kr, ki, o):
    B, mh = ar.shape[0], ar.shape[3] // FFT_J
    mat = pl.BlockSpec((FFT_TF, 2 * FFT_N2, 2 * FFT_N2), lambda f, b: (f, 0, 0))
    spec = pl.BlockSpec((1, FFT_TF, FFT_N2, D_HYENA), lambda f, b: (o, f, 0, 0))
    out = jax.ShapeDtypeStruct(ar.shape, F32)
    return pl.pallas_call(
        _fft_b_kernel,
        out_shape=[out, out],
        grid=(mh // FFT_TF, B),
        in_specs=[_tile_spec(), _tile_spec(), mat, mat, spec, spec],
        out_specs=[_tile_spec(), _tile_spec()],
        compiler_params=_params(2),
        name="fft_inner",
    )(ar, ai, g, ginv, kr, ki)


def _fft_c_kernel(yr_ref, yi_ref, c_ref, u_ref, gate_ref, bias_ref, o_ref, *, tb):
    cm = c_ref[...]
    bias = bias_ref[...]
    mh, r = yr_ref.shape[1] // FFT_J, u_ref.shape[1] // FFT_J
    for b in range(tb):
        for j in range(FFT_J):
            yy = jnp.concatenate([yr_ref[b, pl.ds(j, mh, stride=FFT_J), :],
                                  yi_ref[b, pl.ds(j, mh, stride=FFT_J), :]], axis=0).astype(BF16)
            y = _dot(cm, yy)
            rows = pl.ds(j, r, stride=FFT_J)
            o_ref[b, rows, :] = gate_ref[b, rows, :] * (y + bias * u_ref[b, rows, :])


def _fft_c(yr, yi, cmat, u5, uoff, gate5, goff, bias, tb):
    B, mhj = yr.shape[0], yr.shape[3]
    rj = cmat.shape[0] * FFT_J
    blk = lambda b, j, c: (b, j, c, 0, 0)
    return pl.pallas_call(
        functools.partial(_fft_c_kernel, tb=tb),
        out_shape=jax.ShapeDtypeStruct((B, FFT_JH, HY_CT, rj, LANES), F32),
        grid=(B // tb, FFT_JH, HY_CT),
        in_specs=[pl.BlockSpec((tb, None, None, mhj, LANES), blk),
                  pl.BlockSpec((tb, None, None, mhj, LANES), blk),
                  pl.BlockSpec(cmat.shape, lambda b, j, c: (0, 0)),
                  pl.BlockSpec((tb, None, None, rj, LANES), lambda b, j, c: (b, j, c + uoff, 0, 0)),
                  pl.BlockSpec((tb, None, None, rj, LANES), lambda b, j, c: (b, j, c + goff, 0, 0)),
                  pl.BlockSpec((1, LANES), lambda b, j, c: (0, c))],
        out_specs=pl.BlockSpec((tb, None, None, rj, LANES), blk),
        compiler_params=_params(3),
        name="fft_outer_inv",
    )(yr, yi, cmat, u5, gate5, bias)


def _filter_spectra(kern, mats, L):
    ar, ai = _fft_a(kern, 0, mats["fwd_full"], mats["mh"], 1)
    return _fft_spec(ar, ai, mats["g"], 1.0 / (2 * L))


def _hyena(hy5, kr, ki, hy_bias, mats, tb):
    z5 = hy5
    for o in range(HYENA_ORDER):
        ar, ai = _fft_a(z5, 0, mats["fwd_half"], mats["mh"], tb)
        yr, yi = _fft_b(ar, ai, mats["g"], mats["ginv"], kr, ki, o)
        z5 = _fft_c(yr, yi, mats["inv_half"], z5, 0, hy5, (o + 1) * HY_CT, hy_bias[o][None, :], tb)
    return z5


def _out_ffn_in_kernel(x_ref, cab_ref, c_ref, gc_ref, wab_ref, wc_ref, g2_ref, wi_ref, xn_ref, u_ref):
    c = _load_tiles(c_ref, (), 0, c_ref.shape[2] // FFT_J, HY_CT)
    cn = _rms(c, gc_ref[...]).astype(BF16)
    xn = x_ref[...] + _dot(cab_ref[...], wab_ref[...]) + _dot(cn, wc_ref[...])
    xn_ref[...] = xn
    h = _rms(xn, g2_ref[...]).astype(BF16)
    for j in range(0, 2 * D_FF, FFN_IN_CHUNK):
        u_ref[:, j:j + FFN_IN_CHUNK] = _dot(h, wi_ref[:, j:j + FFN_IN_CHUNK]).astype(BF16)


def _out_ffn_in(x2, cab, c5, g_c, w_ab, w_c, g2, w_in):
    T = x2.shape[0]
    tm = ROW_TILE
    tps = c5.shape[3] // FFT_J * FFT_N2 // tm
    row = lambda i: (i, 0)
    fix = lambda i: (0, 0)
    return pl.pallas_call(
        _out_ffn_in_kernel,
        out_shape=[jax.ShapeDtypeStruct((T, D_MODEL), F32),
                   jax.ShapeDtypeStruct((T, 2 * D_FF), BF16)],
        grid=(T // tm,),
        in_specs=[pl.BlockSpec((tm, D_MODEL), row),
                  pl.BlockSpec((tm, D_AB), row),
                  pl.BlockSpec((None, FFT_JH, HY_CT, tm // FFT_N2 * FFT_J, LANES),
                               lambda i: (i // tps, 0, 0, i % tps, 0)),
                  pl.BlockSpec((1, D_HYENA), fix),
                  pl.BlockSpec((D_AB, D_MODEL), fix),
                  pl.BlockSpec((D_HYENA, D_MODEL), fix),
                  pl.BlockSpec((1, D_MODEL), fix),
                  pl.BlockSpec((D_MODEL, 2 * D_FF), fix)],
        out_specs=[pl.BlockSpec((tm, D_MODEL), row),
                   pl.BlockSpec((tm, 2 * D_FF), row)],
        compiler_params=_params(1),
        name="out_ffn_in",
    )(x2, cab, c5, g_c, w_ab, w_c, g2, w_in)


def _ffn_out_kernel(u_ref, up_ref, un_ref, xn_ref, cw_ref, cb_ref, wo_ref, o_ref, acc_ref, *, tiles_per_seq):
    i = pl.program_id(0)
    is_first = (i % tiles_per_seq) == 0
    is_last = (i % tiles_per_seq) == tiles_per_seq - 1
    tm = u_ref.shape[0]
    row = lax.broadcasted_iota(jnp.int32, (tm, 1), 0)

    def conv(c0):
        cs = slice(c0, c0 + FFN_CHUNK)
        uc = u_ref[:, cs].astype(F32)
        before = jnp.where(is_first, 0.0, up_ref[:, cs].astype(F32)[BF16_ROWS - 1:BF16_ROWS, :])
        after = jnp.where(is_last, 0.0, un_ref[:, cs].astype(F32)[0:1, :])
        um = jnp.where(row == 0, before, pltpu.roll(uc, 1, axis=0))
        up = jnp.where(row == tm - 1, after, pltpu.roll(uc, tm - 1, axis=0))
        return um * cw_ref[0:1, cs] + uc * cw_ref[1:2, cs] + up * cw_ref[2:3, cs] + cb_ref[:, cs]

    for j in range(D_FF // FFN_CHUNK):
        gate = conv(j * FFN_CHUNK)
        val = conv(D_FF + j * FFN_CHUNK)
        act = 0.5 * gate * (1.0 + lax.erf(gate * math.sqrt(0.5))) * val
        part = _dot(act.astype(BF16), wo_ref[j * FFN_CHUNK:(j + 1) * FFN_CHUNK, :])
        if j == 0:
            acc_ref[...] = part
        else:
            acc_ref[...] += part
    o_ref[...] = xn_ref[...] + acc_ref[...]


def _ffn_out(u, xn, cw, cb, w_out, L):
    T = xn.shape[0]
    tm = ROW_TILE
    hpt = tm // BF16_ROWS
    n16 = T // BF16_ROWS
    row = lambda i: (i, 0)
    fix = lambda i: (0, 0)
    return pl.pallas_call(
        functools.partial(_ffn_out_kernel, tiles_per_seq=L // tm),
        out_shape=jax.ShapeDtypeStruct((T, D_MODEL), F32),
        grid=(T // tm,),
        in_specs=[pl.BlockSpec((tm, 2 * D_FF), row),
                  pl.BlockSpec((BF16_ROWS, 2 * D_FF), lambda i: (jnp.maximum(i * hpt - 1, 0), 0)),
                  pl.BlockSpec((BF16_ROWS, 2 * D_FF), lambda i: (jnp.minimum((i + 1) * hpt, n16 - 1), 0)),
                  pl.BlockSpec((tm, D_MODEL), row),
                  pl.BlockSpec((3, 2 * D_FF), fix),
                  pl.BlockSpec((1, 2 * D_FF), fix),
                  pl.BlockSpec((D_FF, D_MODEL), fix)],
        out_specs=pl.BlockSpec((tm, D_MODEL), row),
        scratch_shapes=[pltpu.VMEM((tm, D_MODEL), F32)],
        compiler_params=_params(1),
        name="ffn_out",
    )(u, u, u, xn, cw, cb, w_out)


def _head_ones(n_heads):
    return jnp.kron(jnp.eye(n_heads, dtype=F32), jnp.ones((HEAD_DIM, HEAD_DIM), F32)).astype(BF16)


def _layer(x2, B, L, mats, spectra, lw):
    q, kv, ph = _in_proj(x2, lw["norm1_g"], lw["w_in"], lw["eq"], lw["ek"], lw["qg"], lw["kg"])
    cab, hy = _mix(q, kv, ph, lw["sink"], lw["pw_bd"], lw["pool_scale"], lw["hy_w"], lw["hy_b"],
                   lw["g_a"], lw["g_b"], B, L)
    tb = math.gcd(B, 8)
    c5 = _hyena(hy, spectra[0], spectra[1], lw["hy_bias"], mats, tb)
    xn, u = _out_ffn_in(x2, cab, c5, lw["g_c"], lw["w_ab"], lw["w_c"],
                        lw["norm2_g"], lw["w_ffn_in"])
    return _ffn_out(u, xn, lw["ffn_cw"], lw["ffn_cb"], lw["w_ffn_out"], L)


def kernel(x_prompt, x_sample, norm1_g, w_in, q_norm_g, k_norm_g, attn_sink, pool_w, pool_scale, hy_conv_w,
           hy_conv_b, filt_w1, filt_b1, filt_freq1, filt_w2, filt_b2, filt_freq2, filt_w3, hy_bias, out_norm_g,
           w_out, norm2_g, w_ffn_in, ffn_conv_w, ffn_conv_b, w_ffn_out):
    groups = [x_prompt, x_sample]
    lens = sorted({g.shape[1] for g in groups})
    mats = {L: _dft_mats(L) for L in lens}
    eq, ek = _head_ones(N_Q_HEADS), _head_ones(N_KV_HEADS)
    xs = [g.reshape(-1, D_MODEL) for g in groups]
    for l in range(DEPTH):
        lw = dict(
            norm1_g=norm1_g[l][None, :], w_in=w_in[l].astype(BF16), eq=eq, ek=ek,
            qg=jnp.tile(q_norm_g[l], N_Q_HEADS)[None, :], kg=jnp.tile(k_norm_g[l], N_KV_HEADS)[None, :],
            sink=attn_sink[l],
            pw_bd=jax.scipy.linalg.block_diag(*[pool_w[l, g] for g in range(len(POOL_WINDOWS))]).astype(BF16),
            pool_scale=pool_scale[l][None, :], hy_w=hy_conv_w[l], hy_b=hy_conv_b[l][None, :],
            g_a=out_norm_g[l, :D_ATTN][None, :], g_b=out_norm_g[l, D_ATTN:D_AB][None, :],
            g_c=out_norm_g[l, D_AB:][None, :], hy_bias=hy_bias[l],
            w_ab=w_out[l, :D_AB].astype(BF16), w_c=w_out[l, D_AB:].astype(BF16),
            norm2_g=norm2_g[l][None, :], w_ffn_in=w_ffn_in[l].astype(BF16),
            ffn_cw=ffn_conv_w[l], ffn_cb=ffn_conv_b[l][None, :], w_ffn_out=w_ffn_out[l].astype(BF16),
        )
        spectra = {}
        for L in lens:
            kern = _filter_kernels(L, filt_w1[l], filt_b1[l], filt_freq1[l], filt_w2[l], filt_b2[l], filt_freq2[l],
                                   filt_w3[l])
            spectra[L] = _filter_spectra(kern, mats[L], L)
        xs = [_layer(x2, g.shape[0], g.shape[1], mats[g.shape[1]], spectra[g.shape[1]], lw)
              for x2, g in zip(xs, groups)]
    return tuple(x2.reshape(g.shape) for x2, g in zip(xs, groups))
```

```python
import functools
import math

import jax
import jax.numpy as jnp
from jax import lax
from jax.experimental import pallas as pl
from jax.experimental.pallas import tpu as pltpu

F32 = jnp.float32
BF16 = jnp.bfloat16

D_MODEL = 1024
DEPTH = 2
HEAD_DIM = 64
N_Q_HEADS = 8
N_KV_HEADS = 2
GQA_GROUP = N_Q_HEADS // N_KV_HEADS
D_ATTN = N_Q_HEADS * HEAD_DIM
D_KV = N_KV_HEADS * HEAD_DIM
WINDOW = 128
BLOCK = 128
POOL_WINDOWS = (2, 4, 8, 16)
D_POOL = 256
POOL_GROUP_DIM = D_POOL // len(POOL_WINDOWS)
D_HYENA = 256
HYENA_ORDER = 2
FILTER_BANDS = 16
FILTER_EMB = 1 + 2 * FILTER_BANDS
FILTER_HIDDEN = 64
N_FILTERS = 2 * HYENA_ORDER
DECAY_FAST_PCT = 0.3
DECAY_SLOW_PCT = 1.5
DECAY_TARGET = 1e-2
D_HY_IN = (HYENA_ORDER + 1) * D_HYENA
D_PH = D_POOL + D_HY_IN
D_IN_PROJ = D_ATTN + 2 * D_KV + D_PH
D_CAT = D_ATTN + D_POOL + D_HYENA
D_AB = D_ATTN + D_POOL
D_FF = 2816
EPS = 1e-6
NEG_INF = -1e30

LANES = 128
SUBLANES = 8
FFT_N2 = 128
FFT_J = SUBLANES
FFT_TF = 8
FFT_JH = FFT_N2 // FFT_J
HY_CT = D_HYENA // LANES
FFN_CHUNK = 256
FFN_IN_CHUNK = 512
ROW_TILE = 256
VMEM_LIMIT = 48 * 1024 * 1024


def _params(n_axes):
    return pltpu.CompilerParams(dimension_semantics=("arbitrary",) * n_axes, vmem_limit_bytes=VMEM_LIMIT)


def _rms(x, g):
    return x * lax.rsqrt(jnp.mean(x * x, axis=-1, keepdims=True) + EPS) * g


def _dot(a, b):
    return jnp.dot(a, b, preferred_element_type=F32)


def _store_tiles(ref, lead, row0, val):
    for nl in range(val.shape[0] // FFT_N2):
        for jh in range(FFT_JH):
            for ct in range(val.shape[1] // LANES):
                r0 = nl * FFT_N2 + jh * FFT_J
                ref[lead + (jh, ct, pl.ds(row0 + nl * FFT_J, FFT_J), slice(None))] = (
                    val[r0:r0 + FFT_J, ct * LANES:(ct + 1) * LANES])


def _load_tiles(ref, lead, row0, n_local, n_ct):
    return jnp.concatenate(
        [jnp.concatenate([ref[lead + (jh, ct, pl.ds(row0 + nl * FFT_J, FFT_J), slice(None))]
                          for nl in range(n_local) for jh in range(FFT_JH)], axis=0)
         for ct in range(n_ct)], axis=1)


def _in_proj_kernel(x_ref, g_ref, w_ref, eq_ref, ek_ref, qg_ref, kg_ref, q_ref, kv_ref, ph_ref):
    h = _rms(x_ref[...], g_ref[...]).astype(BF16)
    p = _dot(h, w_ref[...])

    def head_norm(t, e_ref, gain):
        tt = t * t
        hi = tt.astype(BF16)
        lo = (tt - hi.astype(F32)).astype(BF16)
        ss = _dot(hi, e_ref[...]) + _dot(lo, e_ref[...])
        return t * lax.rsqrt(ss * (1.0 / HEAD_DIM) + EPS) * gain

    o1 = D_ATTN + D_KV
    o2 = o1 + D_KV
    q = head_norm(p[:, :D_ATTN], eq_ref, qg_ref[...]) * (1.0 / math.sqrt(HEAD_DIM))
    k = head_norm(p[:, D_ATTN:o1], ek_ref, kg_ref[...])
    q_ref[...] = q.astype(BF16)
    kv_ref[:, :D_KV] = k.astype(BF16)
    kv_ref[:, D_KV:] = p[:, o1:o2].astype(BF16)
    ph_ref[...] = p[:, o2:]


def _in_proj(x2, g, w, eq, ek, qg, kg):
    T = x2.shape[0]
    tm = ROW_TILE
    row = lambda i: (i, 0)
    fix = lambda i: (0, 0)
    return pl.pallas_call(
        _in_proj_kernel,
        out_shape=[jax.ShapeDtypeStruct((T, D_ATTN), BF16),
                   jax.ShapeDtypeStruct((T, 2 * D_KV), BF16),
                   jax.ShapeDtypeStruct((T, D_PH), F32)],
        grid=(T // tm,),
        in_specs=[pl.BlockSpec((tm, D_MODEL), row),
                  pl.BlockSpec((1, D_MODEL), fix),
                  pl.BlockSpec((D_MODEL, D_IN_PROJ), fix),
                  pl.BlockSpec((D_ATTN, D_ATTN), fix),
                  pl.BlockSpec((D_KV, D_KV), fix),
                  pl.BlockSpec((1, D_ATTN), fix),
                  pl.BlockSpec((1, D_KV), fix)],
        out_specs=[pl.BlockSpec((tm, D_ATTN), row),
                   pl.BlockSpec((tm, 2 * D_KV), row),
                   pl.BlockSpec((tm, D_PH), row)],
        compiler_params=_params(1),
        name="in_proj",
    )(x2, g, w, eq, ek, qg, kg)


def _mix_kernel(sink_ref, bias_ref, q_ref, kvp_ref, kvc_ref, kvn_ref, phc_ref, php_ref, phn_ref,
                pw_ref, ps_ref, hw_ref, hb_ref, ga_ref, gb_ref,
                cat_ref, hy_ref, ext_ref, att_ref, s_ref, p_ref, den_ref, *, seq_len):
    l = pl.program_id(1)
    is_first = l == 0
    is_last = l == pl.num_programs(1) - 1

    for h in range(N_Q_HEADS):
        g = h // GQA_GROUP
        ks = slice(g * HEAD_DIM, (g + 1) * HEAD_DIM)
        kh = jnp.concatenate([kvp_ref[:, ks], kvc_ref[:, ks], kvn_ref[:, ks]], axis=0)
        s = lax.dot_general(q_ref[:, h * HEAD_DIM:(h + 1) * HEAD_DIM], kh, (((1,), (1,)), ((), ())),
                            preferred_element_type=F32)
        s_ref[h] = s + bias_ref[h]
    for h in range(N_Q_HEADS):
        s = s_ref[h]
        sk = sink_ref[h]
        m = jnp.maximum(jnp.max(s, axis=-1, keepdims=True), sk)
        e = jnp.exp(s - m)
        den_ref[h] = jnp.sum(e, axis=-1, keepdims=True) + jnp.exp(sk - m)
        p_ref[h] = e.astype(BF16)
    for h in range(N_Q_HEADS):
        g = h // GQA_GROUP
        vs = slice(D_KV + g * HEAD_DIM, D_KV + (g + 1) * HEAD_DIM)
        vh = jnp.concatenate([kvp_ref[:, vs], kvc_ref[:, vs], kvn_ref[:, vs]], axis=0)
        att_ref[:, h * HEAD_DIM:(h + 1) * HEAD_DIM] = _dot(p_ref[h], vh) / den_ref[h]
    cat_ref[:, :D_ATTN] = _rms(att_ref[...], ga_ref[...]).astype(BF16)

    ext_ref[0:SUBLANES, :] = jnp.where(is_first, 0.0, php_ref[...])
    ext_ref[SUBLANES:SUBLANES + BLOCK, :] = phc_ref[...]
    ext_ref[SUBLANES + BLOCK:, :] = jnp.where(is_last, 0.0, phn_ref[...])

    def sh(d):
        return ext_ref[SUBLANES + d:SUBLANES + d + BLOCK, 0:D_POOL]

    u = sh(0)
    s2 = sh(-1) + u
    s4 = s2 + sh(-2) + sh(1)
    s8 = s4 + sh(-4) + sh(-3) + sh(2) + sh(3)
    s16 = s8 + sh(-8) + sh(-7) + sh(-6) + sh(-5) + sh(4) + sh(5) + sh(6) + sh(7)
    t = l * BLOCK + lax.broadcasted_iota(jnp.int32, (BLOCK, 1), 0)

    def count(w):
        lo = jnp.clip(t - w // 2, 0, seq_len)
        hi = jnp.clip(t - w // 2 + w, 0, seq_len)
        return (hi - lo).astype(F32)

    lane = lax.broadcasted_iota(jnp.int32, (1, D_POOL), 1)
    mean = jnp.where(lane < POOL_GROUP_DIM, s2 / count(2),
                     jnp.where(lane < 2 * POOL_GROUP_DIM, s4 / count(4),
                               jnp.where(lane < 3 * POOL_GROUP_DIM, s8 / count(8), s16 / count(16))))
    pooled = _dot((mean - u).astype(BF16), pw_ref[...]) * ps_ref[...]
    cat_ref[:, D_ATTN:] = _rms(pooled, gb_ref[...]).astype(BF16)

    hp = ext_ref[SUBLANES - 1:SUBLANES - 1 + BLOCK, D_POOL:]
    hc = ext_ref[SUBLANES:SUBLANES + BLOCK, D_POOL:]
    hn = ext_ref[SUBLANES + 1:SUBLANES + 1 + BLOCK, D_POOL:]
    _store_tiles(hy_ref, (), 0, hp * hw_ref[0:1, :] + hc * hw_ref[1:2, :] + hn * hw_ref[2:3, :] + hb_ref[...])


def _attn_bias():
    row = jnp.arange(BLOCK, dtype=jnp.int32)[:, None]
    col = jnp.arange(3 * BLOCK, dtype=jnp.int32)[None, :]
    dist = jnp.abs(row + BLOCK - col)
    slopes = 2.0 ** (-8.0 * (jnp.arange(N_Q_HEADS, dtype=F32) + 1.0) / N_Q_HEADS)
    bias = -slopes[:, None, None] * dist.astype(F32)[None]
    out = []
    for case in range(4):
        valid = dist <= WINDOW
        if case & 1:
            valid = valid & (col >= BLOCK)
        if case & 2:
            valid = valid & (col < 2 * BLOCK)
        out.append(jnp.where(valid[None], bias, NEG_INF))
    return jnp.stack(out)


def _mix(q, kv, ph, sink, pw_bd, pool_scale, hy_w, hy_b, g_a, g_b, B, L):
    T = B * L
    nb = L // BLOCK
    rpb = BLOCK // SUBLANES
    n8 = T // SUBLANES
    cur = lambda b, l: (b * nb + l, 0)
    prev = lambda b, l: (b * nb + jnp.maximum(l - 1, 0), 0)
    nxt = lambda b, l: (b * nb + jnp.minimum(l + 1, nb - 1), 0)
    prev8 = lambda b, l: (jnp.maximum((b * nb + l) * rpb - 1, 0), 0)
    next8 = lambda b, l: (jnp.minimum((b * nb + l + 1) * rpb, n8 - 1), 0)
    fix = lambda b, l: (0, 0)
    return pl.pallas_call(
        functools.partial(_mix_kernel, seq_len=L),
        out_shape=[jax.ShapeDtypeStruct((T, D_AB), BF16),
                   jax.ShapeDtypeStruct((B, FFT_JH, D_HY_IN // LANES, nb * FFT_J, LANES), F32)],
        grid=(B, nb),
        in_specs=[pl.BlockSpec(memory_space=pltpu.SMEM),
                  pl.BlockSpec((None, N_Q_HEADS, BLOCK, 3 * BLOCK),
                               lambda b, l: ((l == 0).astype(jnp.int32) + 2 * (l == nb - 1).astype(jnp.int32), 0, 0, 0)),
                  pl.BlockSpec((BLOCK, D_ATTN), cur),
                  pl.BlockSpec((BLOCK, 2 * D_KV), prev),
                  pl.BlockSpec((BLOCK, 2 * D_KV), cur),
                  pl.BlockSpec((BLOCK, 2 * D_KV), nxt),
                  pl.BlockSpec((BLOCK, D_PH), cur),
                  pl.BlockSpec((SUBLANES, D_PH), prev8),
                  pl.BlockSpec((SUBLANES, D_PH), next8),
                  pl.BlockSpec((D_POOL, D_POOL), fix),
                  pl.BlockSpec((1, D_POOL), fix),
                  pl.BlockSpec((3, D_HY_IN), fix),
                  pl.BlockSpec((1, D_HY_IN), fix),
                  pl.BlockSpec((1, D_ATTN), fix),
                  pl.BlockSpec((1, D_POOL), fix)],
        out_specs=[pl.BlockSpec((BLOCK, D_AB), cur),
                   pl.BlockSpec((None, FFT_JH, D_HY_IN // LANES, FFT_J, LANES), lambda b, l: (b, 0, 0, l, 0))],
        scratch_shapes=[pltpu.VMEM((BLOCK + 2 * SUBLANES, D_PH), F32),
                        pltpu.VMEM((BLOCK, D_ATTN), F32),
                        pltpu.VMEM((N_Q_HEADS, BLOCK, 3 * BLOCK), F32),
                        pltpu.VMEM((N_Q_HEADS, BLOCK, 3 * BLOCK), BF16),
                        pltpu.VMEM((N_Q_HEADS, BLOCK, 1), F32)],
        compiler_params=_params(2),
        name="mix",
    )(sink, _attn_bias(), q, kv, kv, kv, ph, ph, ph, pw_bd, pool_scale, hy_w, hy_b, g_a, g_b)


def _split3(x):
    hi = x.astype(BF16)
    lo = (x - hi.astype(F32)).astype(BF16)
    return jnp.concatenate([hi, lo, hi], axis=1)


def _stack3(w):
    hi = w.astype(BF16)
    lo = (w - hi.astype(F32)).astype(BF16)
    return jnp.concatenate([hi, hi, lo], axis=0)


def _filter_kernel(z_ref, t_ref, w1_ref, b1_ref, f1_ref, w2_ref, b2_ref, f2_ref, w3_ref, dl_ref, o_ref, *, seq_len):
    h = jnp.sin(f1_ref[...] * (_dot(z_ref[...], w1_ref[...]) + b1_ref[...]))
    h = jnp.sin(f2_ref[...] * (_dot(_split3(h), w2_ref[...]) + b2_ref[...]))
    h = _dot(_split3(h), w3_ref[...])
    n = pl.program_id(0) * h.shape[0] + lax.broadcasted_iota(jnp.int32, (h.shape[0], 1), 0)
    decay = jnp.where(n == seq_len, 0.0, jnp.exp(-t_ref[...] * dl_ref[...]))
    for o in range(HYENA_ORDER):
        _store_tiles(o_ref, (o,), 0, h[:, o * D_HYENA:(o + 1) * D_HYENA] * decay)


def _filter_kernels(L, w1, b1, fr1, w2, b2, fr2, w3):
    pos = jnp.arange(2 * L, dtype=jnp.int32)
    tpos = jnp.where(pos < L, pos, (2 * L - pos) % L).astype(F32)[:, None]
    t_norm = tpos * (1.0 / (L - 1))
    bands = jnp.linspace(1e-4, FILTER_BANDS - 1, FILTER_BANDS, dtype=F32)[None, :]
    ang = 2.0 * math.pi * tpos * bands / L
    z = jnp.concatenate([t_norm, jnp.cos(ang), -jnp.sin(ang)], axis=-1)
    z3 = jnp.pad(_split3(z), ((0, 0), (0, LANES - 3 * FILTER_EMB)))
    w1s = jnp.pad(_stack3(w1), ((0, LANES - 3 * FILTER_EMB), (0, 0)))
    w3d = w3.reshape(FILTER_HIDDEN, HYENA_ORDER, 2, D_HYENA).transpose(2, 0, 1, 3).reshape(
        2, FILTER_HIDDEN, HYENA_ORDER * D_HYENA)
    w3s = jnp.stack([_stack3(w3d[0]), _stack3(w3d[1])])
    max_decay = math.log(DECAY_TARGET) / DECAY_FAST_PCT
    min_decay = math.log(DECAY_TARGET) / DECAY_SLOW_PCT
    dl = jnp.abs(jnp.linspace(min_decay, max_decay, D_HYENA, dtype=F32))[None, :]
    tl = 512
    half = L // tl
    row = lambda i: (i, 0)
    fix = lambda i: (0, 0)
    return pl.pallas_call(
        functools.partial(_filter_kernel, seq_len=L),
        out_shape=jax.ShapeDtypeStruct((HYENA_ORDER, FFT_JH, HY_CT, 2 * L // FFT_N2 * FFT_J, LANES), F32),
        grid=(2 * L // tl,),
        in_specs=[pl.BlockSpec((tl, LANES), row),
                  pl.BlockSpec((tl, 1), row),
                  pl.BlockSpec((LANES, FILTER_HIDDEN), fix),
                  pl.BlockSpec((1, FILTER_HIDDEN), fix),
                  pl.BlockSpec((1, FILTER_HIDDEN), fix),
                  pl.BlockSpec((3 * FILTER_HIDDEN, FILTER_HIDDEN), fix),
                  pl.BlockSpec((1, FILTER_HIDDEN), fix),
                  pl.BlockSpec((1, FILTER_HIDDEN), fix),
                  pl.BlockSpec((None, 3 * FILTER_HIDDEN, HYENA_ORDER * D_HYENA), lambda i: (i // half, 0, 0)),
                  pl.BlockSpec((1, D_HYENA), fix)],
        out_specs=pl.BlockSpec((HYENA_ORDER, FFT_JH, HY_CT, tl // FFT_N2 * FFT_J, LANES), lambda i: (0, 0, 0, i, 0)),
        compiler_params=_params(1),
        name="hyena_filters",
    )(z3, t_norm, w1s, b1[None, :], fr1[None, :], _stack3(w2), b2[None, :], fr2[None, :], w3s, dl)


def _dft_mats(L):
    N = 2 * L
    n1n = N // FFT_N2
    r = n1n // 2
    assert r % FFT_TF == 0, "sequence length must be a multiple of FFT_TF * FFT_N2 / 2"
    mh = r + FFT_TF
    f1 = jnp.arange(mh, dtype=jnp.int32)[:, None]
    n1 = jnp.arange(n1n, dtype=jnp.int32)[None, :]
    keep = (f1 <= r).astype(F32)
    th = (2.0 * math.pi / n1n) * ((f1 * n1) % n1n).astype(F32)
    c1, s1 = keep * jnp.cos(th), keep * jnp.sin(th)
    fwd_full = jnp.concatenate([c1, -s1], axis=0).astype(BF16)
    fwd_half = fwd_full[:, :r]
    w = jnp.where((f1 == 0) | (f1 == r), 1.0, 2.0)
    inv_half = jnp.concatenate([(w * c1)[:, :r].T, (-w * s1)[:, :r].T], axis=1).astype(BF16)
    i2 = jnp.arange(2 * FFT_N2, dtype=jnp.int32)
    part, idx = i2 // FFT_N2, i2 % FFT_N2
    pa = (2.0 * math.pi / N) * ((jnp.arange(mh, dtype=jnp.int32)[:, None] * idx[None, :]) % N).astype(F32)
    ca, sa = jnp.cos(pa), jnp.sin(pa)
    mb = (idx[:, None] * idx[None, :] * n1n + (part[:, None] - part[None, :]) * (N // 4)) % N
    pb = (2.0 * math.pi / N) * mb.astype(F32)
    cb, sb = jnp.cos(pb), jnp.sin(pb)
    g = (ca[:, None, :] * cb[None] - sa[:, None, :] * sb[None]).astype(BF16)
    ginv = (ca[:, :, None] * cb.T[None] - sa[:, :, None] * sb.T[None]).astype(BF16)
    return dict(n1=n1n, r=r, mh=mh, fwd_full=fwd_full, fwd_half=fwd_half, inv_half=inv_half, g=g, ginv=ginv)


def _fft_a_kernel(x_ref, f_ref, ar_ref, ai_ref, *, tb, mh):
    f = f_ref[...]
    r = x_ref.shape[1] // FFT_J
    for b in range(tb):
        for j in range(FFT_J):
            a = _dot(f, x_ref[b, pl.ds(j, r, stride=FFT_J), :].astype(BF16))
            ar_ref[b, pl.ds(j, mh, stride=FFT_J), :] = a[:mh]
            ai_ref[b, pl.ds(j, mh, stride=FFT_J), :] = a[mh:]


def _fft_a(x5, coff, fmat, mh, tb):
    B, rj = x5.shape[0], x5.shape[3]
    blk = lambda b, j, c: (b, j, c, 0, 0)
    out = jax.ShapeDtypeStruct((B, FFT_JH, HY_CT, mh * FFT_J, LANES), F32)
    return pl.pallas_call(
        functools.partial(_fft_a_kernel, tb=tb, mh=mh),
        out_shape=[out, out],
        grid=(B // tb, FFT_JH, HY_CT),
        in_specs=[pl.BlockSpec((tb, None, None, rj, LANES), lambda b, j, c: (b, j, c + coff, 0, 0)),
                  pl.BlockSpec((2 * mh, rj // FFT_J), lambda b, j, c: (0, 0))],
        out_specs=[pl.BlockSpec((tb, None, None, mh * FFT_J, LANES), blk)] * 2,
        compiler_params=_params(3),
        name="fft_outer_fwd",
    )(x5, fmat)


def _fft_spec_kernel(ar_ref, ai_ref, g_ref, kr_ref, ki_ref, *, scale):
    for f in range(FFT_TF):
        a = jnp.concatenate([_load_tiles(ar_ref, (), f * FFT_J, 1, HY_CT),
                             _load_tiles(ai_ref, (), f * FFT_J, 1, HY_CT)], axis=0).astype(BF16)
        x = _dot(g_ref[f], a) * scale
        kr_ref[0, f] = x[:FFT_N2]
        ki_ref[0, f] = x[FFT_N2:]


def _tile_spec():
    return pl.BlockSpec((None, FFT_JH, HY_CT, FFT_TF * FFT_J, LANES), lambda f, b: (b, 0, 0, f, 0))


def _fft_spec(ar, ai, g, scale):
    B, mh = ar.shape[0], ar.shape[3] // FFT_J
    nat = pl.BlockSpec((1, FFT_TF, FFT_N2, D_HYENA), lambda f, b: (b, f, 0, 0))
    out = jax.ShapeDtypeStruct((B, mh, FFT_N2, D_HYENA), F32)
    return pl.pallas_call(
        functools.partial(_fft_spec_kernel, scale=scale),
        out_shape=[out, out],
        grid=(mh // FFT_TF, B),
        in_specs=[_tile_spec(), _tile_spec(),
                  pl.BlockSpec((FFT_TF, 2 * FFT_N2, 2 * FFT_N2), lambda f, b: (f, 0, 0))],
        out_specs=[nat, nat],
        compiler_params=_params(2),
        name="fft_inner_fwd",
    )(ar, ai, g)


def _fft_b_kernel(ar_ref, ai_ref, g_ref, gi_ref, kr_ref, ki_ref, yr_ref, yi_ref):
    for f in range(FFT_TF):
        a = jnp.concatenate([_load_tiles(ar_ref, (), f * FFT_J, 1, HY_CT),
                             _load_tiles(ai_ref, (), f * FFT_J, 1, HY_CT)], axis=0).astype(BF16)
        x = _dot(g_ref[f], a)
        xr, xi = x[:FFT_N2], x[FFT_N2:]
        kr, ki = kr_ref[0, f], ki_ref[0, f]
        z = jnp.concatenate([xr * kr - xi * ki, xr * ki + xi * kr], axis=0).astype(BF16)
        y = _dot(gi_ref[f], z)
        _store_tiles(yr_ref, (), f * FFT_J, y[:FFT_N2])
        _store_tiles(yi_ref, (), f * FFT_J, y[FFT_N2:])


def _fft_b(ar, ai, g, ginv, kr, ki, o):
    B, mh = ar.shape[0], ar.shape[3] // FFT_J
    mat = pl.BlockSpec((FFT_TF, 2 * FFT_N2, 2 * FFT_N2), lambda f, b: (f, 0, 0))
    spec = pl.BlockSpec((1, FFT_TF, FFT_N2, D_HYENA), lambda f, b: (o, f, 0, 0))
    out = jax.ShapeDtypeStruct(ar.shape, F32)
    return pl.pallas_call(
        _fft_b_kernel,
        out_shape=[out, out],
        grid=(mh // FFT_TF, B),
        in_specs=[_tile_spec(), _tile_spec(), mat, mat, spec, spec],
        out_specs=[_tile_spec(), _tile_spec()],
        compiler_params=_params(2),
        name="fft_inner",
    )(ar, ai, g, ginv, kr, ki)


def _fft_c_kernel(yr_ref, yi_ref, c_ref, u_ref, gate_ref, bias_ref, o_ref, *, tb):
    cm = c_ref[...]
    bias = bias_ref[...]
    mh, r = yr_ref.shape[1] // FFT_J, u_ref.shape[1] // FFT_J
    for b in range(tb):
        for j in range(FFT_J):
            yy = jnp.concatenate([yr_ref[b, pl.ds(j, mh, stride=FFT_J), :],
                                  yi_ref[b, pl.ds(j, mh, stride=FFT_J), :]], axis=0).astype(BF16)
            y = _dot(cm, yy)
            rows = pl.ds(j, r, stride=FFT_J)
            o_ref[b, rows, :] = gate_ref[b, rows, :] * (y + bias * u_ref[b, rows, :])


def _fft_c(yr, yi, cmat, u5, uoff, gate5, goff, bias, tb):
    B, mhj = yr.shape[0], yr.shape[3]
    rj = cmat.shape[0] * FFT_J
    blk = lambda b, j, c: (b, j, c, 0, 0)
    return pl.pallas_call(
        functools.partial(_fft_c_kernel, tb=tb),
        out_shape=jax.ShapeDtypeStruct((B, FFT_JH, HY_CT, rj, LANES), F32),
        grid=(B // tb, FFT_JH, HY_CT),
        in_specs=[pl.BlockSpec((tb, None, None, mhj, LANES), blk),
                  pl.BlockSpec((tb, None, None, mhj, LANES), blk),
                  pl.BlockSpec(cmat.shape, lambda b, j, c: (0, 0)),
                  pl.BlockSpec((tb, None, None, rj, LANES), lambda b, j, c: (b, j, c + uoff, 0, 0)),
                  pl.BlockSpec((tb, None, None, rj, LANES), lambda b, j, c: (b, j, c + goff, 0, 0)),
                  pl.BlockSpec((1, LANES), lambda b, j, c: (0, c))],
        out_specs=pl.BlockSpec((tb, None, None, rj, LANES), blk),
        compiler_params=_params(3),
        name="fft_outer_inv",
    )(yr, yi, cmat, u5, gate5, bias)


def _filter_spectra(kern, mats, L):
    ar, ai = _fft_a(kern, 0, mats["fwd_full"], mats["mh"], 1)
    return _fft_spec(ar, ai, mats["g"], 1.0 / (2 * L))


def _hyena(hy5, kr, ki, hy_bias, mats, tb):
    z5 = hy5
    for o in range(HYENA_ORDER):
        ar, ai = _fft_a(z5, 0, mats["fwd_half"], mats["mh"], tb)
        yr, yi = _fft_b(ar, ai, mats["g"], mats["ginv"], kr, ki, o)
        z5 = _fft_c(yr, yi, mats["inv_half"], z5, 0, hy5, (o + 1) * HY_CT, hy_bias[o][None, :], tb)
    return z5


def _out_ffn_in_kernel(x_ref, cab_ref, c_ref, gc_ref, wab_ref, wc_ref, g2_ref, wi_ref, xn_ref, u_ref):
    c = _load_tiles(c_ref, (), 0, c_ref.shape[2] // FFT_J, HY_CT)
    cn = _rms(c, gc_ref[...]).astype(BF16)
    xn = x_ref[...] + _dot(cab_ref[...], wab_ref[...]) + _dot(cn, wc_ref[...])
    xn_ref[...] = xn
    h = _rms(xn, g2_ref[...]).astype(BF16)
    for j in range(0, 2 * D_FF, FFN_IN_CHUNK):
        u_ref[:, j:j + FFN_IN_CHUNK] = _dot(h, wi_ref[:, j:j + FFN_IN_CHUNK])


def _out_ffn_in(x2, cab, c5, g_c, w_ab, w_c, g2, w_in):
    T = x2.shape[0]
    tm = ROW_TILE
    tps = c5.shape[3] // FFT_J * FFT_N2 // tm
    row = lambda i: (i, 0)
    fix = lambda i: (0, 0)
    return pl.pallas_call(
        _out_ffn_in_kernel,
        out_shape=[jax.ShapeDtypeStruct((T, D_MODEL), F32),
                   jax.ShapeDtypeStruct((T, 2 * D_FF), F32)],
        grid=(T // tm,),
        in_specs=[pl.BlockSpec((tm, D_MODEL), row),
                  pl.BlockSpec((tm, D_AB), row),
                  pl.BlockSpec((None, FFT_JH, HY_CT, tm // FFT_N2 * FFT_J, LANES),
                               lambda i: (i // tps, 0, 0, i % tps, 0)),
                  pl.BlockSpec((1, D_HYENA), fix),
                  pl.BlockSpec((D_AB, D_MODEL), fix),
                  pl.BlockSpec((D_HYENA, D_MODEL), fix),
                  pl.BlockSpec((1, D_MODEL), fix),
                  pl.BlockSpec((D_MODEL, 2 * D_FF), fix)],
        out_specs=[pl.BlockSpec((tm, D_MODEL), row),
                   pl.BlockSpec((tm, 2 * D_FF), row)],
        compiler_params=_params(1),
        name="out_ffn_in",
    )(x2, cab, c5, g_c, w_ab, w_c, g2, w_in)


def _ffn_out_kernel(u_ref, up_ref, un_ref, xn_ref, cw_ref, cb_ref, wo_ref, o_ref, acc_ref, *, tiles_per_seq):
    i = pl.program_id(0)
    is_first = (i % tiles_per_seq) == 0
    is_last = (i % tiles_per_seq) == tiles_per_seq - 1
    tm = u_ref.shape[0]
    row = lax.broadcasted_iota(jnp.int32, (tm, 1), 0)

    def conv(c0):
        cs = slice(c0, c0 + FFN_CHUNK)
        uc = u_ref[:, cs]
        before = jnp.where(is_first, 0.0, up_ref[SUBLANES - 1:SUBLANES, cs])
        after = jnp.where(is_last, 0.0, un_ref[0:1, cs])
        um = jnp.where(row == 0, before, pltpu.roll(uc, 1, axis=0))
        up = jnp.where(row == tm - 1, after, pltpu.roll(uc, tm - 1, axis=0))
        return um * cw_ref[0:1, cs] + uc * cw_ref[1:2, cs] + up * cw_ref[2:3, cs] + cb_ref[:, cs]

    for j in range(D_FF // FFN_CHUNK):
        gate = conv(j * FFN_CHUNK)
        half_val = conv(D_FF + j * FFN_CHUNK)
        act = gate * (1.0 + lax.erf(gate * math.sqrt(0.5))) * half_val
        part = _dot(act.astype(BF16), wo_ref[j * FFN_CHUNK:(j + 1) * FFN_CHUNK, :])
        if j == 0:
            acc_ref[...] = part
        else:
            acc_ref[...] += part
    o_ref[...] = xn_ref[...] + acc_ref[...]


def _ffn_out(u, xn, cw, cb, w_out, L):
    T = xn.shape[0]
    tm = ROW_TILE
    hpt = tm // SUBLANES
    n8 = T // SUBLANES
    row = lambda i: (i, 0)
    fix = lambda i: (0, 0)
    return pl.pallas_call(
        functools.partial(_ffn_out_kernel, tiles_per_seq=L // tm),
        out_shape=jax.ShapeDtypeStruct((T, D_MODEL), F32),
        grid=(T // tm,),
        in_specs=[pl.BlockSpec((tm, 2 * D_FF), row),
                  pl.BlockSpec((SUBLANES, 2 * D_FF), lambda i: (jnp.maximum(i * hpt - 1, 0), 0)),
                  pl.BlockSpec((SUBLANES, 2 * D_FF), lambda i: (jnp.minimum((i + 1) * hpt, n8 - 1), 0)),
                  pl.BlockSpec((tm, D_MODEL), row),
                  pl.BlockSpec((3, 2 * D_FF), fix),
                  pl.BlockSpec((1, 2 * D_FF), fix),
                  pl.BlockSpec((D_FF, D_MODEL), fix)],
        out_specs=pl.BlockSpec((tm, D_MODEL), row),
        scratch_shapes=[pltpu.VMEM((tm, D_MODEL), F32)],
        compiler_params=_params(1),
        name="ffn_out",
    )(u, u, u, xn, cw, cb, w_out)


def _head_ones(n_heads):
    return jnp.kron(jnp.eye(n_heads, dtype=F32), jnp.ones((HEAD_DIM, HEAD_DIM), F32)).astype(BF16)


def _layer(x2, B, L, mats, spectra, lw):
    q, kv, ph = _in_proj(x2, lw["norm1_g"], lw["w_in"], lw["eq"], lw["ek"], lw["qg"], lw["kg"])
    cab, hy = _mix(q, kv, ph, lw["sink"], lw["pw_bd"], lw["pool_scale"], lw["hy_w"], lw["hy_b"],
                   lw["g_a"], lw["g_b"], B, L)
    tb = math.gcd(B, 8)
    c5 = _hyena(hy, spectra[0], spectra[1], lw["hy_bias"], mats, tb)
    xn, u = _out_ffn_in(x2, cab, c5, lw["g_c"], lw["w_ab"], lw["w_c"],
                        lw["norm2_g"], lw["w_ffn_in"])
    return _ffn_out(u, xn, lw["ffn_cw"], lw["ffn_cb"], lw["w_ffn_out"], L)


def kernel(x_prompt, x_sample, norm1_g, w_in, q_norm_g, k_norm_g, attn_sink, pool_w, pool_scale, hy_conv_w,
           hy_conv_b, filt_w1, filt_b1, filt_freq1, filt_w2, filt_b2, filt_freq2, filt_w3, hy_bias, out_norm_g,
           w_out, norm2_g, w_ffn_in, ffn_conv_w, ffn_conv_b, w_ffn_out):
    groups = [x_prompt, x_sample]
    lens = sorted({g.shape[1] for g in groups})
    mats = {L: _dft_mats(L) for L in lens}
    eq, ek = _head_ones(N_Q_HEADS), _head_ones(N_KV_HEADS)
    xs = [g.reshape(-1, D_MODEL) for g in groups]
    glu_scale = jnp.concatenate([jnp.ones((D_FF,), F32), jnp.full((D_FF,), 0.5, F32)])[None, :]
    for l in range(DEPTH):
        lw = dict(
            norm1_g=norm1_g[l][None, :], w_in=w_in[l].astype(BF16), eq=eq, ek=ek,
            qg=jnp.tile(q_norm_g[l], N_Q_HEADS)[None, :], kg=jnp.tile(k_norm_g[l], N_KV_HEADS)[None, :],
            sink=attn_sink[l],
            pw_bd=jax.scipy.linalg.block_diag(*[pool_w[l, g] for g in range(len(POOL_WINDOWS))]).astype(BF16),
            pool_scale=pool_scale[l][None, :], hy_w=hy_conv_w[l], hy_b=hy_conv_b[l][None, :],
            g_a=out_norm_g[l, :D_ATTN][None, :], g_b=out_norm_g[l, D_ATTN:D_AB][None, :],
            g_c=out_norm_g[l, D_AB:][None, :], hy_bias=hy_bias[l],
            w_ab=w_out[l, :D_AB].astype(BF16), w_c=w_out[l, D_AB:].astype(BF16),
            norm2_g=norm2_g[l][None, :], w_ffn_in=w_ffn_in[l].astype(BF16),
            ffn_cw=ffn_conv_w[l] * glu_scale, ffn_cb=(ffn_conv_b[l] * glu_scale[0])[None, :],
            w_ffn_out=w_ffn_out[l].astype(BF16),
        )
        spectra = {}
        for L in lens:
            kern = _filter_kernels(L, filt_w1[l], filt_b1[l], filt_freq1[l], filt_w2[l], filt_b2[l], filt_freq2[l],
                                   filt_w3[l])
            spectra[L] = _filter_spectra(kern, mats[L], L)
        xs = [_layer(x2, g.shape[0], g.shape[1], mats[g.shape[1]], spectra[g.shape[1]], lw)
              for x2, g in zip(xs, groups)]
    return tuple(x2.reshape(g.shape) for x2, g in zip(xs, groups))
```

```python
import functools
import math

import jax
import jax.numpy as jnp
from jax import lax
from jax.experimental import pallas as pl
from jax.experimental.pallas import tpu as pltpu

F32 = jnp.float32
BF16 = jnp.bfloat16

D_MODEL = 1024
DEPTH = 2
HEAD_DIM = 64
N_Q_HEADS = 8
N_KV_HEADS = 2
GQA_GROUP = N_Q_HEADS // N_KV_HEADS
D_ATTN = N_Q_HEADS * HEAD_DIM
D_KV = N_KV_HEADS * HEAD_DIM
WINDOW = 128
BLOCK = 128
POOL_WINDOWS = (2, 4, 8, 16)
D_POOL = 256
POOL_GROUP_DIM = D_POOL // len(POOL_WINDOWS)
D_HYENA = 256
HYENA_ORDER = 2
FILTER_BANDS = 16
FILTER_EMB = 1 + 2 * FILTER_BANDS
FILTER_HIDDEN = 64
N_FILTERS = 2 * HYENA_ORDER
DECAY_FAST_PCT = 0.3
DECAY_SLOW_PCT = 1.5
DECAY_TARGET = 1e-2
D_HY_IN = (HYENA_ORDER + 1) * D_HYENA
D_PH = D_POOL + D_HY_IN
D_IN_PROJ = D_ATTN + 2 * D_KV + D_PH
D_CAT = D_ATTN + D_POOL + D_HYENA
D_AB = D_ATTN + D_POOL
D_FF = 2816
EPS = 1e-6
NEG_INF = -1e30

LANES = 128
SUBLANES = 8
FFT_N2 = 128
FFT_J = SUBLANES
FFT_TF = 8
FFT_JH = FFT_N2 // FFT_J
HY_CT = D_HYENA // LANES
FFN_CHUNK = 256
FFN_IN_CHUNK = 512
ROW_TILE = 256
MIX_NB = 2
VMEM_LIMIT = 48 * 1024 * 1024


def _params(n_axes):
    return pltpu.CompilerParams(dimension_semantics=("arbitrary",) * n_axes, vmem_limit_bytes=VMEM_LIMIT)


def _rms(x, g):
    return x * lax.rsqrt(jnp.mean(x * x, axis=-1, keepdims=True) + EPS) * g


def _dot(a, b):
    return jnp.dot(a, b, preferred_element_type=F32)


def _store_tiles(ref, lead, row0, val):
    for nl in range(val.shape[0] // FFT_N2):
        for jh in range(FFT_JH):
            for ct in range(val.shape[1] // LANES):
                r0 = nl * FFT_N2 + jh * FFT_J
                ref[lead + (jh, ct, pl.ds(row0 + nl * FFT_J, FFT_J), slice(None))] = (
                    val[r0:r0 + FFT_J, ct * LANES:(ct + 1) * LANES])


def _load_tiles(ref, lead, row0, n_local, n_ct):
    return jnp.concatenate(
        [jnp.concatenate([ref[lead + (jh, ct, pl.ds(row0 + nl * FFT_J, FFT_J), slice(None))]
                          for nl in range(n_local) for jh in range(FFT_JH)], axis=0)
         for ct in range(n_ct)], axis=1)


def _pack_pair(re, im):
    hi = lax.bitcast_convert_type(re.astype(BF16).astype(F32), jnp.uint32)
    lo = lax.bitcast_convert_type(im.astype(BF16).astype(F32), jnp.uint32)
    return hi | (lo >> 16)


def _unpack_pair(w):
    re = lax.bitcast_convert_type(w & jnp.uint32(0xFFFF0000), F32)
    im = lax.bitcast_convert_type(w << 16, F32)
    return jnp.concatenate([re, im], axis=0).astype(BF16)


def _in_proj_kernel(x_ref, g_ref, wq_ref, w_ref, ek_ref, qg_ref, kg_ref, q_ref, kv_ref, ph_ref):
    h = _rms(x_ref[...], g_ref[...]).astype(BF16)
    tm = h.shape[0]

    qt = lax.dot_general(wq_ref[...], h, (((1,), (1,)), ((), ())), preferred_element_type=F32)
    qt = qt.reshape(N_Q_HEADS, HEAD_DIM, tm)
    qt = qt * lax.rsqrt(jnp.mean(qt * qt, axis=1, keepdims=True) + EPS)
    qt = (qt.reshape(D_ATTN, tm) * qg_ref[...]).astype(BF16)
    for j in range(tm // BLOCK):
        q_ref[j] = qt[:, j * BLOCK:(j + 1) * BLOCK]

    p = _dot(h, w_ref[...])
    k = p[:, :D_KV]
    ss = _dot((k * k).astype(BF16), ek_ref[...])
    kv_ref[:, :D_KV] = (k * lax.rsqrt(ss * (1.0 / HEAD_DIM) + EPS) * kg_ref[...]).astype(BF16)
    kv_ref[:, D_KV:] = p[:, D_KV:2 * D_KV].astype(BF16)
    ph_ref[...] = p[:, 2 * D_KV:]


def _in_proj(x2, g, wq_t, w, ek, qg, kg):
    T = x2.shape[0]
    tm = ROW_TILE
    row = lambda i: (i, 0)
    fix = lambda i: (0, 0)
    return pl.pallas_call(
        _in_proj_kernel,
        out_shape=[jax.ShapeDtypeStruct((T // BLOCK, D_ATTN, BLOCK), BF16),
                   jax.ShapeDtypeStruct((T, 2 * D_KV), BF16),
                   jax.ShapeDtypeStruct((T, D_PH), F32)],
        grid=(T // tm,),
        in_specs=[pl.BlockSpec((tm, D_MODEL), row),
                  pl.BlockSpec((1, D_MODEL), fix),
                  pl.BlockSpec((D_ATTN, D_MODEL), fix),
                  pl.BlockSpec((D_MODEL, 2 * D_KV + D_PH), fix),
                  pl.BlockSpec((D_KV, D_KV), fix),
                  pl.BlockSpec((D_ATTN, 1), fix),
                  pl.BlockSpec((1, D_KV), fix)],
        out_specs=[pl.BlockSpec((tm // BLOCK, D_ATTN, BLOCK), lambda i: (i, 0, 0)),
                   pl.BlockSpec((tm, 2 * D_KV), row),
                   pl.BlockSpec((tm, D_PH), row)],
        compiler_params=_params(1),
        name="in_proj",
    )(x2, g, wq_t, w, ek, qg, kg)


def _mix_kernel(sink_ref, *refs, seq_len):
    bias_refs = refs[:MIX_NB]
    (q_ref, kvp_ref, kvc_ref, kvn_ref, phc_ref, php_ref, phn_ref, pw_ref, ps_ref, hw_ref, hb_ref, ga_ref, gb_ref,
     cat_ref, hy_ref, ext_ref, att_ref, s_ref, p_ref, rden_ref) = refs[MIX_NB:]
    l = pl.program_id(1)
    is_first = l == 0
    is_last = l == pl.num_programs(1) - 1
    rows = MIX_NB * BLOCK

    tn = (((0,), (0,)), ((), ()))

    def keys(sb, cols):
        def blk(i):
            if i < 0:
                return kvp_ref[:, cols]
            if i >= MIX_NB:
                return kvn_ref[:, cols]
            return kvc_ref[i * BLOCK:(i + 1) * BLOCK, cols]
        return jnp.concatenate([blk(sb - 1), blk(sb), blk(sb + 1)], axis=0)

    pairs = [(sb, h) for sb in range(MIX_NB) for h in range(N_Q_HEADS)]
    for i, (sb, h) in enumerate(pairs):
        g = h // GQA_GROUP
        st = _dot(keys(sb, slice(g * HEAD_DIM, (g + 1) * HEAD_DIM)), q_ref[sb, h * HEAD_DIM:(h + 1) * HEAD_DIM, :])
        s_ref[i] = st + bias_refs[sb][h]
    for i, (sb, h) in enumerate(pairs):
        st = s_ref[i]
        sk = sink_ref[h]
        m = jnp.maximum(jnp.max(st, axis=0, keepdims=True), sk)
        e = jnp.exp(st - m)
        rden_ref[i] = 1.0 / (jnp.sum(e, axis=0, keepdims=True) + jnp.exp(sk - m))
        p_ref[i] = e.astype(BF16)
    for i, (sb, h) in enumerate(pairs):
        g = h // GQA_GROUP
        vh = keys(sb, slice(D_KV + g * HEAD_DIM, D_KV + (g + 1) * HEAD_DIM))
        ot = lax.dot_general(vh, p_ref[i], tn, preferred_element_type=F32)
        att_ref[sb, h * HEAD_DIM:(h + 1) * HEAD_DIM, :] = ot * rden_ref[i]
    for sb in range(MIX_NB):
        cat_ref[sb * BLOCK:(sb + 1) * BLOCK, :D_ATTN] = _rms(att_ref[sb].T, ga_ref[...]).astype(BF16)

    ext_ref[0:SUBLANES, :] = jnp.where(is_first, 0.0, php_ref[...])
    ext_ref[SUBLANES:SUBLANES + rows, :] = phc_ref[...]
    ext_ref[SUBLANES + rows:, :] = jnp.where(is_last, 0.0, phn_ref[...])

    t = l * rows + lax.broadcasted_iota(jnp.int32, (rows, 1), 0)
    lane = lax.broadcasted_iota(jnp.int32, (1, LANES), 1)

    def window_mean(w, col0):
        lo = jnp.clip(t - w // 2, 0, seq_len)
        hi = jnp.clip(t - w // 2 + w, 0, seq_len)
        tot = ext_ref[SUBLANES - w // 2:SUBLANES - w // 2 + rows, col0:col0 + LANES]
        for d in range(1 - w // 2, w // 2):
            tot = tot + ext_ref[SUBLANES + d:SUBLANES + d + rows, col0:col0 + LANES]
        return tot / (hi - lo).astype(F32)

    gpt = LANES // POOL_GROUP_DIM
    means = [jnp.where(lane < POOL_GROUP_DIM, window_mean(POOL_WINDOWS[gpt * i], i * LANES),
                       window_mean(POOL_WINDOWS[gpt * i + 1], i * LANES)) for i in range(D_POOL // LANES)]
    u = ext_ref[SUBLANES:SUBLANES + rows, 0:D_POOL]
    pooled = _dot((jnp.concatenate(means, axis=1) - u).astype(BF16), pw_ref[...]) * ps_ref[...]
    cat_ref[:, D_ATTN:] = _rms(pooled, gb_ref[...]).astype(BF16)

    hp = ext_ref[SUBLANES - 1:SUBLANES - 1 + rows, D_POOL:]
    hc = ext_ref[SUBLANES:SUBLANES + rows, D_POOL:]
    hn = ext_ref[SUBLANES + 1:SUBLANES + 1 + rows, D_POOL:]
    _store_tiles(hy_ref, (), 0, hp * hw_ref[0:1, :] + hc * hw_ref[1:2, :] + hn * hw_ref[2:3, :] + hb_ref[...])


def _attn_bias():
    row = jnp.arange(BLOCK, dtype=jnp.int32)[None, :]
    col = jnp.arange(3 * BLOCK, dtype=jnp.int32)[:, None]
    dist = jnp.abs(row + BLOCK - col)
    slopes = 2.0 ** (-8.0 * (jnp.arange(N_Q_HEADS, dtype=F32) + 1.0) / N_Q_HEADS)
    bias = -slopes[:, None, None] * dist.astype(F32)[None]
    out = []
    for case in range(4):
        valid = dist <= WINDOW
        if case & 1:
            valid = valid & (col >= BLOCK)
        if case & 2:
            valid = valid & (col < 2 * BLOCK)
        out.append(jnp.where(valid[None], bias, NEG_INF))
    return jnp.stack(out)


def _mix(q, kv, ph, sink, pw_bd, pool_scale, hy_w, hy_b, g_a, g_b, B, L):
    T = B * L
    rows = MIX_NB * BLOCK
    nb = L // BLOCK
    ns = L // rows
    rps = rows // SUBLANES
    n8 = T // SUBLANES
    cur = lambda b, l: (b * ns + l, 0)
    prev = lambda b, l: (b * nb + jnp.maximum(l * MIX_NB - 1, 0), 0)
    nxt = lambda b, l: (b * nb + jnp.minimum((l + 1) * MIX_NB, nb - 1), 0)
    prev8 = lambda b, l: (jnp.maximum((b * ns + l) * rps - 1, 0), 0)
    next8 = lambda b, l: (jnp.minimum((b * ns + l + 1) * rps, n8 - 1), 0)
    fix = lambda b, l: (0, 0)

    def bias_spec(sb):
        def index(b, l):
            case = jnp.int32(0)
            if sb == 0:
                case = case + (l == 0).astype(jnp.int32)
            if sb == MIX_NB - 1:
                case = case + 2 * (l == ns - 1).astype(jnp.int32)
            return (case, 0, 0, 0)
        return pl.BlockSpec((None, N_Q_HEADS, 3 * BLOCK, BLOCK), index)

    bias = _attn_bias()
    n_pairs = MIX_NB * N_Q_HEADS
    return pl.pallas_call(
        functools.partial(_mix_kernel, seq_len=L),
        out_shape=[jax.ShapeDtypeStruct((T, D_AB), BF16),
                   jax.ShapeDtypeStruct((B, FFT_JH, D_HY_IN // LANES, nb * FFT_J, LANES), F32)],
        grid=(B, ns),
        in_specs=[pl.BlockSpec(memory_space=pltpu.SMEM)] + [bias_spec(sb) for sb in range(MIX_NB)] + [
                  pl.BlockSpec((MIX_NB, D_ATTN, BLOCK), lambda b, l: (b * ns + l, 0, 0)),
                  pl.BlockSpec((BLOCK, 2 * D_KV), prev),
                  pl.BlockSpec((rows, 2 * D_KV), cur),
                  pl.BlockSpec((BLOCK, 2 * D_KV), nxt),
                  pl.BlockSpec((rows, D_PH), cur),
                  pl.BlockSpec((SUBLANES, D_PH), prev8),
                  pl.BlockSpec((SUBLANES, D_PH), next8),
                  pl.BlockSpec((D_POOL, D_POOL), fix),
                  pl.BlockSpec((1, D_POOL), fix),
                  pl.BlockSpec((3, D_HY_IN), fix),
                  pl.BlockSpec((1, D_HY_IN), fix),
                  pl.BlockSpec((1, D_ATTN), fix),
                  pl.BlockSpec((1, D_POOL), fix)],
        out_specs=[pl.BlockSpec((rows, D_AB), cur),
                   pl.BlockSpec((None, FFT_JH, D_HY_IN // LANES, MIX_NB * FFT_J, LANES),
                                lambda b, l: (b, 0, 0, l, 0))],
        scratch_shapes=[pltpu.VMEM((rows + 2 * SUBLANES, D_PH), F32),
                        pltpu.VMEM((MIX_NB, D_ATTN, BLOCK), F32),
                        pltpu.VMEM((n_pairs, 3 * BLOCK, BLOCK), F32),
                        pltpu.VMEM((n_pairs, 3 * BLOCK, BLOCK), BF16),
                        pltpu.VMEM((n_pairs, 1, BLOCK), F32)],
        compiler_params=_params(2),
        name="mix",
    )(sink, *([bias] * MIX_NB), q, kv, kv, kv, ph, ph, ph, pw_bd, pool_scale, hy_w, hy_b, g_a, g_b)


def _split3(x, axis):
    hi = x.astype(BF16)
    lo = (x - hi.astype(F32)).astype(BF16)
    return jnp.concatenate([hi, lo, hi], axis=axis)


def _stack3(w):
    hi = w.astype(BF16)
    lo = (w - hi.astype(F32)).astype(BF16)
    return jnp.concatenate([hi, hi, lo], axis=0)


def _filter_kernel(z_ref, t_ref, w1_ref, b1_ref, f1_ref, w2_ref, b2_ref, f2_ref, w3_ref, dl_ref, o_ref, *, seq_len):
    h = jnp.sin(f1_ref[...] * (_dot(w1_ref[...], z_ref[...]) + b1_ref[...]))
    h = jnp.sin(f2_ref[...] * (_dot(w2_ref[...], _split3(h, 0)) + b2_ref[...]))
    h = lax.dot_general(_split3(h, 0), w3_ref[...], (((0,), (0,)), ((), ())),
                        preferred_element_type=F32)
    n = pl.program_id(0) * h.shape[0] + lax.broadcasted_iota(jnp.int32, (h.shape[0], 1), 0)
    decay = jnp.where(n == seq_len, 0.0, jnp.exp(-t_ref[...] * dl_ref[...]))
    for o in range(HYENA_ORDER):
        _store_tiles(o_ref, (o,), 0, h[:, o * D_HYENA:(o + 1) * D_HYENA] * decay)


def _filter_kernels(L, w1, b1, fr1, w2, b2, fr2, w3):
    pos = jnp.arange(2 * L, dtype=jnp.int32)
    tpos = jnp.where(pos < L, pos, (2 * L - pos) % L).astype(F32)[:, None]
    t_norm = tpos * (1.0 / (L - 1))
    bands = jnp.linspace(1e-4, FILTER_BANDS - 1, FILTER_BANDS, dtype=F32)[None, :]
    ang = 2.0 * math.pi * tpos * bands / L
    z = jnp.concatenate([t_norm, jnp.cos(ang), -jnp.sin(ang)], axis=-1)
    z3 = jnp.pad(_split3(z, 1), ((0, 0), (0, LANES - 3 * FILTER_EMB))).T
    w1s = jnp.pad(_stack3(w1), ((0, LANES - 3 * FILTER_EMB), (0, 0)))
    w3d = w3.reshape(FILTER_HIDDEN, HYENA_ORDER, 2, D_HYENA).transpose(2, 0, 1, 3).reshape(
        2, FILTER_HIDDEN, HYENA_ORDER * D_HYENA)
    w3s = jnp.stack([_stack3(w3d[0]), _stack3(w3d[1])])
    max_decay = math.log(DECAY_TARGET) / DECAY_FAST_PCT
    min_decay = math.log(DECAY_TARGET) / DECAY_SLOW_PCT
    dl = jnp.abs(jnp.linspace(min_decay, max_decay, D_HYENA, dtype=F32))[None, :]
    tl = 512
    half = L // tl
    row = lambda i: (i, 0)
    fix = lambda i: (0, 0)
    return pl.pallas_call(
        functools.partial(_filter_kernel, seq_len=L),
        out_shape=jax.ShapeDtypeStruct((HYENA_ORDER, FFT_JH, HY_CT, 2 * L // FFT_N2 * FFT_J, LANES), F32),
        grid=(2 * L // tl,),
        in_specs=[pl.BlockSpec((LANES, tl), lambda i: (0, i)),
                  pl.BlockSpec((tl, 1), row),
                  pl.BlockSpec((FILTER_HIDDEN, LANES), fix),
                  pl.BlockSpec((FILTER_HIDDEN, 1), fix),
                  pl.BlockSpec((FILTER_HIDDEN, 1), fix),
                  pl.BlockSpec((FILTER_HIDDEN, 3 * FILTER_HIDDEN), fix),
                  pl.BlockSpec((FILTER_HIDDEN, 1), fix),
                  pl.BlockSpec((FILTER_HIDDEN, 1), fix),
                  pl.BlockSpec((None, 3 * FILTER_HIDDEN, HYENA_ORDER * D_HYENA), lambda i: (i // half, 0, 0)),
                  pl.BlockSpec((1, D_HYENA), fix)],
        out_specs=pl.BlockSpec((HYENA_ORDER, FFT_JH, HY_CT, tl // FFT_N2 * FFT_J, LANES), lambda i: (0, 0, 0, i, 0)),
        compiler_params=_params(1),
        name="hyena_filters",
    )(z3, t_norm, w1s.T, b1[:, None], fr1[:, None], _stack3(w2).T, b2[:, None], fr2[:, None], w3s, dl)


def _dft_mats(L):
    N = 2 * L
    n1n = N // FFT_N2
    r = n1n // 2
    assert r % FFT_TF == 0, "sequence length must be a multiple of FFT_TF * FFT_N2 / 2"
    mh = r + FFT_TF
    f1 = jnp.arange(mh, dtype=jnp.int32)[:, None]
    n1 = jnp.arange(n1n, dtype=jnp.int32)[None, :]
    keep = (f1 <= r).astype(F32)
    th = (2.0 * math.pi / n1n) * ((f1 * n1) % n1n).astype(F32)
    c1, s1 = keep * jnp.cos(th), keep * jnp.sin(th)
    fwd_full = jnp.concatenate([c1, -s1], axis=0).astype(BF16)
    fwd_half = fwd_full[:, :r]
    w = jnp.where((f1 == 0) | (f1 == r), 1.0, 2.0)
    inv_half = jnp.concatenate([(w * c1)[:, :r].T, (-w * s1)[:, :r].T], axis=1).astype(BF16)
    i2 = jnp.arange(2 * FFT_N2, dtype=jnp.int32)
    part, idx = i2 // FFT_N2, i2 % FFT_N2
    pa = (2.0 * math.pi / N) * ((jnp.arange(mh, dtype=jnp.int32)[:, None] * idx[None, :]) % N).astype(F32)
    ca, sa = jnp.cos(pa), jnp.sin(pa)
    mb = (idx[:, None] * idx[None, :] * n1n + (part[:, None] - part[None, :]) * (N // 4)) % N
    pb = (2.0 * math.pi / N) * mb.astype(F32)
    cb, sb = jnp.cos(pb), jnp.sin(pb)
    g = (ca[:, None, :] * cb[None] - sa[:, None, :] * sb[None]).astype(BF16)
    ginv = (ca[:, :, None] * cb.T[None] - sa[:, :, None] * sb.T[None]).astype(BF16)
    return dict(n1=n1n, r=r, mh=mh, fwd_full=fwd_full, fwd_half=fwd_half, inv_half=inv_half, g=g, ginv=ginv)


def _fft_a_kernel(x_ref, f_ref, a_ref, *, tb, mh):
    f = f_ref[...]
    r = x_ref.shape[1] // FFT_J
    for b in range(tb):
        for j in range(FFT_J):
            a = _dot(f, x_ref[b, pl.ds(j, r, stride=FFT_J), :].astype(BF16))
            a_ref[b, pl.ds(j, mh, stride=FFT_J), :] = _pack_pair(a[:mh], a[mh:])


def _fft_a(x5, coff, fmat, mh, tb):
    B, rj = x5.shape[0], x5.shape[3]
    blk = lambda b, j, c: (b, j, c, 0, 0)
    return pl.pallas_call(
        functools.partial(_fft_a_kernel, tb=tb, mh=mh),
        out_shape=jax.ShapeDtypeStruct((B, FFT_JH, HY_CT, mh * FFT_J, LANES), jnp.uint32),
        grid=(B // tb, FFT_JH, HY_CT),
        in_specs=[pl.BlockSpec((tb, None, None, rj, LANES), lambda b, j, c: (b, j, c + coff, 0, 0)),
                  pl.BlockSpec((2 * mh, rj // FFT_J), lambda b, j, c: (0, 0))],
        out_specs=pl.BlockSpec((tb, None, None, mh * FFT_J, LANES), blk),
        compiler_params=_params(3),
        name="fft_outer_fwd",
    )(x5, fmat)


def _fft_spec_kernel(a_ref, g_ref, kr_ref, ki_ref, *, scale):
    for f in range(FFT_TF):
        x = _dot(g_ref[f], _unpack_pair(_load_tiles(a_ref, (), f * FFT_J, 1, HY_CT))) * scale
        kr_ref[0, f] = x[:FFT_N2]
        ki_ref[0, f] = x[FFT_N2:]


def _tile_spec():
    return pl.BlockSpec((None, FFT_JH, HY_CT, FFT_TF * FFT_J, LANES), lambda f, b: (b, 0, 0, f, 0))


def _fft_spec(a, g, scale):
    B, mh = a.shape[0], a.shape[3] // FFT_J
    nat = pl.BlockSpec((1, FFT_TF, FFT_N2, D_HYENA), lambda f, b: (b, f, 0, 0))
    out = jax.ShapeDtypeStruct((B, mh, FFT_N2, D_HYENA), F32)
    return pl.pallas_call(
        functools.partial(_fft_spec_kernel, scale=scale),
        out_shape=[out, out],
        grid=(mh // FFT_TF, B),
        in_specs=[_tile_spec(), pl.BlockSpec((FFT_TF, 2 * FFT_N2, 2 * FFT_N2), lambda f, b: (f, 0, 0))],
        out_specs=[nat, nat],
        compiler_params=_params(2),
        name="fft_inner_fwd",
    )(a, g)


def _fft_b_kernel(a_ref, g_ref, gi_ref, kr_ref, ki_ref, y_ref):
    for f in range(FFT_TF):
        x = _dot(g_ref[f], _unpack_pair(_load_tiles(a_ref, (), f * FFT_J, 1, HY_CT)))
        xr, xi = x[:FFT_N2], x[FFT_N2:]
        kr, ki = kr_ref[0, f], ki_ref[0, f]
        z = jnp.concatenate([xr * kr - xi * ki, xr * ki + xi * kr], axis=0).astype(BF16)
        y = _dot(gi_ref[f], z)
        _store_tiles(y_ref, (), f * FFT_J, _pack_pair(y[:FFT_N2], y[FFT_N2:]))


def _fft_b(a, g, ginv, kr, ki, o):
    B, mh = a.shape[0], a.shape[3] // FFT_J
    mat = pl.BlockSpec((FFT_TF, 2 * FFT_N2, 2 * FFT_N2), lambda f, b: (f, 0, 0))
    spec = pl.BlockSpec((1, FFT_TF, FFT_N2, D_HYENA), lambda f, b: (o, f, 0, 0))
    return pl.pallas_call(
        _fft_b_kernel,
        out_shape=jax.ShapeDtypeStruct(a.shape, jnp.uint32),
        grid=(mh // FFT_TF, B),
        in_specs=[_tile_spec(), mat, mat, spec, spec],
        out_specs=_tile_spec(),
        compiler_params=_params(2),
        name="fft_inner",
    )(a, g, ginv, kr, ki)


def _fft_c_kernel(y_ref, c_ref, u_ref, gate_ref, bias_ref, o_ref, *, tb):
    cm = c_ref[...]
    bias = bias_ref[...]
    mh, r = y_ref.shape[1] // FFT_J, u_ref.shape[1] // FFT_J
    for b in range(tb):
        for j in range(FFT_J):
            y = _dot(cm, _unpack_pair(y_ref[b, pl.ds(j, mh, stride=FFT_J), :]))
            rows = pl.ds(j, r, stride=FFT_J)
            o_ref[b, rows, :] = gate_ref[b, rows, :] * (y + bias * u_ref[b, rows, :])


def _fft_c(y, cmat, u5, uoff, gate5, goff, bias, tb):
    B, mhj = y.shape[0], y.shape[3]
    rj = cmat.shape[0] * FFT_J
    blk = lambda b, j, c: (b, j, c, 0, 0)
    return pl.pallas_call(
        functools.partial(_fft_c_kernel, tb=tb),
        out_shape=jax.ShapeDtypeStruct((B, FFT_JH, HY_CT, rj, LANES), F32),
        grid=(B // tb, FFT_JH, HY_CT),
        in_specs=[pl.BlockSpec((tb, None, None, mhj, LANES), blk),
                  pl.BlockSpec(cmat.shape, lambda b, j, c: (0, 0)),
                  pl.BlockSpec((tb, None, None, rj, LANES), lambda b, j, c: (b, j, c + uoff, 0, 0)),
                  pl.BlockSpec((tb, None, None, rj, LANES), lambda b, j, c: (b, j, c + goff, 0, 0)),
                  pl.BlockSpec((1, LANES), lambda b, j, c: (0, c))],
        out_specs=pl.BlockSpec((tb, None, None, rj, LANES), blk),
        compiler_params=_params(3),
        name="fft_outer_inv",
    )(y, cmat, u5, gate5, bias)


def _filter_spectra(kern, mats, L):
    return _fft_spec(_fft_a(kern, 0, mats["fwd_full"], mats["mh"], 1), mats["g"], 1.0 / (2 * L))


def _hyena(hy5, kr, ki, hy_bias, mats, tb):
    z5 = hy5
    for o in range(HYENA_ORDER):
        a = _fft_a(z5, 0, mats["fwd_half"], mats["mh"], tb)
        y = _fft_b(a, mats["g"], mats["ginv"], kr, ki, o)
        z5 = _fft_c(y, mats["inv_half"], z5, 0, hy5, (o + 1) * HY_CT, hy_bias[o][None, :], tb)
    return z5


def _out_ffn_in_kernel(x_ref, cab_ref, c_ref, gc_ref, wab_ref, wc_ref, g2_ref, wi_ref, xn_ref, u_ref):
    c = _load_tiles(c_ref, (), 0, c_ref.shape[2] // FFT_J, HY_CT)
    cn = _rms(c, gc_ref[...]).astype(BF16)
    xn = x_ref[...] + _dot(cab_ref[...], wab_ref[...]) + _dot(cn, wc_ref[...])
    xn_ref[...] = xn
    h = _rms(xn, g2_ref[...]).astype(BF16)
    for j in range(0, 2 * D_FF, FFN_IN_CHUNK):
        u_ref[:, j:j + FFN_IN_CHUNK] = _dot(h, wi_ref[:, j:j + FFN_IN_CHUNK])


def _out_ffn_in(x2, cab, c5, g_c, w_ab, w_c, g2, w_in):
    T = x2.shape[0]
    tm = ROW_TILE
    tps = c5.shape[3] // FFT_J * FFT_N2 // tm
    row = lambda i: (i, 0)
    fix = lambda i: (0, 0)
    return pl.pallas_call(
        _out_ffn_in_kernel,
        out_shape=[jax.ShapeDtypeStruct((T, D_MODEL), F32),
                   jax.ShapeDtypeStruct((T, 2 * D_FF), F32)],
        grid=(T // tm,),
        in_specs=[pl.BlockSpec((tm, D_MODEL), row),
                  pl.BlockSpec((tm, D_AB), row),
                  pl.BlockSpec((None, FFT_JH, HY_CT, tm // FFT_N2 * FFT_J, LANES),
                               lambda i: (i // tps, 0, 0, i % tps, 0)),
                  pl.BlockSpec((1, D_HYENA), fix),
                  pl.BlockSpec((D_AB, D_MODEL), fix),
                  pl.BlockSpec((D_HYENA, D_MODEL), fix),
                  pl.BlockSpec((1, D_MODEL), fix),
                  pl.BlockSpec((D_MODEL, 2 * D_FF), fix)],
        out_specs=[pl.BlockSpec((tm, D_MODEL), row),
                   pl.BlockSpec((tm, 2 * D_FF), row)],
        compiler_params=_params(1),
        name="out_ffn_in",
    )(x2, cab, c5, g_c, w_ab, w_c, g2, w_in)


def _ffn_out_kernel(u_ref, up_ref, un_ref, xn_ref, cw_ref, cb_ref, wo_ref, o_ref, acc_ref, *, tiles_per_seq):
    i = pl.program_id(0)
    is_first = (i % tiles_per_seq) == 0
    is_last = (i % tiles_per_seq) == tiles_per_seq - 1
    tm = u_ref.shape[0]
    row = lax.broadcasted_iota(jnp.int32, (tm, 1), 0)

    def conv(c0):
        cs = slice(c0, c0 + FFN_CHUNK)
        uc = u_ref[:, cs]
        before = jnp.where(is_first, 0.0, up_ref[SUBLANES - 1:SUBLANES, cs])
        after = jnp.where(is_last, 0.0, un_ref[0:1, cs])
        um = jnp.where(row == 0, before, pltpu.roll(uc, 1, axis=0))
        up = jnp.where(row == tm - 1, after, pltpu.roll(uc, tm - 1, axis=0))
        return um * cw_ref[0:1, cs] + uc * cw_ref[1:2, cs] + up * cw_ref[2:3, cs] + cb_ref[:, cs]

    for j in range(D_FF // FFN_CHUNK):
        gate = conv(j * FFN_CHUNK)
        half_val = conv(D_FF + j * FFN_CHUNK)
        act = gate * (1.0 + lax.erf(gate * math.sqrt(0.5))) * half_val
        part = _dot(act.astype(BF16), wo_ref[j * FFN_CHUNK:(j + 1) * FFN_CHUNK, :])
        if j == 0:
            acc_ref[...] = part
        else:
            acc_ref[...] += part
    o_ref[...] = xn_ref[...] + acc_ref[...]


def _ffn_out(u, xn, cw, cb, w_out, L):
    T = xn.shape[0]
    tm = ROW_TILE
    hpt = tm // SUBLANES
    n8 = T // SUBLANES
    row = lambda i: (i, 0)
    fix = lambda i: (0, 0)
    return pl.pallas_call(
        functools.partial(_ffn_out_kernel, tiles_per_seq=L // tm),
        out_shape=jax.ShapeDtypeStruct((T, D_MODEL), F32),
        grid=(T // tm,),
        in_specs=[pl.BlockSpec((tm, 2 * D_FF), row),
                  pl.BlockSpec((SUBLANES, 2 * D_FF), lambda i: (jnp.maximum(i * hpt - 1, 0), 0)),
                  pl.BlockSpec((SUBLANES, 2 * D_FF), lambda i: (jnp.minimum((i + 1) * hpt, n8 - 1), 0)),
                  pl.BlockSpec((tm, D_MODEL), row),
                  pl.BlockSpec((3, 2 * D_FF), fix),
                  pl.BlockSpec((1, 2 * D_FF), fix),
                  pl.BlockSpec((D_FF, D_MODEL), fix)],
        out_specs=pl.BlockSpec((tm, D_MODEL), row),
        scratch_shapes=[pltpu.VMEM((tm, D_MODEL), F32)],
        compiler_params=_params(1),
        name="ffn_out",
    )(u, u, u, xn, cw, cb, w_out)


def _head_ones(n_heads):
    return jnp.kron(jnp.eye(n_heads, dtype=F32), jnp.ones((HEAD_DIM, HEAD_DIM), F32)).astype(BF16)


def _layer(x2, B, L, mats, spectra, lw):
    q, kv, ph = _in_proj(x2, lw["norm1_g"], lw["wq_t"], lw["w_in"], lw["ek"], lw["qg"], lw["kg"])
    cab, hy = _mix(q, kv, ph, lw["sink"], lw["pw_bd"], lw["pool_scale"], lw["hy_w"], lw["hy_b"],
                   lw["g_a"], lw["g_b"], B, L)
    tb = math.gcd(B, 8)
    c5 = _hyena(hy, spectra[0], spectra[1], lw["hy_bias"], mats, tb)
    xn, u = _out_ffn_in(x2, cab, c5, lw["g_c"], lw["w_ab"], lw["w_c"],
                        lw["norm2_g"], lw["w_ffn_in"])
    return _ffn_out(u, xn, lw["ffn_cw"], lw["ffn_cb"], lw["w_ffn_out"], L)


def kernel(x_prompt, x_sample, norm1_g, w_in, q_norm_g, k_norm_g, attn_sink, pool_w, pool_scale, hy_conv_w,
           hy_conv_b, filt_w1, filt_b1, filt_freq1, filt_w2, filt_b2, filt_freq2, filt_w3, hy_bias, out_norm_g,
           w_out, norm2_g, w_ffn_in, ffn_conv_w, ffn_conv_b, w_ffn_out):
    groups = [x_prompt, x_sample]
    lens = sorted({g.shape[1] for g in groups})
    mats = {L: _dft_mats(L) for L in lens}
    ek = _head_ones(N_KV_HEADS)
    xs = [g.reshape(-1, D_MODEL) for g in groups]
    glu_scale = jnp.concatenate([jnp.ones((D_FF,), F32), jnp.full((D_FF,), 0.5, F32)])[None, :]
    for l in range(DEPTH):
        lw = dict(
            norm1_g=norm1_g[l][None, :], wq_t=w_in[l, :, :D_ATTN].T.astype(BF16), w_in=w_in[l, :, D_ATTN:].astype(BF16),
            ek=ek, kg=jnp.tile(k_norm_g[l], N_KV_HEADS)[None, :],
            qg=(jnp.tile(q_norm_g[l], N_Q_HEADS) * (1.0 / math.sqrt(HEAD_DIM)))[:, None],
            sink=attn_sink[l],
            pw_bd=jax.scipy.linalg.block_diag(*[pool_w[l, g] for g in range(len(POOL_WINDOWS))]).astype(BF16),
            pool_scale=pool_scale[l][None, :], hy_w=hy_conv_w[l], hy_b=hy_conv_b[l][None, :],
            g_a=out_norm_g[l, :D_ATTN][None, :], g_b=out_norm_g[l, D_ATTN:D_AB][None, :],
            g_c=out_norm_g[l, D_AB:][None, :], hy_bias=hy_bias[l],
            w_ab=w_out[l, :D_AB].astype(BF16), w_c=w_out[l, D_AB:].astype(BF16),
            norm2_g=norm2_g[l][None, :], w_ffn_in=w_ffn_in[l].astype(BF16),
            ffn_cw=ffn_conv_w[l] * glu_scale, ffn_cb=(ffn_conv_b[l] * glu_scale[0])[None, :],
            w_ffn_out=w_ffn_out[l].astype(BF16),
        )
        spectra = {}
        for L in lens:
            kern = _filter_kernels(L, filt_w1[l], filt_b1[l], filt_freq1[l], filt_w2[l], filt_b2[l], filt_freq2[l],
                                   filt_w3[l])
            spectra[L] = _filter_spectra(kern, mats[L], L)
        xs = [_layer(x2, g.shape[0], g.shape[1], mats[g.shape[1]], spectra[g.shape[1]], lw)
              for x2, g in zip(xs, groups)]
    return tuple(x2.reshape(g.shape) for x2, g in zip(xs, groups))
```

```python
import functools
import math

import jax
import jax.numpy as jnp
from jax import lax
from jax.experimental import pallas as pl
from jax.experimental.pallas import tpu as pltpu

F32 = jnp.float32
BF16 = jnp.bfloat16

D_MODEL = 1024
DEPTH = 2
HEAD_DIM = 64
N_Q_HEADS = 8
N_KV_HEADS = 2
GQA_GROUP = N_Q_HEADS // N_KV_HEADS
D_ATTN = N_Q_HEADS * HEAD_DIM
D_KV = N_KV_HEADS * HEAD_DIM
WINDOW = 128
BLOCK = 128
POOL_WINDOWS = (2, 4, 8, 16)
D_POOL = 256
POOL_GROUP_DIM = D_POOL // len(POOL_WINDOWS)
D_HYENA = 256
HYENA_ORDER = 2
FILTER_BANDS = 16
FILTER_EMB = 1 + 2 * FILTER_BANDS
FILTER_HIDDEN = 64
N_FILTERS = 2 * HYENA_ORDER
DECAY_FAST_PCT = 0.3
DECAY_SLOW_PCT = 1.5
DECAY_TARGET = 1e-2
D_HY_IN = (HYENA_ORDER + 1) * D_HYENA
D_PH = D_POOL + D_HY_IN
D_IN_PROJ = D_ATTN + 2 * D_KV + D_PH
D_CAT = D_ATTN + D_POOL + D_HYENA
D_AB = D_ATTN + D_POOL
D_FF = 2816
EPS = 1e-6
NEG_INF = -1e30

LANES = 128
SUBLANES = 8
FFT_N2 = 128
FFT_J = SUBLANES
FFT_TF = 8
FFT_JH = FFT_N2 // FFT_J
HY_CT = D_HYENA // LANES
FFN_CHUNK = 256
FFN_IN_CHUNK = 512
ROW_TILE = 512
MIX_NB = 2
VMEM_LIMIT = 56 * 1024 * 1024


def _params(n_axes):
    return pltpu.CompilerParams(dimension_semantics=("arbitrary",) * n_axes, vmem_limit_bytes=VMEM_LIMIT)


def _rms(x, g):
    return x * lax.rsqrt(jnp.mean(x * x, axis=-1, keepdims=True) + EPS) * g


def _dot(a, b):
    return jnp.dot(a, b, preferred_element_type=F32)


def _store_tiles(ref, lead, row0, val):
    for nl in range(val.shape[0] // FFT_N2):
        for jh in range(FFT_JH):
            for ct in range(val.shape[1] // LANES):
                r0 = nl * FFT_N2 + jh * FFT_J
                ref[lead + (jh, ct, pl.ds(row0 + nl * FFT_J, FFT_J), slice(None))] = (
                    val[r0:r0 + FFT_J, ct * LANES:(ct + 1) * LANES])


def _load_tiles(ref, lead, row0, n_local, n_ct):
    return jnp.concatenate(
        [jnp.concatenate([ref[lead + (jh, ct, pl.ds(row0 + nl * FFT_J, FFT_J), slice(None))]
                          for nl in range(n_local) for jh in range(FFT_JH)], axis=0)
         for ct in range(n_ct)], axis=1)


def _pack_pair(re, im):
    hi = lax.bitcast_convert_type(re.astype(BF16).astype(F32), jnp.uint32)
    lo = lax.bitcast_convert_type(im.astype(BF16).astype(F32), jnp.uint32)
    return hi | (lo >> 16)


def _unpack_pair(w):
    re = lax.bitcast_convert_type(w & jnp.uint32(0xFFFF0000), F32)
    im = lax.bitcast_convert_type(w << 16, F32)
    return jnp.concatenate([re, im], axis=0).astype(BF16)


def _in_proj_kernel(x_ref, g_ref, wq_ref, w_ref, ek_ref, qg_ref, kg_ref, q_ref, kv_ref, ph_ref):
    h = _rms(x_ref[...], g_ref[...]).astype(BF16)
    tm = h.shape[0]

    qt = lax.dot_general(wq_ref[...], h, (((1,), (1,)), ((), ())), preferred_element_type=F32)
    qt = qt.reshape(N_Q_HEADS, HEAD_DIM, tm)
    qt = qt * lax.rsqrt(jnp.mean(qt * qt, axis=1, keepdims=True) + EPS)
    qt = (qt.reshape(D_ATTN, tm) * qg_ref[...]).astype(BF16)
    for j in range(tm // BLOCK):
        q_ref[j] = qt[:, j * BLOCK:(j + 1) * BLOCK]

    p = _dot(h, w_ref[...])
    k = p[:, :D_KV]
    ss = _dot((k * k).astype(BF16), ek_ref[...])
    kv_ref[:, :D_KV] = (k * lax.rsqrt(ss * (1.0 / HEAD_DIM) + EPS) * kg_ref[...]).astype(BF16)
    kv_ref[:, D_KV:] = p[:, D_KV:2 * D_KV].astype(BF16)
    ph_ref[...] = p[:, 2 * D_KV:]


def _in_proj(x2, g, wq_t, w, ek, qg, kg):
    T = x2.shape[0]
    tm = ROW_TILE
    row = lambda i: (i, 0)
    fix = lambda i: (0, 0)
    return pl.pallas_call(
        _in_proj_kernel,
        out_shape=[jax.ShapeDtypeStruct((T // BLOCK, D_ATTN, BLOCK), BF16),
                   jax.ShapeDtypeStruct((T, 2 * D_KV), BF16),
                   jax.ShapeDtypeStruct((T, D_PH), F32)],
        grid=(T // tm,),
        in_specs=[pl.BlockSpec((tm, D_MODEL), row),
                  pl.BlockSpec((1, D_MODEL), fix),
                  pl.BlockSpec((D_ATTN, D_MODEL), fix),
                  pl.BlockSpec((D_MODEL, 2 * D_KV + D_PH), fix),
                  pl.BlockSpec((D_KV, D_KV), fix),
                  pl.BlockSpec((D_ATTN, 1), fix),
                  pl.BlockSpec((1, D_KV), fix)],
        out_specs=[pl.BlockSpec((tm // BLOCK, D_ATTN, BLOCK), lambda i: (i, 0, 0)),
                   pl.BlockSpec((tm, 2 * D_KV), row),
                   pl.BlockSpec((tm, D_PH), row)],
        compiler_params=_params(1),
        name="in_proj",
    )(x2, g, wq_t, w, ek, qg, kg)


def _mix_kernel(sink_ref, *refs, seq_len):
    bias_refs = refs[:MIX_NB]
    (q_ref, kvp_ref, kvc_ref, kvn_ref, phc_ref, php_ref, phn_ref, pw_ref, ps_ref, hw_ref, hb_ref, ga_ref, gb_ref,
     cat_ref, hy_ref, ext_ref, att_ref, s_ref, p_ref, rden_ref) = refs[MIX_NB:]
    l = pl.program_id(1)
    is_first = l == 0
    is_last = l == pl.num_programs(1) - 1
    rows = MIX_NB * BLOCK

    tn = (((0,), (0,)), ((), ()))

    def keys(sb, cols):
        def blk(i):
            if i < 0:
                return kvp_ref[:, cols]
            if i >= MIX_NB:
                return kvn_ref[:, cols]
            return kvc_ref[i * BLOCK:(i + 1) * BLOCK, cols]
        return jnp.concatenate([blk(sb - 1), blk(sb), blk(sb + 1)], axis=0)

    pairs = [(sb, g) for sb in range(MIX_NB) for g in range(N_KV_HEADS)]
    gw = GQA_GROUP * BLOCK
    for i, (sb, g) in enumerate(pairs):
        qg = jnp.concatenate([q_ref[sb, h * HEAD_DIM:(h + 1) * HEAD_DIM, :]
                              for h in range(g * GQA_GROUP, (g + 1) * GQA_GROUP)], axis=1)
        st = _dot(keys(sb, slice(g * HEAD_DIM, (g + 1) * HEAD_DIM)), qg)
        for j in range(GQA_GROUP):
            s_ref[i, :, j * BLOCK:(j + 1) * BLOCK] = st[:, j * BLOCK:(j + 1) * BLOCK] + bias_refs[sb][g * GQA_GROUP + j]
    for i, (sb, g) in enumerate(pairs):
        for j in range(GQA_GROUP):
            cols = slice(j * BLOCK, (j + 1) * BLOCK)
            st = s_ref[i, :, cols]
            sk = sink_ref[g * GQA_GROUP + j]
            m = jnp.maximum(jnp.max(st, axis=0, keepdims=True), sk)
            e = jnp.exp(st - m)
            rden_ref[i, :, cols] = 1.0 / (jnp.sum(e, axis=0, keepdims=True) + jnp.exp(sk - m))
            p_ref[i, :, cols] = e.astype(BF16)
    for i, (sb, g) in enumerate(pairs):
        vh = keys(sb, slice(D_KV + g * HEAD_DIM, D_KV + (g + 1) * HEAD_DIM))
        ot = lax.dot_general(vh, p_ref[i], tn, preferred_element_type=F32) * rden_ref[i]
        for j in range(GQA_GROUP):
            h = g * GQA_GROUP + j
            att_ref[sb, h * HEAD_DIM:(h + 1) * HEAD_DIM, :] = ot[:, j * BLOCK:(j + 1) * BLOCK]
    for sb in range(MIX_NB):
        cat_ref[sb * BLOCK:(sb + 1) * BLOCK, :D_ATTN] = _rms(att_ref[sb].T, ga_ref[...]).astype(BF16)

    ext_ref[0:SUBLANES, :] = jnp.where(is_first, 0.0, php_ref[...])
    ext_ref[SUBLANES:SUBLANES + rows, :] = phc_ref[...]
    ext_ref[SUBLANES + rows:, :] = jnp.where(is_last, 0.0, phn_ref[...])

    t = l * rows + lax.broadcasted_iota(jnp.int32, (rows, 1), 0)
    lane = lax.broadcasted_iota(jnp.int32, (1, LANES), 1)

    def window_mean(w, col0):
        lo = jnp.clip(t - w // 2, 0, seq_len)
        hi = jnp.clip(t - w // 2 + w, 0, seq_len)
        tot = ext_ref[SUBLANES - w // 2:SUBLANES - w // 2 + rows, col0:col0 + LANES]
        for d in range(1 - w // 2, w // 2):
            tot = tot + ext_ref[SUBLANES + d:SUBLANES + d + rows, col0:col0 + LANES]
        return tot / (hi - lo).astype(F32)

    gpt = LANES // POOL_GROUP_DIM
    means = [jnp.where(lane < POOL_GROUP_DIM, window_mean(POOL_WINDOWS[gpt * i], i * LANES),
                       window_mean(POOL_WINDOWS[gpt * i + 1], i * LANES)) for i in range(D_POOL // LANES)]
    u = ext_ref[SUBLANES:SUBLANES + rows, 0:D_POOL]
    pooled = _dot((jnp.concatenate(means, axis=1) - u).astype(BF16), pw_ref[...]) * ps_ref[...]
    cat_ref[:, D_ATTN:] = _rms(pooled, gb_ref[...]).astype(BF16)

    hp = ext_ref[SUBLANES - 1:SUBLANES - 1 + rows, D_POOL:]
    hc = ext_ref[SUBLANES:SUBLANES + rows, D_POOL:]
    hn = ext_ref[SUBLANES + 1:SUBLANES + 1 + rows, D_POOL:]
    _store_tiles(hy_ref, (), 0, hp * hw_ref[0:1, :] + hc * hw_ref[1:2, :] + hn * hw_ref[2:3, :] + hb_ref[...])


def _attn_bias():
    row = jnp.arange(BLOCK, dtype=jnp.int32)[None, :]
    col = jnp.arange(3 * BLOCK, dtype=jnp.int32)[:, None]
    dist = jnp.abs(row + BLOCK - col)
    slopes = 2.0 ** (-8.0 * (jnp.arange(N_Q_HEADS, dtype=F32) + 1.0) / N_Q_HEADS)
    bias = -slopes[:, None, None] * dist.astype(F32)[None]
    out = []
    for case in range(4):
        valid = dist <= WINDOW
        if case & 1:
            valid = valid & (col >= BLOCK)
        if case & 2:
            valid = valid & (col < 2 * BLOCK)
        out.append(jnp.where(valid[None], bias, NEG_INF))
    return jnp.stack(out)


def _mix(q, kv, ph, sink, pw_bd, pool_scale, hy_w, hy_b, g_a, g_b, B, L):
    T = B * L
    rows = MIX_NB * BLOCK
    nb = L // BLOCK
    ns = L // rows
    rps = rows // SUBLANES
    n8 = T // SUBLANES
    cur = lambda b, l: (b * ns + l, 0)
    prev = lambda b, l: (b * nb + jnp.maximum(l * MIX_NB - 1, 0), 0)
    nxt = lambda b, l: (b * nb + jnp.minimum((l + 1) * MIX_NB, nb - 1), 0)
    prev8 = lambda b, l: (jnp.maximum((b * ns + l) * rps - 1, 0), 0)
    next8 = lambda b, l: (jnp.minimum((b * ns + l + 1) * rps, n8 - 1), 0)
    fix = lambda b, l: (0, 0)

    def bias_spec(sb):
        def index(b, l):
            case = jnp.int32(0)
            if sb == 0:
                case = case + (l == 0).astype(jnp.int32)
            if sb == MIX_NB - 1:
                case = case + 2 * (l == ns - 1).astype(jnp.int32)
            return (case, 0, 0, 0)
        return pl.BlockSpec((None, N_Q_HEADS, 3 * BLOCK, BLOCK), index)

    bias = _attn_bias()
    n_pairs = MIX_NB * N_KV_HEADS
    gw = GQA_GROUP * BLOCK
    return pl.pallas_call(
        functools.partial(_mix_kernel, seq_len=L),
        out_shape=[jax.ShapeDtypeStruct((T, D_AB), BF16),
                   jax.ShapeDtypeStruct((B, FFT_JH, D_HY_IN // LANES, nb * FFT_J, LANES), F32)],
        grid=(B, ns),
        in_specs=[pl.BlockSpec(memory_space=pltpu.SMEM)] + [bias_spec(sb) for sb in range(MIX_NB)] + [
                  pl.BlockSpec((MIX_NB, D_ATTN, BLOCK), lambda b, l: (b * ns + l, 0, 0)),
                  pl.BlockSpec((BLOCK, 2 * D_KV), prev),
                  pl.BlockSpec((rows, 2 * D_KV), cur),
                  pl.BlockSpec((BLOCK, 2 * D_KV), nxt),
                  pl.BlockSpec((rows, D_PH), cur),
                  pl.BlockSpec((SUBLANES, D_PH), prev8),
                  pl.BlockSpec((SUBLANES, D_PH), next8),
                  pl.BlockSpec((D_POOL, D_POOL), fix),
                  pl.BlockSpec((1, D_POOL), fix),
                  pl.BlockSpec((3, D_HY_IN), fix),
                  pl.BlockSpec((1, D_HY_IN), fix),
                  pl.BlockSpec((1, D_ATTN), fix),
                  pl.BlockSpec((1, D_POOL), fix)],
        out_specs=[pl.BlockSpec((rows, D_AB), cur),
                   pl.BlockSpec((None, FFT_JH, D_HY_IN // LANES, MIX_NB * FFT_J, LANES),
                                lambda b, l: (b, 0, 0, l, 0))],
        scratch_shapes=[pltpu.VMEM((rows + 2 * SUBLANES, D_PH), F32),
                        pltpu.VMEM((MIX_NB, D_ATTN, BLOCK), F32),
                        pltpu.VMEM((n_pairs, 3 * BLOCK, gw), F32),
                        pltpu.VMEM((n_pairs, 3 * BLOCK, gw), BF16),
                        pltpu.VMEM((n_pairs, 1, gw), F32)],
        compiler_params=_params(2),
        name="mix",
    )(sink, *([bias] * MIX_NB), q, kv, kv, kv, ph, ph, ph, pw_bd, pool_scale, hy_w, hy_b, g_a, g_b)


def _split3(x, axis):
    hi = x.astype(BF16)
    lo = (x - hi.astype(F32)).astype(BF16)
    return jnp.concatenate([hi, lo, hi], axis=axis)


def _stack3(w):
    hi = w.astype(BF16)
    lo = (w - hi.astype(F32)).astype(BF16)
    return jnp.concatenate([hi, hi, lo], axis=0)


def _filter_kernel(z_ref, t_ref, w1_ref, b1_ref, f1_ref, w2_ref, b2_ref, f2_ref, w3_ref, dl_ref, o_ref, *, seq_len):
    h = jnp.sin(f1_ref[...] * (_dot(w1_ref[...], z_ref[...]) + b1_ref[...]))
    h = jnp.sin(f2_ref[...] * (_dot(w2_ref[...], _split3(h, 0)) + b2_ref[...]))
    h = lax.dot_general(_split3(h, 0), w3_ref[...], (((0,), (0,)), ((), ())),
                        preferred_element_type=F32)
    n = pl.program_id(0) * h.shape[0] + lax.broadcasted_iota(jnp.int32, (h.shape[0], 1), 0)
    decay = jnp.where(n == seq_len, 0.0, jnp.exp(-t_ref[...] * dl_ref[...]))
    for o in range(HYENA_ORDER):
        _store_tiles(o_ref, (o,), 0, h[:, o * D_HYENA:(o + 1) * D_HYENA] * decay)


def _filter_kernels(L, w1, b1, fr1, w2, b2, fr2, w3):
    pos = jnp.arange(2 * L, dtype=jnp.int32)
    tpos = jnp.where(pos < L, pos, (2 * L - pos) % L).astype(F32)[:, None]
    t_norm = tpos * (1.0 / (L - 1))
    bands = jnp.linspace(1e-4, FILTER_BANDS - 1, FILTER_BANDS, dtype=F32)[None, :]
    ang = 2.0 * math.pi * tpos * bands / L
    z = jnp.concatenate([t_norm, jnp.cos(ang), -jnp.sin(ang)], axis=-1)
    z3 = jnp.pad(_split3(z, 1), ((0, 0), (0, LANES - 3 * FILTER_EMB))).T
    w1s = jnp.pad(_stack3(w1), ((0, LANES - 3 * FILTER_EMB), (0, 0)))
    w3d = w3.reshape(FILTER_HIDDEN, HYENA_ORDER, 2, D_HYENA).transpose(2, 0, 1, 3).reshape(
        2, FILTER_HIDDEN, HYENA_ORDER * D_HYENA)
    w3s = jnp.stack([_stack3(w3d[0]), _stack3(w3d[1])])
    max_decay = math.log(DECAY_TARGET) / DECAY_FAST_PCT
    min_decay = math.log(DECAY_TARGET) / DECAY_SLOW_PCT
    dl = jnp.abs(jnp.linspace(min_decay, max_decay, D_HYENA, dtype=F32))[None, :]
    tl = 512
    half = L // tl
    row = lambda i: (i, 0)
    fix = lambda i: (0, 0)
    return pl.pallas_call(
        functools.partial(_filter_kernel, seq_len=L),
        out_shape=jax.ShapeDtypeStruct((HYENA_ORDER, FFT_JH, HY_CT, 2 * L // FFT_N2 * FFT_J, LANES), F32),
        grid=(2 * L // tl,),
        in_specs=[pl.BlockSpec((LANES, tl), lambda i: (0, i)),
                  pl.BlockSpec((tl, 1), row),
                  pl.BlockSpec((FILTER_HIDDEN, LANES), fix),
                  pl.BlockSpec((FILTER_HIDDEN, 1), fix),
                  pl.BlockSpec((FILTER_HIDDEN, 1), fix),
                  pl.BlockSpec((FILTER_HIDDEN, 3 * FILTER_HIDDEN), fix),
                  pl.BlockSpec((FILTER_HIDDEN, 1), fix),
                  pl.BlockSpec((FILTER_HIDDEN, 1), fix),
                  pl.BlockSpec((None, 3 * FILTER_HIDDEN, HYENA_ORDER * D_HYENA), lambda i: (i // half, 0, 0)),
                  pl.BlockSpec((1, D_HYENA), fix)],
        out_specs=pl.BlockSpec((HYENA_ORDER, FFT_JH, HY_CT, tl // FFT_N2 * FFT_J, LANES), lambda i: (0, 0, 0, i, 0)),
        compiler_params=_params(1),
        name="hyena_filters",
    )(z3, t_norm, w1s.T, b1[:, None], fr1[:, None], _stack3(w2).T, b2[:, None], fr2[:, None], w3s, dl)


def _dft_mats(L):
    N = 2 * L
    n1n = N // FFT_N2
    r = n1n // 2
    assert r % FFT_TF == 0, "sequence length must be a multiple of FFT_TF * FFT_N2 / 2"
    mh = r + FFT_TF
    f1 = jnp.arange(mh, dtype=jnp.int32)[:, None]
    n1 = jnp.arange(n1n, dtype=jnp.int32)[None, :]
    keep = (f1 <= r).astype(F32)
    th = (2.0 * math.pi / n1n) * ((f1 * n1) % n1n).astype(F32)
    c1, s1 = keep * jnp.cos(th), keep * jnp.sin(th)
    fwd_full = jnp.concatenate([c1, -s1], axis=0).astype(BF16)
    fwd_half = fwd_full[:, :r]
    w = jnp.where((f1 == 0) | (f1 == r), 1.0, 2.0)
    inv_half = jnp.concatenate([(w * c1)[:, :r].T, (-w * s1)[:, :r].T], axis=1).astype(BF16)
    i2 = jnp.arange(2 * FFT_N2, dtype=jnp.int32)
    part, idx = i2 // FFT_N2, i2 % FFT_N2
    pa = (2.0 * math.pi / N) * ((jnp.arange(mh, dtype=jnp.int32)[:, None] * idx[None, :]) % N).astype(F32)
    ca, sa = jnp.cos(pa), jnp.sin(pa)
    mb = (idx[:, None] * idx[None, :] * n1n + (part[:, None] - part[None, :]) * (N // 4)) % N
    pb = (2.0 * math.pi / N) * mb.astype(F32)
    cb, sb = jnp.cos(pb), jnp.sin(pb)
    g = (ca[:, None, :] * cb[None] - sa[:, None, :] * sb[None]).astype(BF16)
    ginv = (ca[:, :, None] * cb.T[None] - sa[:, :, None] * sb.T[None]).astype(BF16)
    return dict(n1=n1n, r=r, mh=mh, fwd_full=fwd_full, fwd_half=fwd_half, inv_half=inv_half, g=g, ginv=ginv)


def _fft_a_kernel(x_ref, f_ref, a_ref, *, tb, mh):
    f = f_ref[...]
    r = x_ref.shape[1] // FFT_J
    for b in range(tb):
        for j in range(FFT_J):
            a = _dot(f, x_ref[b, pl.ds(j, r, stride=FFT_J), :].astype(BF16))
            a_ref[b, pl.ds(j, mh, stride=FFT_J), :] = _pack_pair(a[:mh], a[mh:])


def _fft_a(x5, coff, fmat, mh, tb):
    B, rj = x5.shape[0], x5.shape[3]
    blk = lambda b, j, c: (b, j, c, 0, 0)
    return pl.pallas_call(
        functools.partial(_fft_a_kernel, tb=tb, mh=mh),
        out_shape=jax.ShapeDtypeStruct((B, FFT_JH, HY_CT, mh * FFT_J, LANES), jnp.uint32),
        grid=(B // tb, FFT_JH, HY_CT),
        in_specs=[pl.BlockSpec((tb, None, None, rj, LANES), lambda b, j, c: (b, j, c + coff, 0, 0)),
                  pl.BlockSpec((2 * mh, rj // FFT_J), lambda b, j, c: (0, 0))],
        out_specs=pl.BlockSpec((tb, None, None, mh * FFT_J, LANES), blk),
        compiler_params=_params(3),
        name="fft_outer_fwd",
    )(x5, fmat)


def _fft_spec_kernel(a_ref, g_ref, kr_ref, ki_ref, *, scale):
    for f in range(FFT_TF):
        x = _dot(g_ref[f], _unpack_pair(_load_tiles(a_ref, (), f * FFT_J, 1, HY_CT))) * scale
        kr_ref[0, f] = x[:FFT_N2]
        ki_ref[0, f] = x[FFT_N2:]


def _tile_spec():
    return pl.BlockSpec((None, FFT_JH, HY_CT, FFT_TF * FFT_J, LANES), lambda f, b: (b, 0, 0, f, 0))


def _fft_spec(a, g, scale):
    B, mh = a.shape[0], a.shape[3] // FFT_J
    nat = pl.BlockSpec((1, FFT_TF, FFT_N2, D_HYENA), lambda f, b: (b, f, 0, 0))
    out = jax.ShapeDtypeStruct((B, mh, FFT_N2, D_HYENA), F32)
    return pl.pallas_call(
        functools.partial(_fft_spec_kernel, scale=scale),
        out_shape=[out, out],
        grid=(mh // FFT_TF, B),
        in_specs=[_tile_spec(), pl.BlockSpec((FFT_TF, 2 * FFT_N2, 2 * FFT_N2), lambda f, b: (f, 0, 0))],
        out_specs=[nat, nat],
        compiler_params=_params(2),
        name="fft_inner_fwd",
    )(a, g)


def _fft_b_kernel(a_ref, g_ref, gi_ref, kr_ref, ki_ref, y_ref):
    for f in range(FFT_TF):
        x = _dot(g_ref[f], _unpack_pair(_load_tiles(a_ref, (), f * FFT_J, 1, HY_CT)))
        xr, xi = x[:FFT_N2], x[FFT_N2:]
        kr, ki = kr_ref[0, f], ki_ref[0, f]
        z = jnp.concatenate([xr * kr - xi * ki, xr * ki + xi * kr], axis=0).astype(BF16)
        y = _dot(gi_ref[f], z)
        _store_tiles(y_ref, (), f * FFT_J, _pack_pair(y[:FFT_N2], y[FFT_N2:]))


def _fft_b(a, g, ginv, kr, ki, o):
    B, mh = a.shape[0], a.shape[3] // FFT_J
    mat = pl.BlockSpec((FFT_TF, 2 * FFT_N2, 2 * FFT_N2), lambda f, b: (f, 0, 0))
    spec = pl.BlockSpec((1, FFT_TF, FFT_N2, D_HYENA), lambda f, b: (o, f, 0, 0))
    return pl.pallas_call(
        _fft_b_kernel,
        out_shape=jax.ShapeDtypeStruct(a.shape, jnp.uint32),
        grid=(mh // FFT_TF, B),
        in_specs=[_tile_spec(), mat, mat, spec, spec],
        out_specs=_tile_spec(),
        compiler_params=_params(2),
        name="fft_inner",
    )(a, g, ginv, kr, ki)


def _fft_c_kernel(y_ref, c_ref, u_ref, gate_ref, bias_ref, o_ref, *, tb):
    cm = c_ref[...]
    bias = bias_ref[...]
    mh, r = y_ref.shape[1] // FFT_J, u_ref.shape[1] // FFT_J
    for b in range(tb):
        for j in range(FFT_J):
            y = _dot(cm, _unpack_pair(y_ref[b, pl.ds(j, mh, stride=FFT_J), :]))
            rows = pl.ds(j, r, stride=FFT_J)
            o_ref[b, rows, :] = gate_ref[b, rows, :] * (y + bias * u_ref[b, rows, :])


def _fft_c(y, cmat, u5, uoff, gate5, goff, bias, tb):
    B, mhj = y.shape[0], y.shape[3]
    rj = cmat.shape[0] * FFT_J
    blk = lambda b, j, c: (b, j, c, 0, 0)
    return pl.pallas_call(
        functools.partial(_fft_c_kernel, tb=tb),
        out_shape=jax.ShapeDtypeStruct((B, FFT_JH, HY_CT, rj, LANES), F32),
        grid=(B // tb, FFT_JH, HY_CT),
        in_specs=[pl.BlockSpec((tb, None, None, mhj, LANES), blk),
                  pl.BlockSpec(cmat.shape, lambda b, j, c: (0, 0)),
                  pl.BlockSpec((tb, None, None, rj, LANES), lambda b, j, c: (b, j, c + uoff, 0, 0)),
                  pl.BlockSpec((tb, None, None, rj, LANES), lambda b, j, c: (b, j, c + goff, 0, 0)),
                  pl.BlockSpec((1, LANES), lambda b, j, c: (0, c))],
        out_specs=pl.BlockSpec((tb, None, None, rj, LANES), blk),
        compiler_params=_params(3),
        name="fft_outer_inv",
    )(y, cmat, u5, gate5, bias)


def _filter_spectra(kern, mats, L):
    return _fft_spec(_fft_a(kern, 0, mats["fwd_full"], mats["mh"], 1), mats["g"], 1.0 / (2 * L))


def _hyena(hy5, kr, ki, hy_bias, mats, tb):
    z5 = hy5
    for o in range(HYENA_ORDER):
        a = _fft_a(z5, 0, mats["fwd_half"], mats["mh"], tb)
        y = _fft_b(a, mats["g"], mats["ginv"], kr, ki, o)
        z5 = _fft_c(y, mats["inv_half"], z5, 0, hy5, (o + 1) * HY_CT, hy_bias[o][None, :], tb)
    return z5


def _out_ffn_in_kernel(x_ref, cab_ref, c_ref, gc_ref, wab_ref, wc_ref, g2_ref, wi_ref, xn_ref, u_ref):
    c = _load_tiles(c_ref, (), 0, c_ref.shape[2] // FFT_J, HY_CT)
    cn = _rms(c, gc_ref[...]).astype(BF16)
    xn = x_ref[...] + _dot(cab_ref[...], wab_ref[...]) + _dot(cn, wc_ref[...])
    xn_ref[...] = xn
    h = _rms(xn, g2_ref[...]).astype(BF16)
    for j in range(0, 2 * D_FF, FFN_IN_CHUNK):
        u_ref[:, j:j + FFN_IN_CHUNK] = _dot(h, wi_ref[:, j:j + FFN_IN_CHUNK])


def _out_ffn_in(x2, cab, c5, g_c, w_ab, w_c, g2, w_in):
    T = x2.shape[0]
    tm = ROW_TILE
    tps = c5.shape[3] // FFT_J * FFT_N2 // tm
    row = lambda i: (i, 0)
    fix = lambda i: (0, 0)
    return pl.pallas_call(
        _out_ffn_in_kernel,
        out_shape=[jax.ShapeDtypeStruct((T, D_MODEL), F32),
                   jax.ShapeDtypeStruct((T, 2 * D_FF), F32)],
        grid=(T // tm,),
        in_specs=[pl.BlockSpec((tm, D_MODEL), row),
                  pl.BlockSpec((tm, D_AB), row),
                  pl.BlockSpec((None, FFT_JH, HY_CT, tm // FFT_N2 * FFT_J, LANES),
                               lambda i: (i // tps, 0, 0, i % tps, 0)),
                  pl.BlockSpec((1, D_HYENA), fix),
                  pl.BlockSpec((D_AB, D_MODEL), fix),
                  pl.BlockSpec((D_HYENA, D_MODEL), fix),
                  pl.BlockSpec((1, D_MODEL), fix),
                  pl.BlockSpec((D_MODEL, 2 * D_FF), fix, pipeline_mode=pl.Buffered(1))],
        out_specs=[pl.BlockSpec((tm, D_MODEL), row),
                   pl.BlockSpec((tm, 2 * D_FF), row)],
        compiler_params=_params(1),
        name="out_ffn_in",
    )(x2, cab, c5, g_c, w_ab, w_c, g2, w_in)


def _ffn_out_kernel(u_ref, up_ref, un_ref, xn_ref, cw_ref, cb_ref, wo_ref, o_ref, acc_ref, *, tiles_per_seq):
    i = pl.program_id(0)
    is_first = (i % tiles_per_seq) == 0
    is_last = (i % tiles_per_seq) == tiles_per_seq - 1
    tm = u_ref.shape[0]
    row = lax.broadcasted_iota(jnp.int32, (tm, 1), 0)

    def conv(c0):
        cs = slice(c0, c0 + FFN_CHUNK)
        uc = u_ref[:, cs]
        before = jnp.where(is_first, 0.0, up_ref[SUBLANES - 1:SUBLANES, cs])
        after = jnp.where(is_last, 0.0, un_ref[0:1, cs])
        um = jnp.where(row == 0, before, pltpu.roll(uc, 1, axis=0))
        up = jnp.where(row == tm - 1, after, pltpu.roll(uc, tm - 1, axis=0))
        return um * cw_ref[0:1, cs] + uc * cw_ref[1:2, cs] + up * cw_ref[2:3, cs] + cb_ref[:, cs]

    for j in range(D_FF // FFN_CHUNK):
        gate = conv(j * FFN_CHUNK)
        half_val = conv(D_FF + j * FFN_CHUNK)
        act = gate * (1.0 + lax.erf(gate * math.sqrt(0.5))) * half_val
        part = _dot(act.astype(BF16), wo_ref[j * FFN_CHUNK:(j + 1) * FFN_CHUNK, :])
        if j == 0:
            acc_ref[...] = part
        else:
            acc_ref[...] += part
    o_ref[...] = xn_ref[...] + acc_ref[...]


def _ffn_out(u, xn, cw, cb, w_out, L):
    T = xn.shape[0]
    tm = ROW_TILE
    hpt = tm // SUBLANES
    n8 = T // SUBLANES
    row = lambda i: (i, 0)
    fix = lambda i: (0, 0)
    return pl.pallas_call(
        functools.partial(_ffn_out_kernel, tiles_per_seq=L // tm),
        out_shape=jax.ShapeDtypeStruct((T, D_MODEL), F32),
        grid=(T // tm,),
        in_specs=[pl.BlockSpec((tm, 2 * D_FF), row),
                  pl.BlockSpec((SUBLANES, 2 * D_FF), lambda i: (jnp.maximum(i * hpt - 1, 0), 0)),
                  pl.BlockSpec((SUBLANES, 2 * D_FF), lambda i: (jnp.minimum((i + 1) * hpt, n8 - 1), 0)),
                  pl.BlockSpec((tm, D_MODEL), row),
                  pl.BlockSpec((3, 2 * D_FF), fix),
                  pl.BlockSpec((1, 2 * D_FF), fix),
                  pl.BlockSpec((D_FF, D_MODEL), fix, pipeline_mode=pl.Buffered(1))],
        out_specs=pl.BlockSpec((tm, D_MODEL), row),
        scratch_shapes=[pltpu.VMEM((tm, D_MODEL), F32)],
        compiler_params=_params(1),
        name="ffn_out",
    )(u, u, u, xn, cw, cb, w_out)


def _head_ones(n_heads):
    return jnp.kron(jnp.eye(n_heads, dtype=F32), jnp.ones((HEAD_DIM, HEAD_DIM), F32)).astype(BF16)


def _layer(x2, B, L, mats, spectra, lw):
    q, kv, ph = _in_proj(x2, lw["norm1_g"], lw["wq_t"], lw["w_in"], lw["ek"], lw["qg"], lw["kg"])
    cab, hy = _mix(q, kv, ph, lw["sink"], lw["pw_bd"], lw["pool_scale"], lw["hy_w"], lw["hy_b"],
                   lw["g_a"], lw["g_b"], B, L)
    tb = math.gcd(B, 8)
    c5 = _hyena(hy, spectra[0], spectra[1], lw["hy_bias"], mats, tb)
    xn, u = _out_ffn_in(x2, cab, c5, lw["g_c"], lw["w_ab"], lw["w_c"],
                        lw["norm2_g"], lw["w_ffn_in"])
    return _ffn_out(u, xn, lw["ffn_cw"], lw["ffn_cb"], lw["w_ffn_out"], L)


def kernel(x_prompt, x_sample, norm1_g, w_in, q_norm_g, k_norm_g, attn_sink, pool_w, pool_scale, hy_conv_w,
           hy_conv_b, filt_w1, filt_b1, filt_freq1, filt_w2, filt_b2, filt_freq2, filt_w3, hy_bias, out_norm_g,
           w_out, norm2_g, w_ffn_in, ffn_conv_w, ffn_conv_b, w_ffn_out):
    groups = [x_prompt, x_sample]
    lens = sorted({g.shape[1] for g in groups})
    mats = {L: _dft_mats(L) for L in lens}
    ek = _head_ones(N_KV_HEADS)
    xs = [g.reshape(-1, D_MODEL) for g in groups]
    glu_scale = jnp.concatenate([jnp.ones((D_FF,), F32), jnp.full((D_FF,), 0.5, F32)])[None, :]
    for l in range(DEPTH):
        lw = dict(
            norm1_g=norm1_g[l][None, :], wq_t=w_in[l, :, :D_ATTN].T.astype(BF16), w_in=w_in[l, :, D_ATTN:].astype(BF16),
            ek=ek, kg=jnp.tile(k_norm_g[l], N_KV_HEADS)[None, :],
            qg=(jnp.tile(q_norm_g[l], N_Q_HEADS) * (1.0 / math.sqrt(HEAD_DIM)))[:, None],
            sink=attn_sink[l],
            pw_bd=jax.scipy.linalg.block_diag(*[pool_w[l, g] for g in range(len(POOL_WINDOWS))]).astype(BF16),
            pool_scale=pool_scale[l][None, :], hy_w=hy_conv_w[l], hy_b=hy_conv_b[l][None, :],
            g_a=out_norm_g[l, :D_ATTN][None, :], g_b=out_norm_g[l, D_ATTN:D_AB][None, :],
            g_c=out_norm_g[l, D_AB:][None, :], hy_bias=hy_bias[l],
            w_ab=w_out[l, :D_AB].astype(BF16), w_c=w_out[l, D_AB:].astype(BF16),
            norm2_g=norm2_g[l][None, :], w_ffn_in=w_ffn_in[l].astype(BF16),
            ffn_cw=ffn_conv_w[l] * glu_scale, ffn_cb=(ffn_conv_b[l] * glu_scale[0])[None, :],
            w_ffn_out=w_ffn_out[l].astype(BF16),
        )
        spectra = {}
        for L in lens:
            kern = _filter_kernels(L, filt_w1[l], filt_b1[l], filt_freq1[l], filt_w2[l], filt_b2[l], filt_freq2[l],
                                   filt_w3[l])
            spectra[L] = _filter_spectra(kern, mats[L], L)
        xs = [_layer(x2, g.shape[0], g.shape[1], mats[g.shape[1]], spectra[g.shape[1]], lw)
              for x2, g in zip(xs, groups)]
    return tuple(x2.reshape(g.shape) for x2, g in zip(xs, groups))
```

```python
import functools
import math

import jax
import jax.numpy as jnp
from jax import lax
from jax.experimental import pallas as pl
from jax.experimental.pallas import tpu as pltpu

F32 = jnp.float32
BF16 = jnp.bfloat16

D_MODEL = 1024
DEPTH = 2
HEAD_DIM = 64
N_Q_HEADS = 8
N_KV_HEADS = 2
GQA_GROUP = N_Q_HEADS // N_KV_HEADS
D_ATTN = N_Q_HEADS * HEAD_DIM
D_KV = N_KV_HEADS * HEAD_DIM
WINDOW = 128
BLOCK = 128
POOL_WINDOWS = (2, 4, 8, 16)
D_POOL = 256
POOL_GROUP_DIM = D_POOL // len(POOL_WINDOWS)
D_HYENA = 256
HYENA_ORDER = 2
FILTER_BANDS = 16
FILTER_EMB = 1 + 2 * FILTER_BANDS
FILTER_HIDDEN = 64
N_FILTERS = 2 * HYENA_ORDER
DECAY_FAST_PCT = 0.3
DECAY_SLOW_PCT = 1.5
DECAY_TARGET = 1e-2
D_HY_IN = (HYENA_ORDER + 1) * D_HYENA
D_PH = D_POOL + D_HY_IN
D_IN_PROJ = D_ATTN + 2 * D_KV + D_PH
D_CAT = D_ATTN + D_POOL + D_HYENA
D_AB = D_ATTN + D_POOL
D_FF = 2816
EPS = 1e-6
NEG_INF = -1e30

LANES = 128
SUBLANES = 8
FFT_N2 = 128
FFT_J = SUBLANES
FFT_TF = 8
FFT_JH = FFT_N2 // FFT_J
HY_CT = D_HYENA // LANES
FFN_CHUNK = 256
FFN_IN_CHUNK = 512
ROW_TILE = 512
MIX_NB = 2
VMEM_LIMIT = 56 * 1024 * 1024


def _params(n_axes):
    return pltpu.CompilerParams(dimension_semantics=("arbitrary",) * n_axes, vmem_limit_bytes=VMEM_LIMIT)


def _rms(x, g):
    return x * lax.rsqrt(jnp.mean(x * x, axis=-1, keepdims=True) + EPS) * g


def _dot(a, b):
    return jnp.dot(a, b, preferred_element_type=F32)


def _store_tiles(ref, lead, row0, val):
    for nl in range(val.shape[0] // FFT_N2):
        for jh in range(FFT_JH):
            for ct in range(val.shape[1] // LANES):
                r0 = nl * FFT_N2 + jh * FFT_J
                ref[lead + (jh, ct, pl.ds(row0 + nl * FFT_J, FFT_J), slice(None))] = (
                    val[r0:r0 + FFT_J, ct * LANES:(ct + 1) * LANES])


def _load_tiles(ref, lead, row0, n_local, n_ct):
    return jnp.concatenate(
        [jnp.concatenate([ref[lead + (jh, ct, pl.ds(row0 + nl * FFT_J, FFT_J), slice(None))]
                          for nl in range(n_local) for jh in range(FFT_JH)], axis=0)
         for ct in range(n_ct)], axis=1)


def _pack_pair(re, im):
    hi = lax.bitcast_convert_type(re.astype(BF16).astype(F32), jnp.uint32)
    lo = lax.bitcast_convert_type(im.astype(BF16).astype(F32), jnp.uint32)
    return hi | (lo >> 16)


def _unpack_pair(w):
    re = lax.bitcast_convert_type(w & jnp.uint32(0xFFFF0000), F32)
    im = lax.bitcast_convert_type(w << 16, F32)
    return jnp.concatenate([re, im], axis=0).astype(BF16)


def _in_proj_kernel(x_ref, g_ref, wq_ref, w_ref, ek_ref, qg_ref, kg_ref, q_ref, kv_ref, ph_ref):
    h = _rms(x_ref[...], g_ref[...]).astype(BF16)
    tm = h.shape[0]

    qt = lax.dot_general(wq_ref[...], h, (((1,), (1,)), ((), ())), preferred_element_type=F32)
    qt = qt.reshape(N_Q_HEADS, HEAD_DIM, tm)
    qt = qt * lax.rsqrt(jnp.mean(qt * qt, axis=1, keepdims=True) + EPS)
    qt = (qt.reshape(D_ATTN, tm) * qg_ref[...]).astype(BF16)
    for j in range(tm // BLOCK):
        q_ref[j] = qt[:, j * BLOCK:(j + 1) * BLOCK]

    p = _dot(h, w_ref[...])
    k = p[:, :D_KV]
    ss = _dot((k * k).astype(BF16), ek_ref[...])
    kv_ref[:, :D_KV] = (k * lax.rsqrt(ss * (1.0 / HEAD_DIM) + EPS) * kg_ref[...]).astype(BF16)
    kv_ref[:, D_KV:] = p[:, D_KV:2 * D_KV].astype(BF16)
    ph_ref[...] = p[:, 2 * D_KV:]


def _in_proj(x2, g, wq_t, w, ek, qg, kg):
    T = x2.shape[0]
    tm = ROW_TILE
    row = lambda i: (i, 0)
    fix = lambda i: (0, 0)
    return pl.pallas_call(
        _in_proj_kernel,
        out_shape=[jax.ShapeDtypeStruct((T // BLOCK, D_ATTN, BLOCK), BF16),
                   jax.ShapeDtypeStruct((T, 2 * D_KV), BF16),
                   jax.ShapeDtypeStruct((T, D_PH), F32)],
        grid=(T // tm,),
        in_specs=[pl.BlockSpec((tm, D_MODEL), row),
                  pl.BlockSpec((1, D_MODEL), fix),
                  pl.BlockSpec((D_ATTN, D_MODEL), fix),
                  pl.BlockSpec((D_MODEL, 2 * D_KV + D_PH), fix),
                  pl.BlockSpec((D_KV, D_KV), fix),
                  pl.BlockSpec((D_ATTN, 1), fix),
                  pl.BlockSpec((1, D_KV), fix)],
        out_specs=[pl.BlockSpec((tm // BLOCK, D_ATTN, BLOCK), lambda i: (i, 0, 0)),
                   pl.BlockSpec((tm, 2 * D_KV), row),
                   pl.BlockSpec((tm, D_PH), row)],
        compiler_params=_params(1),
        name="in_proj",
    )(x2, g, wq_t, w, ek, qg, kg)


def _mix_kernel(sink_ref, *refs, seq_len):
    bias_refs = refs[:MIX_NB]
    (q_ref, kvp_ref, kvc_ref, kvn_ref, phc_ref, php_ref, phn_ref, pw_ref, ps_ref, hw_ref, hb_ref, ga_ref, gb_ref,
     cat_ref, hy_ref, ext_ref, att_ref, s_ref, p_ref, rden_ref) = refs[MIX_NB:]
    l = pl.program_id(1)
    is_first = l == 0
    is_last = l == pl.num_programs(1) - 1
    rows = MIX_NB * BLOCK

    tn = (((0,), (0,)), ((), ()))

    def keys(sb, cols):
        def blk(i):
            if i < 0:
                return kvp_ref[:, cols]
            if i >= MIX_NB:
                return kvn_ref[:, cols]
            return kvc_ref[i * BLOCK:(i + 1) * BLOCK, cols]
        return jnp.concatenate([blk(sb - 1), blk(sb), blk(sb + 1)], axis=0)

    pairs = [(sb, g) for sb in range(MIX_NB) for g in range(N_KV_HEADS)]
    gw = GQA_GROUP * BLOCK
    for i, (sb, g) in enumerate(pairs):
        qg = jnp.concatenate([q_ref[sb, h * HEAD_DIM:(h + 1) * HEAD_DIM, :]
                              for h in range(g * GQA_GROUP, (g + 1) * GQA_GROUP)], axis=1)
        st = _dot(keys(sb, slice(g * HEAD_DIM, (g + 1) * HEAD_DIM)), qg)
        for j in range(GQA_GROUP):
            s_ref[i, :, j * BLOCK:(j + 1) * BLOCK] = st[:, j * BLOCK:(j + 1) * BLOCK] + bias_refs[sb][g * GQA_GROUP + j]
    for i, (sb, g) in enumerate(pairs):
        for j in range(GQA_GROUP):
            cols = slice(j * BLOCK, (j + 1) * BLOCK)
            st = s_ref[i, :, cols]
            sk = sink_ref[g * GQA_GROUP + j]
            m = jnp.maximum(jnp.max(st, axis=0, keepdims=True), sk)
            e = jnp.exp(st - m)
            rden_ref[i, :, cols] = 1.0 / (jnp.sum(e, axis=0, keepdims=True) + jnp.exp(sk - m))
            p_ref[i, :, cols] = e.astype(BF16)
    for i, (sb, g) in enumerate(pairs):
        vh = keys(sb, slice(D_KV + g * HEAD_DIM, D_KV + (g + 1) * HEAD_DIM))
        ot = lax.dot_general(vh, p_ref[i], tn, preferred_element_type=F32) * rden_ref[i]
        for j in range(GQA_GROUP):
            h = g * GQA_GROUP + j
            att_ref[sb, h * HEAD_DIM:(h + 1) * HEAD_DIM, :] = ot[:, j * BLOCK:(j + 1) * BLOCK]
    for sb in range(MIX_NB):
        cat_ref[sb * BLOCK:(sb + 1) * BLOCK, :D_ATTN] = _rms(att_ref[sb].T, ga_ref[...]).astype(BF16)

    ext_ref[0:SUBLANES, :] = jnp.where(is_first, 0.0, php_ref[...])
    ext_ref[SUBLANES:SUBLANES + rows, :] = phc_ref[...]
    ext_ref[SUBLANES + rows:, :] = jnp.where(is_last, 0.0, phn_ref[...])

    t = l * rows + lax.broadcasted_iota(jnp.int32, (rows, 1), 0)
    lane = lax.broadcasted_iota(jnp.int32, (1, LANES), 1)

    def window_mean(w, col0):
        lo = jnp.clip(t - w // 2, 0, seq_len)
        hi = jnp.clip(t - w // 2 + w, 0, seq_len)
        tot = ext_ref[SUBLANES - w // 2:SUBLANES - w // 2 + rows, col0:col0 + LANES]
        for d in range(1 - w // 2, w // 2):
            tot = tot + ext_ref[SUBLANES + d:SUBLANES + d + rows, col0:col0 + LANES]
        return tot / (hi - lo).astype(F32)

    gpt = LANES // POOL_GROUP_DIM
    means = [jnp.where(lane < POOL_GROUP_DIM, window_mean(POOL_WINDOWS[gpt * i], i * LANES),
                       window_mean(POOL_WINDOWS[gpt * i + 1], i * LANES)) for i in range(D_POOL // LANES)]
    u = ext_ref[SUBLANES:SUBLANES + rows, 0:D_POOL]
    pooled = _dot((jnp.concatenate(means, axis=1) - u).astype(BF16), pw_ref[...]) * ps_ref[...]
    cat_ref[:, D_ATTN:] = _rms(pooled, gb_ref[...]).astype(BF16)

    hp = ext_ref[SUBLANES - 1:SUBLANES - 1 + rows, D_POOL:]
    hc = ext_ref[SUBLANES:SUBLANES + rows, D_POOL:]
    hn = ext_ref[SUBLANES + 1:SUBLANES + 1 + rows, D_POOL:]
    _store_tiles(hy_ref, (), 0, hp * hw_ref[0:1, :] + hc * hw_ref[1:2, :] + hn * hw_ref[2:3, :] + hb_ref[...])


def _attn_bias():
    row = jnp.arange(BLOCK, dtype=jnp.int32)[None, :]
    col = jnp.arange(3 * BLOCK, dtype=jnp.int32)[:, None]
    dist = jnp.abs(row + BLOCK - col)
    slopes = 2.0 ** (-8.0 * (jnp.arange(N_Q_HEADS, dtype=F32) + 1.0) / N_Q_HEADS)
    bias = -slopes[:, None, None] * dist.astype(F32)[None]
    out = []
    for case in range(4):
        valid = dist <= WINDOW
        if case & 1:
            valid = valid & (col >= BLOCK)
        if case & 2:
            valid = valid & (col < 2 * BLOCK)
        out.append(jnp.where(valid[None], bias, NEG_INF))
    return jnp.stack(out)


def _mix(q, kv, ph, sink, pw_bd, pool_scale, hy_w, hy_b, g_a, g_b, B, L):
    T = B * L
    rows = MIX_NB * BLOCK
    nb = L // BLOCK
    ns = L // rows
    rps = rows // SUBLANES
    n8 = T // SUBLANES
    cur = lambda b, l: (b * ns + l, 0)
    prev = lambda b, l: (b * nb + jnp.maximum(l * MIX_NB - 1, 0), 0)
    nxt = lambda b, l: (b * nb + jnp.minimum((l + 1) * MIX_NB, nb - 1), 0)
    prev8 = lambda b, l: (jnp.maximum((b * ns + l) * rps - 1, 0), 0)
    next8 = lambda b, l: (jnp.minimum((b * ns + l + 1) * rps, n8 - 1), 0)
    fix = lambda b, l: (0, 0)

    def bias_spec(sb):
        def index(b, l):
            case = jnp.int32(0)
            if sb == 0:
                case = case + (l == 0).astype(jnp.int32)
            if sb == MIX_NB - 1:
                case = case + 2 * (l == ns - 1).astype(jnp.int32)
            return (case, 0, 0, 0)
        return pl.BlockSpec((None, N_Q_HEADS, 3 * BLOCK, BLOCK), index)

    bias = _attn_bias()
    n_pairs = MIX_NB * N_KV_HEADS
    gw = GQA_GROUP * BLOCK
    return pl.pallas_call(
        functools.partial(_mix_kernel, seq_len=L),
        out_shape=[jax.ShapeDtypeStruct((T, D_AB), BF16),
                   jax.ShapeDtypeStruct((B, FFT_JH, D_HY_IN // LANES, nb * FFT_J, LANES), F32)],
        grid=(B, ns),
        in_specs=[pl.BlockSpec(memory_space=pltpu.SMEM)] + [bias_spec(sb) for sb in range(MIX_NB)] + [
                  pl.BlockSpec((MIX_NB, D_ATTN, BLOCK), lambda b, l: (b * ns + l, 0, 0)),
                  pl.BlockSpec((BLOCK, 2 * D_KV), prev),
                  pl.BlockSpec((rows, 2 * D_KV), cur),
                  pl.BlockSpec((BLOCK, 2 * D_KV), nxt),
                  pl.BlockSpec((rows, D_PH), cur),
                  pl.BlockSpec((SUBLANES, D_PH), prev8),
                  pl.BlockSpec((SUBLANES, D_PH), next8),
                  pl.BlockSpec((D_POOL, D_POOL), fix),
                  pl.BlockSpec((1, D_POOL), fix),
                  pl.BlockSpec((3, D_HY_IN), fix),
                  pl.BlockSpec((1, D_HY_IN), fix),
                  pl.BlockSpec((1, D_ATTN), fix),
                  pl.BlockSpec((1, D_POOL), fix)],
        out_specs=[pl.BlockSpec((rows, D_AB), cur),
                   pl.BlockSpec((None, FFT_JH, D_HY_IN // LANES, MIX_NB * FFT_J, LANES),
                                lambda b, l: (b, 0, 0, l, 0))],
        scratch_shapes=[pltpu.VMEM((rows + 2 * SUBLANES, D_PH), F32),
                        pltpu.VMEM((MIX_NB, D_ATTN, BLOCK), F32),
                        pltpu.VMEM((n_pairs, 3 * BLOCK, gw), F32),
                        pltpu.VMEM((n_pairs, 3 * BLOCK, gw), BF16),
                        pltpu.VMEM((n_pairs, 1, gw), F32)],
        compiler_params=_params(2),
        name="mix",
    )(sink, *([bias] * MIX_NB), q, kv, kv, kv, ph, ph, ph, pw_bd, pool_scale, hy_w, hy_b, g_a, g_b)


def _split3(x, axis):
    hi = x.astype(BF16)
    lo = (x - hi.astype(F32)).astype(BF16)
    return jnp.concatenate([hi, lo, hi], axis=axis)


def _stack3(w):
    hi = w.astype(BF16)
    lo = (w - hi.astype(F32)).astype(BF16)
    return jnp.concatenate([hi, hi, lo], axis=0)


def _filter_kernel(z_ref, t_ref, w1_ref, b1_ref, f1_ref, w2_ref, b2_ref, f2_ref, w3_ref, dl_ref, o_ref, *, seq_len):
    h = jnp.sin(f1_ref[...] * (_dot(w1_ref[...], z_ref[...]) + b1_ref[...]))
    h = jnp.sin(f2_ref[...] * (_dot(w2_ref[...], _split3(h, 0)) + b2_ref[...]))
    h = lax.dot_general(_split3(h, 0), w3_ref[...], (((0,), (0,)), ((), ())),
                        preferred_element_type=F32)
    n = pl.program_id(0) * h.shape[0] + lax.broadcasted_iota(jnp.int32, (h.shape[0], 1), 0)
    decay = jnp.where(n == seq_len, 0.0, jnp.exp(-t_ref[...] * dl_ref[...]))
    for o in range(HYENA_ORDER):
        _store_tiles(o_ref, (o,), 0, h[:, o * D_HYENA:(o + 1) * D_HYENA] * decay)


def _filter_kernels(L, w1, b1, fr1, w2, b2, fr2, w3):
    pos = jnp.arange(2 * L, dtype=jnp.int32)
    tpos = jnp.where(pos < L, pos, (2 * L - pos) % L).astype(F32)[:, None]
    t_norm = tpos * (1.0 / (L - 1))
    bands = jnp.linspace(1e-4, FILTER_BANDS - 1, FILTER_BANDS, dtype=F32)[None, :]
    ang = 2.0 * math.pi * tpos * bands / L
    z = jnp.concatenate([t_norm, jnp.cos(ang), -jnp.sin(ang)], axis=-1)
    z3 = jnp.pad(_split3(z, 1), ((0, 0), (0, LANES - 3 * FILTER_EMB))).T
    w1s = jnp.pad(_stack3(w1), ((0, LANES - 3 * FILTER_EMB), (0, 0)))
    w3d = w3.reshape(FILTER_HIDDEN, HYENA_ORDER, 2, D_HYENA).transpose(2, 0, 1, 3).reshape(
        2, FILTER_HIDDEN, HYENA_ORDER * D_HYENA)
    w3s = jnp.stack([_stack3(w3d[0]), _stack3(w3d[1])])
    max_decay = math.log(DECAY_TARGET) / DECAY_FAST_PCT
    min_decay = math.log(DECAY_TARGET) / DECAY_SLOW_PCT
    dl = jnp.abs(jnp.linspace(min_decay, max_decay, D_HYENA, dtype=F32))[None, :]
    tl = 512
    half = L // tl
    row = lambda i: (i, 0)
    fix = lambda i: (0, 0)
    return pl.pallas_call(
        functools.partial(_filter_kernel, seq_len=L),
        out_shape=jax.ShapeDtypeStruct((HYENA_ORDER, FFT_JH, HY_CT, 2 * L // FFT_N2 * FFT_J, LANES), F32),
        grid=(2 * L // tl,),
        in_specs=[pl.BlockSpec((LANES, tl), lambda i: (0, i)),
                  pl.BlockSpec((tl, 1), row),
                  pl.BlockSpec((FILTER_HIDDEN, LANES), fix),
                  pl.BlockSpec((FILTER_HIDDEN, 1), fix),
                  pl.BlockSpec((FILTER_HIDDEN, 1), fix),
                  pl.BlockSpec((FILTER_HIDDEN, 3 * FILTER_HIDDEN), fix),
                  pl.BlockSpec((FILTER_HIDDEN, 1), fix),
                  pl.BlockSpec((FILTER_HIDDEN, 1), fix),
                  pl.BlockSpec((None, 3 * FILTER_HIDDEN, HYENA_ORDER * D_HYENA), lambda i: (i // half, 0, 0)),
                  pl.BlockSpec((1, D_HYENA), fix)],
        out_specs=pl.BlockSpec((HYENA_ORDER, FFT_JH, HY_CT, tl // FFT_N2 * FFT_J, LANES), lambda i: (0, 0, 0, i, 0)),
        compiler_params=_params(1),
        name="hyena_filters",
    )(z3, t_norm, w1s.T, b1[:, None], fr1[:, None], _stack3(w2).T, b2[:, None], fr2[:, None], w3s, dl)


def _dft_mats(L):
    N = 2 * L
    n1n = N // FFT_N2
    r = n1n // 2
    assert r % FFT_TF == 0, "sequence length must be a multiple of FFT_TF * FFT_N2 / 2"
    mh = r + FFT_TF
    f1 = jnp.arange(mh, dtype=jnp.int32)[:, None]
    n1 = jnp.arange(n1n, dtype=jnp.int32)[None, :]
    keep = (f1 <= r).astype(F32)
    th = (2.0 * math.pi / n1n) * ((f1 * n1) % n1n).astype(F32)
    c1, s1 = keep * jnp.cos(th), keep * jnp.sin(th)
    fwd_full = jnp.concatenate([c1, -s1], axis=0).astype(BF16)
    fwd_half = fwd_full[:, :r]
    w = jnp.where((f1 == 0) | (f1 == r), 1.0, 2.0)
    inv_half = jnp.concatenate([(w * c1)[:, :r].T, (-w * s1)[:, :r].T], axis=1).astype(BF16)
    i2 = jnp.arange(2 * FFT_N2, dtype=jnp.int32)
    part, idx = i2 // FFT_N2, i2 % FFT_N2
    pa = (2.0 * math.pi / N) * ((jnp.arange(mh, dtype=jnp.int32)[:, None] * idx[None, :]) % N).astype(F32)
    ca, sa = jnp.cos(pa), jnp.sin(pa)
    mb = (idx[:, None] * idx[None, :] * n1n + (part[:, None] - part[None, :]) * (N // 4)) % N
    pb = (2.0 * math.pi / N) * mb.astype(F32)
    cb, sb = jnp.cos(pb), jnp.sin(pb)
    g = (ca[:, None, :] * cb[None] - sa[:, None, :] * sb[None]).astype(BF16)
    ginv = (ca[:, :, None] * cb.T[None] - sa[:, :, None] * sb.T[None]).astype(BF16)
    return dict(n1=n1n, r=r, mh=mh, fwd_full=fwd_full, fwd_half=fwd_half, inv_half=inv_half, g=g, ginv=ginv)


def _outer_blocks(tb):
    return min(4, max(1, 8 // tb))


def _fft_a_kernel(x_ref, f_ref, a_ref, *, mh):
    f = f_ref[...]
    r = x_ref.shape[3] // FFT_J
    for lead in [(b, jb, ct) for b in range(x_ref.shape[0]) for jb in range(x_ref.shape[1]) for ct in range(HY_CT)]:
        for j in range(FFT_J):
            a = _dot(f, x_ref[lead + (pl.ds(j, r, stride=FFT_J), slice(None))].astype(BF16))
            a_ref[lead + (pl.ds(j, mh, stride=FFT_J), slice(None))] = _pack_pair(a[:mh], a[mh:])


def _fft_a(x5, coff, fmat, mh, tb):
    B, rj = x5.shape[0], x5.shape[3]
    jb = _outer_blocks(tb)
    return pl.pallas_call(
        functools.partial(_fft_a_kernel, mh=mh),
        out_shape=jax.ShapeDtypeStruct((B, FFT_JH, HY_CT, mh * FFT_J, LANES), jnp.uint32),
        grid=(B // tb, FFT_JH // jb),
        in_specs=[pl.BlockSpec((tb, jb, HY_CT, rj, LANES), lambda b, j: (b, j, coff // HY_CT, 0, 0)),
                  pl.BlockSpec((2 * mh, rj // FFT_J), lambda b, j: (0, 0))],
        out_specs=pl.BlockSpec((tb, jb, HY_CT, mh * FFT_J, LANES), lambda b, j: (b, j, 0, 0, 0)),
        compiler_params=_params(2),
        name="fft_outer_fwd",
    )(x5, fmat)


def _fft_spec_kernel(a_ref, g_ref, kr_ref, ki_ref, *, scale):
    for f in range(FFT_TF):
        x = _dot(g_ref[f], _unpack_pair(_load_tiles(a_ref, (), f * FFT_J, 1, HY_CT))) * scale
        kr_ref[0, f] = x[:FFT_N2]
        ki_ref[0, f] = x[FFT_N2:]


def _tile_spec():
    return pl.BlockSpec((None, FFT_JH, HY_CT, FFT_TF * FFT_J, LANES), lambda f, b: (b, 0, 0, f, 0))


def _fft_spec(a, g, scale):
    B, mh = a.shape[0], a.shape[3] // FFT_J
    nat = pl.BlockSpec((1, FFT_TF, FFT_N2, D_HYENA), lambda f, b: (b, f, 0, 0))
    out = jax.ShapeDtypeStruct((B, mh, FFT_N2, D_HYENA), F32)
    return pl.pallas_call(
        functools.partial(_fft_spec_kernel, scale=scale),
        out_shape=[out, out],
        grid=(mh // FFT_TF, B),
        in_specs=[_tile_spec(), pl.BlockSpec((FFT_TF, 2 * FFT_N2, 2 * FFT_N2), lambda f, b: (f, 0, 0))],
        out_specs=[nat, nat],
        compiler_params=_params(2),
        name="fft_inner_fwd",
    )(a, g)


def _fft_b_kernel(a_ref, g_ref, gi_ref, kr_ref, ki_ref, y_ref):
    for f in range(FFT_TF):
        x = _dot(g_ref[f], _unpack_pair(_load_tiles(a_ref, (), f * FFT_J, 1, HY_CT)))
        xr, xi = x[:FFT_N2], x[FFT_N2:]
        kr, ki = kr_ref[0, f], ki_ref[0, f]
        z = jnp.concatenate([xr * kr - xi * ki, xr * ki + xi * kr], axis=0).astype(BF16)
        y = _dot(gi_ref[f], z)
        _store_tiles(y_ref, (), f * FFT_J, _pack_pair(y[:FFT_N2], y[FFT_N2:]))


def _fft_b(a, g, ginv, kr, ki, o):
    B, mh = a.shape[0], a.shape[3] // FFT_J
    mat = pl.BlockSpec((FFT_TF, 2 * FFT_N2, 2 * FFT_N2), lambda f, b: (f, 0, 0))
    spec = pl.BlockSpec((1, FFT_TF, FFT_N2, D_HYENA), lambda f, b: (o, f, 0, 0))
    return pl.pallas_call(
        _fft_b_kernel,
        out_shape=jax.ShapeDtypeStruct(a.shape, jnp.uint32),
        grid=(mh // FFT_TF, B),
        in_specs=[_tile_spec(), mat, mat, spec, spec],
        out_specs=_tile_spec(),
        compiler_params=_params(2),
        name="fft_inner",
    )(a, g, ginv, kr, ki)


def _fft_c_kernel(y_ref, c_ref, u_ref, gate_ref, bias_ref, o_ref):
    cm = c_ref[...]
    mh, r = y_ref.shape[3] // FFT_J, u_ref.shape[3] // FFT_J
    for lead in [(b, jb, ct) for b in range(y_ref.shape[0]) for jb in range(y_ref.shape[1]) for ct in range(HY_CT)]:
        bias = bias_ref[:, lead[2] * LANES:(lead[2] + 1) * LANES]
        for j in range(FFT_J):
            y = _dot(cm, _unpack_pair(y_ref[lead + (pl.ds(j, mh, stride=FFT_J), slice(None))]))
            rows = lead + (pl.ds(j, r, stride=FFT_J), slice(None))
            o_ref[rows] = gate_ref[rows] * (y + bias * u_ref[rows])


def _fft_c(y, cmat, u5, uoff, gate5, goff, bias, tb):
    B, mhj = y.shape[0], y.shape[3]
    rj = cmat.shape[0] * FFT_J
    jb = _outer_blocks(tb)
    blk = lambda b, j: (b, j, 0, 0, 0)
    return pl.pallas_call(
        _fft_c_kernel,
        out_shape=jax.ShapeDtypeStruct((B, FFT_JH, HY_CT, rj, LANES), F32),
        grid=(B // tb, FFT_JH // jb),
        in_specs=[pl.BlockSpec((tb, jb, HY_CT, mhj, LANES), blk),
                  pl.BlockSpec(cmat.shape, lambda b, j: (0, 0)),
                  pl.BlockSpec((tb, jb, HY_CT, rj, LANES), lambda b, j: (b, j, uoff // HY_CT, 0, 0)),
                  pl.BlockSpec((tb, jb, HY_CT, rj, LANES), lambda b, j: (b, j, goff // HY_CT, 0, 0)),
                  pl.BlockSpec((1, D_HYENA), lambda b, j: (0, 0))],
        out_specs=pl.BlockSpec((tb, jb, HY_CT, rj, LANES), blk),
        compiler_params=_params(2),
        name="fft_outer_inv",
    )(y, cmat, u5, gate5, bias)


def _filter_spectra(kern, mats, L):
    return _fft_spec(_fft_a(kern, 0, mats["fwd_full"], mats["mh"], 1), mats["g"], 1.0 / (2 * L))


def _hyena(hy5, kr, ki, hy_bias, mats, tb):
    z5 = hy5
    for o in range(HYENA_ORDER):
        a = _fft_a(z5, 0, mats["fwd_half"], mats["mh"], tb)
        y = _fft_b(a, mats["g"], mats["ginv"], kr, ki, o)
        z5 = _fft_c(y, mats["inv_half"], z5, 0, hy5, (o + 1) * HY_CT, hy_bias[o][None, :], tb)
    return z5


def _out_ffn_in_kernel(x_ref, cab_ref, c_ref, gc_ref, wab_ref, wc_ref, g2_ref, wi_ref, xn_ref, u_ref):
    c = _load_tiles(c_ref, (), 0, c_ref.shape[2] // FFT_J, HY_CT)
    cn = _rms(c, gc_ref[...]).astype(BF16)
    xn = x_ref[...] + _dot(cab_ref[...], wab_ref[...]) + _dot(cn, wc_ref[...])
    xn_ref[...] = xn
    h = _rms(xn, g2_ref[...]).astype(BF16)
    for j in range(0, 2 * D_FF, FFN_IN_CHUNK):
        u_ref[:, j:j + FFN_IN_CHUNK] = _dot(h, wi_ref[:, j:j + FFN_IN_CHUNK])


def _out_ffn_in(x2, cab, c5, g_c, w_ab, w_c, g2, w_in):
    T = x2.shape[0]
    tm = ROW_TILE
    tps = c5.shape[3] // FFT_J * FFT_N2 // tm
    row = lambda i: (i, 0)
    fix = lambda i: (0, 0)
    return pl.pallas_call(
        _out_ffn_in_kernel,
        out_shape=[jax.ShapeDtypeStruct((T, D_MODEL), F32),
                   jax.ShapeDtypeStruct((T, 2 * D_FF), F32)],
        grid=(T // tm,),
        in_specs=[pl.BlockSpec((tm, D_MODEL), row),
                  pl.BlockSpec((tm, D_AB), row),
                  pl.BlockSpec((None, FFT_JH, HY_CT, tm // FFT_N2 * FFT_J, LANES),
                               lambda i: (i // tps, 0, 0, i % tps, 0)),
                  pl.BlockSpec((1, D_HYENA), fix),
                  pl.BlockSpec((D_AB, D_MODEL), fix),
                  pl.BlockSpec((D_HYENA, D_MODEL), fix),
                  pl.BlockSpec((1, D_MODEL), fix),
                  pl.BlockSpec((D_MODEL, 2 * D_FF), fix, pipeline_mode=pl.Buffered(1))],
        out_specs=[pl.BlockSpec((tm, D_MODEL), row),
                   pl.BlockSpec((tm, 2 * D_FF), row)],
        compiler_params=_params(1),
        name="out_ffn_in",
    )(x2, cab, c5, g_c, w_ab, w_c, g2, w_in)


def _ffn_out_kernel(u_ref, up_ref, un_ref, xn_ref, cw_ref, cb_ref, wo_ref, o_ref, acc_ref, *, tiles_per_seq):
    i = pl.program_id(0)
    is_first = (i % tiles_per_seq) == 0
    is_last = (i % tiles_per_seq) == tiles_per_seq - 1
    tm = u_ref.shape[0]
    row = lax.broadcasted_iota(jnp.int32, (tm, 1), 0)

    def conv(c0):
        cs = slice(c0, c0 + FFN_CHUNK)
        uc = u_ref[:, cs]
        before = jnp.where(is_first, 0.0, up_ref[SUBLANES - 1:SUBLANES, cs])
        after = jnp.where(is_last, 0.0, un_ref[0:1, cs])
        um = jnp.where(row == 0, before, pltpu.roll(uc, 1, axis=0))
        up = jnp.where(row == tm - 1, after, pltpu.roll(uc, tm - 1, axis=0))
        return um * cw_ref[0:1, cs] + uc * cw_ref[1:2, cs] + up * cw_ref[2:3, cs] + cb_ref[:, cs]

    for j in range(D_FF // FFN_CHUNK):
        gate = conv(j * FFN_CHUNK)
        half_val = conv(D_FF + j * FFN_CHUNK)
        act = gate * (1.0 + lax.erf(gate * math.sqrt(0.5))) * half_val
        part = _dot(act.astype(BF16), wo_ref[j * FFN_CHUNK:(j + 1) * FFN_CHUNK, :])
        if j == 0:
            acc_ref[...] = part
        else:
            acc_ref[...] += part
    o_ref[...] = xn_ref[...] + acc_ref[...]


def _ffn_out(u, xn, cw, cb, w_out, L):
    T = xn.shape[0]
    tm = ROW_TILE
    hpt = tm // SUBLANES
    n8 = T // SUBLANES
    row = lambda i: (i, 0)
    fix = lambda i: (0, 0)
    return pl.pallas_call(
        functools.partial(_ffn_out_kernel, tiles_per_seq=L // tm),
        out_shape=jax.ShapeDtypeStruct((T, D_MODEL), F32),
        grid=(T // tm,),
        in_specs=[pl.BlockSpec((tm, 2 * D_FF), row),
                  pl.BlockSpec((SUBLANES, 2 * D_FF), lambda i: (jnp.maximum(i * hpt - 1, 0), 0)),
                  pl.BlockSpec((SUBLANES, 2 * D_FF), lambda i: (jnp.minimum((i + 1) * hpt, n8 - 1), 0)),
                  pl.BlockSpec((tm, D_MODEL), row),
                  pl.BlockSpec((3, 2 * D_FF), fix),
                  pl.BlockSpec((1, 2 * D_FF), fix),
                  pl.BlockSpec((D_FF, D_MODEL), fix, pipeline_mode=pl.Buffered(1))],
        out_specs=pl.BlockSpec((tm, D_MODEL), row),
        scratch_shapes=[pltpu.VMEM((tm, D_MODEL), F32)],
        compiler_params=_params(1),
        name="ffn_out",
    )(u, u, u, xn, cw, cb, w_out)


def _head_ones(n_heads):
    return jnp.kron(jnp.eye(n_heads, dtype=F32), jnp.ones((HEAD_DIM, HEAD_DIM), F32)).astype(BF16)


def _layer(x2, B, L, mats, spectra, lw):
    q, kv, ph = _in_proj(x2, lw["norm1_g"], lw["wq_t"], lw["w_in"], lw["ek"], lw["qg"], lw["kg"])
    cab, hy = _mix(q, kv, ph, lw["sink"], lw["pw_bd"], lw["pool_scale"], lw["hy_w"], lw["hy_b"],
                   lw["g_a"], lw["g_b"], B, L)
    tb = math.gcd(B, 8)
    c5 = _hyena(hy, spectra[0], spectra[1], lw["hy_bias"], mats, tb)
    xn, u = _out_ffn_in(x2, cab, c5, lw["g_c"], lw["w_ab"], lw["w_c"],
                        lw["norm2_g"], lw["w_ffn_in"])
    return _ffn_out(u, xn, lw["ffn_cw"], lw["ffn_cb"], lw["w_ffn_out"], L)


def kernel(x_prompt, x_sample, norm1_g, w_in, q_norm_g, k_norm_g, attn_sink, pool_w, pool_scale, hy_conv_w,
           hy_conv_b, filt_w1, filt_b1, filt_freq1, filt_w2, filt_b2, filt_freq2, filt_w3, hy_bias, out_norm_g,
           w_out, norm2_g, w_ffn_in, ffn_conv_w, ffn_conv_b, w_ffn_out):
    groups = [x_prompt, x_sample]
    lens = sorted({g.shape[1] for g in groups})
    mats = {L: _dft_mats(L) for L in lens}
    ek = _head_ones(N_KV_HEADS)
    xs = [g.reshape(-1, D_MODEL) for g in groups]
    glu_scale = jnp.concatenate([jnp.ones((D_FF,), F32), jnp.full((D_FF,), 0.5, F32)])[None, :]
    for l in range(DEPTH):
        lw = dict(
            norm1_g=norm1_g[l][None, :], wq_t=w_in[l, :, :D_ATTN].T.astype(BF16), w_in=w_in[l, :, D_ATTN:].astype(BF16),
            ek=ek, kg=jnp.tile(k_norm_g[l], N_KV_HEADS)[None, :],
            qg=(jnp.tile(q_norm_g[l], N_Q_HEADS) * (1.0 / math.sqrt(HEAD_DIM)))[:, None],
            sink=attn_sink[l],
            pw_bd=jax.scipy.linalg.block_diag(*[pool_w[l, g] for g in range(len(POOL_WINDOWS))]).astype(BF16),
            pool_scale=pool_scale[l][None, :], hy_w=hy_conv_w[l], hy_b=hy_conv_b[l][None, :],
            g_a=out_norm_g[l, :D_ATTN][None, :], g_b=out_norm_g[l, D_ATTN:D_AB][None, :],
            g_c=out_norm_g[l, D_AB:][None, :], hy_bias=hy_bias[l],
            w_ab=w_out[l, :D_AB].astype(BF16), w_c=w_out[l, D_AB:].astype(BF16),
            norm2_g=norm2_g[l][None, :], w_ffn_in=w_ffn_in[l].astype(BF16),
            ffn_cw=ffn_conv_w[l] * glu_scale, ffn_cb=(ffn_conv_b[l] * glu_scale[0])[None, :],
            w_ffn_out=w_ffn_out[l].astype(BF16),
        )
        spectra = {}
        for L in lens:
            kern = _filter_kernels(L, filt_w1[l], filt_b1[l], filt_freq1[l], filt_w2[l], filt_b2[l], filt_freq2[l],
                                   filt_w3[l])
            spectra[L] = _filter_spectra(kern, mats[L], L)
        xs = [_layer(x2, g.shape[0], g.shape[1], mats[g.shape[1]], spectra[g.shape[1]], lw)
              for x2, g in zip(xs, groups)]
    return tuple(x2.reshape(g.shape) for x2, g in zip(xs, groups))
```

```python
import functools
import math

import jax
import jax.numpy as jnp
from jax import lax
from jax.experimental import pallas as pl
from jax.experimental.pallas import tpu as pltpu

F32 = jnp.float32
BF16 = jnp.bfloat16

D_MODEL = 1024
DEPTH = 2
HEAD_DIM = 64
N_Q_HEADS = 8
N_KV_HEADS = 2
GQA_GROUP = N_Q_HEADS // N_KV_HEADS
D_ATTN = N_Q_HEADS * HEAD_DIM
D_KV = N_KV_HEADS * HEAD_DIM
WINDOW = 128
BLOCK = 128
POOL_WINDOWS = (2, 4, 8, 16)
D_POOL = 256
POOL_GROUP_DIM = D_POOL // len(POOL_WINDOWS)
D_HYENA = 256
HYENA_ORDER = 2
FILTER_BANDS = 16
FILTER_EMB = 1 + 2 * FILTER_BANDS
FILTER_HIDDEN = 64
N_FILTERS = 2 * HYENA_ORDER
DECAY_FAST_PCT = 0.3
DECAY_SLOW_PCT = 1.5
DECAY_TARGET = 1e-2
D_HY_IN = (HYENA_ORDER + 1) * D_HYENA
D_PH = D_POOL + D_HY_IN
D_IN_PROJ = D_ATTN + 2 * D_KV + D_PH
D_CAT = D_ATTN + D_POOL + D_HYENA
D_AB = D_ATTN + D_POOL
D_FF = 2816
EPS = 1e-6
NEG_INF = -1e30

LANES = 128
SUBLANES = 8
FFT_N2 = 128
FFT_J = SUBLANES
FFT_TF = 8
FFT_JH = FFT_N2 // FFT_J
HY_CT = D_HYENA // LANES
FFN_CHUNKS = 2
BF16_ROWS = 16
ROW_TILE = 512
MIX_NB = 2
VMEM_LIMIT = 56 * 1024 * 1024


def _params(n_axes):
    return pltpu.CompilerParams(dimension_semantics=("arbitrary",) * n_axes, vmem_limit_bytes=VMEM_LIMIT)


def _rms(x, g):
    return x * lax.rsqrt(jnp.mean(x * x, axis=-1, keepdims=True) + EPS) * g


def _dot(a, b):
    return jnp.dot(a, b, preferred_element_type=F32)


def _store_tiles(ref, lead, row0, val):
    for nl in range(val.shape[0] // FFT_N2):
        for jh in range(FFT_JH):
            for ct in range(val.shape[1] // LANES):
                r0 = nl * FFT_N2 + jh * FFT_J
                ref[lead + (jh, ct, pl.ds(row0 + nl * FFT_J, FFT_J), slice(None))] = (
                    val[r0:r0 + FFT_J, ct * LANES:(ct + 1) * LANES])


def _load_tiles(ref, lead, row0, n_local, n_ct):
    return jnp.concatenate(
        [jnp.concatenate([ref[lead + (jh, ct, pl.ds(row0 + nl * FFT_J, FFT_J), slice(None))]
                          for nl in range(n_local) for jh in range(FFT_JH)], axis=0)
         for ct in range(n_ct)], axis=1)


def _pack_pair(re, im):
    hi = lax.bitcast_convert_type(re.astype(BF16).astype(F32), jnp.uint32)
    lo = lax.bitcast_convert_type(im.astype(BF16).astype(F32), jnp.uint32)
    return hi | (lo >> 16)


def _unpack_pair(w):
    re = lax.bitcast_convert_type(w & jnp.uint32(0xFFFF0000), F32)
    im = lax.bitcast_convert_type(w << 16, F32)
    return jnp.concatenate([re, im], axis=0).astype(BF16)


def _in_proj_kernel(x_ref, g_ref, wq_ref, w_ref, ek_ref, qg_ref, kg_ref, q_ref, kv_ref, ph_ref):
    h = _rms(x_ref[...], g_ref[...]).astype(BF16)
    tm = h.shape[0]

    qt = lax.dot_general(wq_ref[...], h, (((1,), (1,)), ((), ())), preferred_element_type=F32)
    qt = qt.reshape(N_Q_HEADS, HEAD_DIM, tm)
    qt = qt * lax.rsqrt(jnp.mean(qt * qt, axis=1, keepdims=True) + EPS)
    qt = (qt.reshape(D_ATTN, tm) * qg_ref[...]).astype(BF16)
    for j in range(tm // BLOCK):
        q_ref[j] = qt[:, j * BLOCK:(j + 1) * BLOCK]

    p = _dot(h, w_ref[...])
    k = p[:, :D_KV]
    ss = _dot((k * k).astype(BF16), ek_ref[...])
    kv_ref[:, :D_KV] = (k * lax.rsqrt(ss * (1.0 / HEAD_DIM) + EPS) * kg_ref[...]).astype(BF16)
    kv_ref[:, D_KV:] = p[:, D_KV:2 * D_KV].astype(BF16)
    ph_ref[...] = p[:, 2 * D_KV:]


def _in_proj(x2, g, wq_t, w, ek, qg, kg):
    T = x2.shape[0]
    tm = ROW_TILE
    row = lambda i: (i, 0)
    fix = lambda i: (0, 0)
    return pl.pallas_call(
        _in_proj_kernel,
        out_shape=[jax.ShapeDtypeStruct((T // BLOCK, D_ATTN, BLOCK), BF16),
                   jax.ShapeDtypeStruct((T, 2 * D_KV), BF16),
                   jax.ShapeDtypeStruct((T, D_PH), F32)],
        grid=(T // tm,),
        in_specs=[pl.BlockSpec((tm, D_MODEL), row),
                  pl.BlockSpec((1, D_MODEL), fix),
                  pl.BlockSpec((D_ATTN, D_MODEL), fix),
                  pl.BlockSpec((D_MODEL, 2 * D_KV + D_PH), fix),
                  pl.BlockSpec((D_KV, D_KV), fix),
                  pl.BlockSpec((D_ATTN, 1), fix),
                  pl.BlockSpec((1, D_KV), fix)],
        out_specs=[pl.BlockSpec((tm // BLOCK, D_ATTN, BLOCK), lambda i: (i, 0, 0)),
                   pl.BlockSpec((tm, 2 * D_KV), row),
                   pl.BlockSpec((tm, D_PH), row)],
        compiler_params=_params(1),
        name="in_proj",
    )(x2, g, wq_t, w, ek, qg, kg)


def _mix_kernel(sink_ref, *refs, seq_len):
    bias_refs = refs[:MIX_NB]
    (q_ref, kvp_ref, kvc_ref, kvn_ref, phc_ref, php_ref, phn_ref, pw_ref, ps_ref, hw_ref, hb_ref, ga_ref, gb_ref,
     cat_ref, hy_ref, ext_ref, att_ref, s_ref, p_ref, rden_ref) = refs[MIX_NB:]
    l = pl.program_id(1)
    is_first = l == 0
    is_last = l == pl.num_programs(1) - 1
    rows = MIX_NB * BLOCK

    tn = (((0,), (0,)), ((), ()))

    def keys(sb, cols):
        def blk(i):
            if i < 0:
                return kvp_ref[:, cols]
            if i >= MIX_NB:
                return kvn_ref[:, cols]
            return kvc_ref[i * BLOCK:(i + 1) * BLOCK, cols]
        return jnp.concatenate([blk(sb - 1), blk(sb), blk(sb + 1)], axis=0)

    pairs = [(sb, g) for sb in range(MIX_NB) for g in range(N_KV_HEADS)]
    gw = GQA_GROUP * BLOCK
    for i, (sb, g) in enumerate(pairs):
        qg = jnp.concatenate([q_ref[sb, h * HEAD_DIM:(h + 1) * HEAD_DIM, :]
                              for h in range(g * GQA_GROUP, (g + 1) * GQA_GROUP)], axis=1)
        st = _dot(keys(sb, slice(g * HEAD_DIM, (g + 1) * HEAD_DIM)), qg)
        for j in range(GQA_GROUP):
            s_ref[i, :, j * BLOCK:(j + 1) * BLOCK] = st[:, j * BLOCK:(j + 1) * BLOCK] + bias_refs[sb][g * GQA_GROUP + j]
    for i, (sb, g) in enumerate(pairs):
        for j in range(GQA_GROUP):
            cols = slice(j * BLOCK, (j + 1) * BLOCK)
            st = s_ref[i, :, cols]
            sk = sink_ref[g * GQA_GROUP + j]
            m = jnp.maximum(jnp.max(st, axis=0, keepdims=True), sk)
            e = jnp.exp(st - m)
            rden_ref[i, :, cols] = 1.0 / (jnp.sum(e, axis=0, keepdims=True) + jnp.exp(sk - m))
            p_ref[i, :, cols] = e.astype(BF16)
    for i, (sb, g) in enumerate(pairs):
        vh = keys(sb, slice(D_KV + g * HEAD_DIM, D_KV + (g + 1) * HEAD_DIM))
        ot = lax.dot_general(vh, p_ref[i], tn, preferred_element_type=F32) * rden_ref[i]
        for j in range(GQA_GROUP):
            h = g * GQA_GROUP + j
            att_ref[sb, h * HEAD_DIM:(h + 1) * HEAD_DIM, :] = ot[:, j * BLOCK:(j + 1) * BLOCK]
    for sb in range(MIX_NB):
        cat_ref[sb * BLOCK:(sb + 1) * BLOCK, :D_ATTN] = _rms(att_ref[sb].T, ga_ref[...]).astype(BF16)

    ext_ref[0:SUBLANES, :] = jnp.where(is_first, 0.0, php_ref[...])
    ext_ref[SUBLANES:SUBLANES + rows, :] = phc_ref[...]
    ext_ref[SUBLANES + rows:, :] = jnp.where(is_last, 0.0, phn_ref[...])

    t = l * rows + lax.broadcasted_iota(jnp.int32, (rows, 1), 0)
    lane = lax.broadcasted_iota(jnp.int32, (1, LANES), 1)

    def window_mean(w, col0):
        lo = jnp.clip(t - w // 2, 0, seq_len)
        hi = jnp.clip(t - w // 2 + w, 0, seq_len)
        tot = ext_ref[SUBLANES - w // 2:SUBLANES - w // 2 + rows, col0:col0 + LANES]
        for d in range(1 - w // 2, w // 2):
            tot = tot + ext_ref[SUBLANES + d:SUBLANES + d + rows, col0:col0 + LANES]
        return tot / (hi - lo).astype(F32)

    gpt = LANES // POOL_GROUP_DIM
    means = [jnp.where(lane < POOL_GROUP_DIM, window_mean(POOL_WINDOWS[gpt * i], i * LANES),
                       window_mean(POOL_WINDOWS[gpt * i + 1], i * LANES)) for i in range(D_POOL // LANES)]
    u = ext_ref[SUBLANES:SUBLANES + rows, 0:D_POOL]
    pooled = _dot((jnp.concatenate(means, axis=1) - u).astype(BF16), pw_ref[...]) * ps_ref[...]
    cat_ref[:, D_ATTN:] = _rms(pooled, gb_ref[...]).astype(BF16)

    hp = ext_ref[SUBLANES - 1:SUBLANES - 1 + rows, D_POOL:]
    hc = ext_ref[SUBLANES:SUBLANES + rows, D_POOL:]
    hn = ext_ref[SUBLANES + 1:SUBLANES + 1 + rows, D_POOL:]
    _store_tiles(hy_ref, (), 0, hp * hw_ref[0:1, :] + hc * hw_ref[1:2, :] + hn * hw_ref[2:3, :] + hb_ref[...])


def _attn_bias():
    row = jnp.arange(BLOCK, dtype=jnp.int32)[None, :]
    col = jnp.arange(3 * BLOCK, dtype=jnp.int32)[:, None]
    dist = jnp.abs(row + BLOCK - col)
    slopes = 2.0 ** (-8.0 * (jnp.arange(N_Q_HEADS, dtype=F32) + 1.0) / N_Q_HEADS)
    bias = -slopes[:, None, None] * dist.astype(F32)[None]
    out = []
    for case in range(4):
        valid = dist <= WINDOW
        if case & 1:
            valid = valid & (col >= BLOCK)
        if case & 2:
            valid = valid & (col < 2 * BLOCK)
        out.append(jnp.where(valid[None], bias, NEG_INF))
    return jnp.stack(out)


def _mix(q, kv, ph, sink, pw_bd, pool_scale, hy_w, hy_b, g_a, g_b, B, L):
    T = B * L
    rows = MIX_NB * BLOCK
    nb = L // BLOCK
    ns = L // rows
    rps = rows // SUBLANES
    n8 = T // SUBLANES
    cur = lambda b, l: (b * ns + l, 0)
    prev = lambda b, l: (b * nb + jnp.maximum(l * MIX_NB - 1, 0), 0)
    nxt = lambda b, l: (b * nb + jnp.minimum((l + 1) * MIX_NB, nb - 1), 0)
    prev8 = lambda b, l: (jnp.maximum((b * ns + l) * rps - 1, 0), 0)
    next8 = lambda b, l: (jnp.minimum((b * ns + l + 1) * rps, n8 - 1), 0)
    fix = lambda b, l: (0, 0)

    def bias_spec(sb):
        def index(b, l):
            case = jnp.int32(0)
            if sb == 0:
                case = case + (l == 0).astype(jnp.int32)
            if sb == MIX_NB - 1:
                case = case + 2 * (l == ns - 1).astype(jnp.int32)
            return (case, 0, 0, 0)
        return pl.BlockSpec((None, N_Q_HEADS, 3 * BLOCK, BLOCK), index)

    bias = _attn_bias()
    n_pairs = MIX_NB * N_KV_HEADS
    gw = GQA_GROUP * BLOCK
    return pl.pallas_call(
        functools.partial(_mix_kernel, seq_len=L),
        out_shape=[jax.ShapeDtypeStruct((T, D_AB), BF16),
                   jax.ShapeDtypeStruct((B, FFT_JH, D_HY_IN // LANES, nb * FFT_J, LANES), F32)],
        grid=(B, ns),
        in_specs=[pl.BlockSpec(memory_space=pltpu.SMEM)] + [bias_spec(sb) for sb in range(MIX_NB)] + [
                  pl.BlockSpec((MIX_NB, D_ATTN, BLOCK), lambda b, l: (b * ns + l, 0, 0)),
                  pl.BlockSpec((BLOCK, 2 * D_KV), prev),
                  pl.BlockSpec((rows, 2 * D_KV), cur),
                  pl.BlockSpec((BLOCK, 2 * D_KV), nxt),
                  pl.BlockSpec((rows, D_PH), cur),
                  pl.BlockSpec((SUBLANES, D_PH), prev8),
                  pl.BlockSpec((SUBLANES, D_PH), next8),
                  pl.BlockSpec((D_POOL, D_POOL), fix),
                  pl.BlockSpec((1, D_POOL), fix),
                  pl.BlockSpec((3, D_HY_IN), fix),
                  pl.BlockSpec((1, D_HY_IN), fix),
                  pl.BlockSpec((1, D_ATTN), fix),
                  pl.BlockSpec((1, D_POOL), fix)],
        out_specs=[pl.BlockSpec((rows, D_AB), cur),
                   pl.BlockSpec((None, FFT_JH, D_HY_IN // LANES, MIX_NB * FFT_J, LANES),
                                lambda b, l: (b, 0, 0, l, 0))],
        scratch_shapes=[pltpu.VMEM((rows + 2 * SUBLANES, D_PH), F32),
                        pltpu.VMEM((MIX_NB, D_ATTN, BLOCK), F32),
                        pltpu.VMEM((n_pairs, 3 * BLOCK, gw), F32),
                        pltpu.VMEM((n_pairs, 3 * BLOCK, gw), BF16),
                        pltpu.VMEM((n_pairs, 1, gw), F32)],
        compiler_params=_params(2),
        name="mix",
    )(sink, *([bias] * MIX_NB), q, kv, kv, kv, ph, ph, ph, pw_bd, pool_scale, hy_w, hy_b, g_a, g_b)


def _split3(x, axis):
    hi = x.astype(BF16)
    lo = (x - hi.astype(F32)).astype(BF16)
    return jnp.concatenate([hi, lo, hi], axis=axis)


def _stack3(w):
    hi = w.astype(BF16)
    lo = (w - hi.astype(F32)).astype(BF16)
    return jnp.concatenate([hi, hi, lo], axis=0)


def _filter_kernel(z_ref, t_ref, w1_ref, b1_ref, f1_ref, w2_ref, b2_ref, f2_ref, w3_ref, dl_ref, o_ref, *, seq_len):
    h = jnp.sin(f1_ref[...] * (_dot(w1_ref[...], z_ref[...]) + b1_ref[...]))
    h = jnp.sin(f2_ref[...] * (_dot(w2_ref[...], _split3(h, 0)) + b2_ref[...]))
    h = lax.dot_general(_split3(h, 0), w3_ref[...], (((0,), (0,)), ((), ())),
                        preferred_element_type=F32)
    n = pl.program_id(0) * h.shape[0] + lax.broadcasted_iota(jnp.int32, (h.shape[0], 1), 0)
    decay = jnp.where(n == seq_len, 0.0, jnp.exp(-t_ref[...] * dl_ref[...]))
    for o in range(HYENA_ORDER):
        _store_tiles(o_ref, (o,), 0, h[:, o * D_HYENA:(o + 1) * D_HYENA] * decay)


def _filter_kernels(L, w1, b1, fr1, w2, b2, fr2, w3):
    pos = jnp.arange(2 * L, dtype=jnp.int32)
    tpos = jnp.where(pos < L, pos, (2 * L - pos) % L).astype(F32)[:, None]
    t_norm = tpos * (1.0 / (L - 1))
    bands = jnp.linspace(1e-4, FILTER_BANDS - 1, FILTER_BANDS, dtype=F32)[None, :]
    ang = 2.0 * math.pi * tpos * bands / L
    z = jnp.concatenate([t_norm, jnp.cos(ang), -jnp.sin(ang)], axis=-1)
    z3 = jnp.pad(_split3(z, 1), ((0, 0), (0, LANES - 3 * FILTER_EMB))).T
    w1s = jnp.pad(_stack3(w1), ((0, LANES - 3 * FILTER_EMB), (0, 0)))
    w3d = w3.reshape(FILTER_HIDDEN, HYENA_ORDER, 2, D_HYENA).transpose(2, 0, 1, 3).reshape(
        2, FILTER_HIDDEN, HYENA_ORDER * D_HYENA)
    w3s = jnp.stack([_stack3(w3d[0]), _stack3(w3d[1])])
    max_decay = math.log(DECAY_TARGET) / DECAY_FAST_PCT
    min_decay = math.log(DECAY_TARGET) / DECAY_SLOW_PCT
    dl = jnp.abs(jnp.linspace(min_decay, max_decay, D_HYENA, dtype=F32))[None, :]
    tl = 512
    half = L // tl
    row = lambda i: (i, 0)
    fix = lambda i: (0, 0)
    return pl.pallas_call(
        functools.partial(_filter_kernel, seq_len=L),
        out_shape=jax.ShapeDtypeStruct((HYENA_ORDER, FFT_JH, HY_CT, 2 * L // FFT_N2 * FFT_J, LANES), F32),
        grid=(2 * L // tl,),
        in_specs=[pl.BlockSpec((LANES, tl), lambda i: (0, i)),
                  pl.BlockSpec((tl, 1), row),
                  pl.BlockSpec((FILTER_HIDDEN, LANES), fix),
                  pl.BlockSpec((FILTER_HIDDEN, 1), fix),
                  pl.BlockSpec((FILTER_HIDDEN, 1), fix),
                  pl.BlockSpec((FILTER_HIDDEN, 3 * FILTER_HIDDEN), fix),
                  pl.BlockSpec((FILTER_HIDDEN, 1), fix),
                  pl.BlockSpec((FILTER_HIDDEN, 1), fix),
                  pl.BlockSpec((None, 3 * FILTER_HIDDEN, HYENA_ORDER * D_HYENA), lambda i: (i // half, 0, 0)),
                  pl.BlockSpec((1, D_HYENA), fix)],
        out_specs=pl.BlockSpec((HYENA_ORDER, FFT_JH, HY_CT, tl // FFT_N2 * FFT_J, LANES), lambda i: (0, 0, 0, i, 0)),
        compiler_params=_params(1),
        name="hyena_filters",
    )(z3, t_norm, w1s.T, b1[:, None], fr1[:, None], _stack3(w2).T, b2[:, None], fr2[:, None], w3s, dl)


def _dft_mats(L):
    N = 2 * L
    n1n = N // FFT_N2
    r = n1n // 2
    assert r % FFT_TF == 0, "sequence length must be a multiple of FFT_TF * FFT_N2 / 2"
    mh = r + FFT_TF
    f1 = jnp.arange(mh, dtype=jnp.int32)[:, None]
    n1 = jnp.arange(n1n, dtype=jnp.int32)[None, :]
    keep = (f1 <= r).astype(F32)
    th = (2.0 * math.pi / n1n) * ((f1 * n1) % n1n).astype(F32)
    c1, s1 = keep * jnp.cos(th), keep * jnp.sin(th)
    fwd_full = jnp.concatenate([c1, -s1], axis=0).astype(BF16)
    fwd_half = fwd_full[:, :r]
    w = jnp.where((f1 == 0) | (f1 == r), 1.0, 2.0)
    inv_half = jnp.concatenate([(w * c1)[:, :r].T, (-w * s1)[:, :r].T], axis=1).astype(BF16)
    i2 = jnp.arange(2 * FFT_N2, dtype=jnp.int32)
    part, idx = i2 // FFT_N2, i2 % FFT_N2
    pa = (2.0 * math.pi / N) * ((jnp.arange(mh, dtype=jnp.int32)[:, None] * idx[None, :]) % N).astype(F32)
    ca, sa = jnp.cos(pa), jnp.sin(pa)
    mb = (idx[:, None] * idx[None, :] * n1n + (part[:, None] - part[None, :]) * (N // 4)) % N
    pb = (2.0 * math.pi / N) * mb.astype(F32)
    cb, sb = jnp.cos(pb), jnp.sin(pb)
    g = (ca[:, None, :] * cb[None] - sa[:, None, :] * sb[None]).astype(BF16)
    ginv = (ca[:, :, None] * cb.T[None] - sa[:, :, None] * sb.T[None]).astype(BF16)
    return dict(n1=n1n, r=r, mh=mh, fwd_full=fwd_full, fwd_half=fwd_half, inv_half=inv_half, g=g, ginv=ginv)


def _outer_blocks(tb):
    return min(4, max(1, 8 // tb))


def _fft_a_kernel(x_ref, f_ref, a_ref, *, mh):
    f = f_ref[...]
    r = x_ref.shape[3] // FFT_J
    for lead in [(b, jb, ct) for b in range(x_ref.shape[0]) for jb in range(x_ref.shape[1]) for ct in range(HY_CT)]:
        for j in range(FFT_J):
            a = _dot(f, x_ref[lead + (pl.ds(j, r, stride=FFT_J), slice(None))].astype(BF16))
            a_ref[lead + (pl.ds(j, mh, stride=FFT_J), slice(None))] = _pack_pair(a[:mh], a[mh:])


def _fft_a(x5, coff, fmat, mh, tb):
    B, rj = x5.shape[0], x5.shape[3]
    jb = _outer_blocks(tb)
    return pl.pallas_call(
        functools.partial(_fft_a_kernel, mh=mh),
        out_shape=jax.ShapeDtypeStruct((B, FFT_JH, HY_CT, mh * FFT_J, LANES), jnp.uint32),
        grid=(B // tb, FFT_JH // jb),
        in_specs=[pl.BlockSpec((tb, jb, HY_CT, rj, LANES), lambda b, j: (b, j, coff // HY_CT, 0, 0)),
                  pl.BlockSpec((2 * mh, rj // FFT_J), lambda b, j: (0, 0))],
        out_specs=pl.BlockSpec((tb, jb, HY_CT, mh * FFT_J, LANES), lambda b, j: (b, j, 0, 0, 0)),
        compiler_params=_params(2),
        name="fft_outer_fwd",
    )(x5, fmat)


def _fft_spec_kernel(a_ref, g_ref, kr_ref, ki_ref, *, scale):
    for f in range(FFT_TF):
        x = _dot(g_ref[f], _unpack_pair(_load_tiles(a_ref, (), f * FFT_J, 1, HY_CT))) * scale
        kr_ref[0, f] = x[:FFT_N2]
        ki_ref[0, f] = x[FFT_N2:]


def _tile_spec():
    return pl.BlockSpec((None, FFT_JH, HY_CT, FFT_TF * FFT_J, LANES), lambda f, b: (b, 0, 0, f, 0))


def _fft_spec(a, g, scale):
    B, mh = a.shape[0], a.shape[3] // FFT_J
    nat = pl.BlockSpec((1, FFT_TF, FFT_N2, D_HYENA), lambda f, b: (b, f, 0, 0))
    out = jax.ShapeDtypeStruct((B, mh, FFT_N2, D_HYENA), F32)
    return pl.pallas_call(
        functools.partial(_fft_spec_kernel, scale=scale),
        out_shape=[out, out],
        grid=(mh // FFT_TF, B),
        in_specs=[_tile_spec(), pl.BlockSpec((FFT_TF, 2 * FFT_N2, 2 * FFT_N2), lambda f, b: (f, 0, 0))],
        out_specs=[nat, nat],
        compiler_params=_params(2),
        name="fft_inner_fwd",
    )(a, g)


def _fft_b_kernel(a_ref, g_ref, gi_ref, kr_ref, ki_ref, y_ref):
    for f in range(FFT_TF):
        x = _dot(g_ref[f], _unpack_pair(_load_tiles(a_ref, (), f * FFT_J, 1, HY_CT)))
        xr, xi = x[:FFT_N2], x[FFT_N2:]
        kr, ki = kr_ref[0, f], ki_ref[0, f]
        z = jnp.concatenate([xr * kr - xi * ki, xr * ki + xi * kr], axis=0).astype(BF16)
        y = _dot(gi_ref[f], z)
        _store_tiles(y_ref, (), f * FFT_J, _pack_pair(y[:FFT_N2], y[FFT_N2:]))


def _fft_b(a, g, ginv, kr, ki, o):
    B, mh = a.shape[0], a.shape[3] // FFT_J
    mat = pl.BlockSpec((FFT_TF, 2 * FFT_N2, 2 * FFT_N2), lambda f, b: (f, 0, 0))
    spec = pl.BlockSpec((1, FFT_TF, FFT_N2, D_HYENA), lambda f, b: (o, f, 0, 0))
    return pl.pallas_call(
        _fft_b_kernel,
        out_shape=jax.ShapeDtypeStruct(a.shape, jnp.uint32),
        grid=(mh // FFT_TF, B),
        in_specs=[_tile_spec(), mat, mat, spec, spec],
        out_specs=_tile_spec(),
        compiler_params=_params(2),
        name="fft_inner",
    )(a, g, ginv, kr, ki)


def _fft_c_kernel(y_ref, c_ref, u_ref, gate_ref, bias_ref, o_ref):
    cm = c_ref[...]
    mh, r = y_ref.shape[3] // FFT_J, u_ref.shape[3] // FFT_J
    for lead in [(b, jb, ct) for b in range(y_ref.shape[0]) for jb in range(y_ref.shape[1]) for ct in range(HY_CT)]:
        bias = bias_ref[:, lead[2] * LANES:(lead[2] + 1) * LANES]
        for j in range(FFT_J):
            y = _dot(cm, _unpack_pair(y_ref[lead + (pl.ds(j, mh, stride=FFT_J), slice(None))]))
            rows = lead + (pl.ds(j, r, stride=FFT_J), slice(None))
            o_ref[rows] = gate_ref[rows] * (y + bias * u_ref[rows])


def _fft_c(y, cmat, u5, uoff, gate5, goff, bias, tb):
    B, mhj = y.shape[0], y.shape[3]
    rj = cmat.shape[0] * FFT_J
    jb = _outer_blocks(tb)
    blk = lambda b, j: (b, j, 0, 0, 0)
    return pl.pallas_call(
        _fft_c_kernel,
        out_shape=jax.ShapeDtypeStruct((B, FFT_JH, HY_CT, rj, LANES), F32),
        grid=(B // tb, FFT_JH // jb),
        in_specs=[pl.BlockSpec((tb, jb, HY_CT, mhj, LANES), blk),
                  pl.BlockSpec(cmat.shape, lambda b, j: (0, 0)),
                  pl.BlockSpec((tb, jb, HY_CT, rj, LANES), lambda b, j: (b, j, uoff // HY_CT, 0, 0)),
                  pl.BlockSpec((tb, jb, HY_CT, rj, LANES), lambda b, j: (b, j, goff // HY_CT, 0, 0)),
                  pl.BlockSpec((1, D_HYENA), lambda b, j: (0, 0))],
        out_specs=pl.BlockSpec((tb, jb, HY_CT, rj, LANES), blk),
        compiler_params=_params(2),
        name="fft_outer_inv",
    )(y, cmat, u5, gate5, bias)


def _filter_spectra(kern, mats, L):
    return _fft_spec(_fft_a(kern, 0, mats["fwd_full"], mats["mh"], 1), mats["g"], 1.0 / (2 * L))


def _hyena(hy5, kr, ki, hy_bias, mats, tb):
    z5 = hy5
    for o in range(HYENA_ORDER):
        a = _fft_a(z5, 0, mats["fwd_half"], mats["mh"], tb)
        y = _fft_b(a, mats["g"], mats["ginv"], kr, ki, o)
        z5 = _fft_c(y, mats["inv_half"], z5, 0, hy5, (o + 1) * HY_CT, hy_bias[o][None, :], tb)
    return z5


def _out_proj_kernel(x_ref, cab_ref, c_ref, gc_ref, wab_ref, wc_ref, g2_ref, xn_ref, h_ref):
    c = _load_tiles(c_ref, (), 0, c_ref.shape[2] // FFT_J, HY_CT)
    cn = _rms(c, gc_ref[...]).astype(BF16)
    xn = x_ref[...] + _dot(cab_ref[...], wab_ref[...]) + _dot(cn, wc_ref[...])
    xn_ref[...] = xn
    h_ref[...] = _rms(xn, g2_ref[...]).astype(BF16)


def _out_proj(x2, cab, c5, g_c, w_ab, w_c, g2):
    T = x2.shape[0]
    tm = ROW_TILE
    tps = c5.shape[3] // FFT_J * FFT_N2 // tm
    row = lambda i: (i, 0)
    fix = lambda i: (0, 0)
    return pl.pallas_call(
        _out_proj_kernel,
        out_shape=[jax.ShapeDtypeStruct((T, D_MODEL), F32),
                   jax.ShapeDtypeStruct((T, D_MODEL), BF16)],
        grid=(T // tm,),
        in_specs=[pl.BlockSpec((tm, D_MODEL), row),
                  pl.BlockSpec((tm, D_AB), row),
                  pl.BlockSpec((None, FFT_JH, HY_CT, tm // FFT_N2 * FFT_J, LANES),
                               lambda i: (i // tps, 0, 0, i % tps, 0)),
                  pl.BlockSpec((1, D_HYENA), fix),
                  pl.BlockSpec((D_AB, D_MODEL), fix),
                  pl.BlockSpec((D_HYENA, D_MODEL), fix),
                  pl.BlockSpec((1, D_MODEL), fix)],
        out_specs=[pl.BlockSpec((tm, D_MODEL), row),
                   pl.BlockSpec((tm, D_MODEL), row)],
        compiler_params=_params(1),
        name="out_proj",
    )(x2, cab, c5, g_c, w_ab, w_c, g2)


def _ffn_kernel(h_ref, hp_ref, hn_ref, xn_ref, wi_ref, cw_ref, cb_ref, wo_ref, o_ref,
                acc_ref, hx_ref, ug_ref, uv_ref, *, tiles_per_seq):
    i = pl.program_id(0)
    is_first = (i % tiles_per_seq) == 0
    is_last = (i % tiles_per_seq) == tiles_per_seq - 1
    tm = h_ref.shape[0]
    row = lax.broadcasted_iota(jnp.int32, (tm, 1), 0)
    hx_ref[0:tm, :] = h_ref[...]
    hx_ref[tm:tm + BF16_ROWS, :] = hp_ref[...]
    hx_ref[tm + BF16_ROWS:, :] = hn_ref[...]

    def up(j):
        ug_ref[j % 2] = _dot(hx_ref[...], wi_ref[j])
        uv_ref[j % 2] = _dot(hx_ref[...], wi_ref[FFN_CHUNKS + j])

    def conv(u_ref, j, k):
        uc = u_ref[j % 2, 0:tm, :]
        before = jnp.where(is_first, 0.0, u_ref[j % 2, tm + BF16_ROWS - 1:tm + BF16_ROWS, :])
        after = jnp.where(is_last, 0.0, u_ref[j % 2, tm + BF16_ROWS:tm + BF16_ROWS + 1, :])
        um = jnp.where(row == 0, before, pltpu.roll(uc, 1, axis=0))
        un = jnp.where(row == tm - 1, after, pltpu.roll(uc, tm - 1, axis=0))
        w = cw_ref[k]
        return um * w[0:1] + uc * w[1:2] + un * w[2:3] + cb_ref[k]

    def down(j):
        gate = conv(ug_ref, j, j)
        half_val = conv(uv_ref, j, FFN_CHUNKS + j)
        act = gate * (1.0 + lax.erf(gate * math.sqrt(0.5))) * half_val
        acc_ref[...] += _dot(act.astype(BF16), wo_ref[j])

    acc_ref[...] = xn_ref[...]
    up(0)
    for j in range(FFN_CHUNKS):
        if j + 1 < FFN_CHUNKS:
            up(j + 1)
        down(j)
    o_ref[...] = acc_ref[...]


def _ffn(h, xn, w_in, cw, cb, w_out, L):
    T = xn.shape[0]
    tm = ROW_TILE
    ch = D_FF // FFN_CHUNKS
    hpt = tm // BF16_ROWS
    n16 = T // BF16_ROWS
    wi = w_in.reshape(D_MODEL, 2 * FFN_CHUNKS, ch).transpose(1, 0, 2)
    cwc = cw.reshape(3, 2 * FFN_CHUNKS, ch).transpose(1, 0, 2)
    cbc = cb.reshape(2 * FFN_CHUNKS, 1, ch)
    wo = w_out.reshape(FFN_CHUNKS, ch, D_MODEL)
    row = lambda i: (i, 0)
    fix3 = lambda i: (0, 0, 0)
    once = pl.Buffered(1)
    return pl.pallas_call(
        functools.partial(_ffn_kernel, tiles_per_seq=L // tm),
        out_shape=jax.ShapeDtypeStruct((T, D_MODEL), F32),
        grid=(T // tm,),
        in_specs=[pl.BlockSpec((tm, D_MODEL), row),
                  pl.BlockSpec((BF16_ROWS, D_MODEL), lambda i: (jnp.maximum(i * hpt - 1, 0), 0)),
                  pl.BlockSpec((BF16_ROWS, D_MODEL), lambda i: (jnp.minimum((i + 1) * hpt, n16 - 1), 0)),
                  pl.BlockSpec((tm, D_MODEL), row),
                  pl.BlockSpec((2 * FFN_CHUNKS, D_MODEL, ch), fix3, pipeline_mode=once),
                  pl.BlockSpec((2 * FFN_CHUNKS, 3, ch), fix3),
                  pl.BlockSpec((2 * FFN_CHUNKS, 1, ch), fix3),
                  pl.BlockSpec((FFN_CHUNKS, ch, D_MODEL), fix3, pipeline_mode=once)],
        out_specs=pl.BlockSpec((tm, D_MODEL), row),
        scratch_shapes=[pltpu.VMEM((tm, D_MODEL), F32),
                        pltpu.VMEM((tm + 2 * BF16_ROWS, D_MODEL), BF16),
                        pltpu.VMEM((2, tm + 2 * BF16_ROWS, ch), F32),
                        pltpu.VMEM((2, tm + 2 * BF16_ROWS, ch), F32)],
        compiler_params=_params(1),
        name="ffn",
    )(h, h, h, xn, wi, cwc, cbc, wo)


def _head_ones(n_heads):
    return jnp.kron(jnp.eye(n_heads, dtype=F32), jnp.ones((HEAD_DIM, HEAD_DIM), F32)).astype(BF16)


def _layer(x2, B, L, mats, spectra, lw):
    q, kv, ph = _in_proj(x2, lw["norm1_g"], lw["wq_t"], lw["w_in"], lw["ek"], lw["qg"], lw["kg"])
    cab, hy = _mix(q, kv, ph, lw["sink"], lw["pw_bd"], lw["pool_scale"], lw["hy_w"], lw["hy_b"],
                   lw["g_a"], lw["g_b"], B, L)
    tb = math.gcd(B, 8)
    c5 = _hyena(hy, spectra[0], spectra[1], lw["hy_bias"], mats, tb)
    xn, h = _out_proj(x2, cab, c5, lw["g_c"], lw["w_ab"], lw["w_c"], lw["norm2_g"])
    return _ffn(h, xn, lw["w_ffn_in"], lw["ffn_cw"], lw["ffn_cb"], lw["w_ffn_out"], L)


def kernel(x_prompt, x_sample, norm1_g, w_in, q_norm_g, k_norm_g, attn_sink, pool_w, pool_scale, hy_conv_w,
           hy_conv_b, filt_w1, filt_b1, filt_freq1, filt_w2, filt_b2, filt_freq2, filt_w3, hy_bias, out_norm_g,
           w_out, norm2_g, w_ffn_in, ffn_conv_w, ffn_conv_b, w_ffn_out):
    groups = [x_prompt, x_sample]
    lens = sorted({g.shape[1] for g in groups})
    mats = {L: _dft_mats(L) for L in lens}
    ek = _head_ones(N_KV_HEADS)
    xs = [g.reshape(-1, D_MODEL) for g in groups]
    glu_scale = jnp.concatenate([jnp.ones((D_FF,), F32), jnp.full((D_FF,), 0.5, F32)])[None, :]
    for l in range(DEPTH):
        lw = dict(
            norm1_g=norm1_g[l][None, :], wq_t=w_in[l, :, :D_ATTN].T.astype(BF16), w_in=w_in[l, :, D_ATTN:].astype(BF16),
            ek=ek, kg=jnp.tile(k_norm_g[l], N_KV_HEADS)[None, :],
            qg=(jnp.tile(q_norm_g[l], N_Q_HEADS) * (1.0 / math.sqrt(HEAD_DIM)))[:, None],
            sink=attn_sink[l],
            pw_bd=jax.scipy.linalg.block_diag(*[pool_w[l, g] for g in range(len(POOL_WINDOWS))]).astype(BF16),
            pool_scale=pool_scale[l][None, :], hy_w=hy_conv_w[l], hy_b=hy_conv_b[l][None, :],
            g_a=out_norm_g[l, :D_ATTN][None, :], g_b=out_norm_g[l, D_ATTN:D_AB][None, :],
            g_c=out_norm_g[l, D_AB:][None, :], hy_bias=hy_bias[l],
            w_ab=w_out[l, :D_AB].astype(BF16), w_c=w_out[l, D_AB:].astype(BF16),
            norm2_g=norm2_g[l][None, :], w_ffn_in=w_ffn_in[l].astype(BF16),
            ffn_cw=ffn_conv_w[l] * glu_scale, ffn_cb=(ffn_conv_b[l] * glu_scale[0])[None, :],
            w_ffn_out=w_ffn_out[l].astype(BF16),
        )
        spectra = {}
        for L in lens:
            kern = _filter_kernels(L, filt_w1[l], filt_b1[l], filt_freq1[l], filt_w2[l], filt_b2[l], filt_freq2[l],
                                   filt_w3[l])
            spectra[L] = _filter_spectra(kern, mats[L], L)
        xs = [_layer(x2, g.shape[0], g.shape[1], mats[g.shape[1]], spectra[g.shape[1]], lw)
              for x2, g in zip(xs, groups)]
    return tuple(x2.reshape(g.shape) for x2, g in zip(xs, groups))
```

```python
import functools
import math

import jax
import jax.numpy as jnp
from jax import lax
from jax.experimental import pallas as pl
from jax.experimental.pallas import tpu as pltpu

F32 = jnp.float32
BF16 = jnp.bfloat16

D_MODEL = 1024
DEPTH = 2
HEAD_DIM = 64
N_Q_HEADS = 8
N_KV_HEADS = 2
GQA_GROUP = N_Q_HEADS // N_KV_HEADS
D_ATTN = N_Q_HEADS * HEAD_DIM
D_KV = N_KV_HEADS * HEAD_DIM
WINDOW = 128
BLOCK = 128
POOL_WINDOWS = (2, 4, 8, 16)
D_POOL = 256
POOL_GROUP_DIM = D_POOL // len(POOL_WINDOWS)
D_HYENA = 256
HYENA_ORDER = 2
FILTER_BANDS = 16
FILTER_EMB = 1 + 2 * FILTER_BANDS
FILTER_HIDDEN = 64
N_FILTERS = 2 * HYENA_ORDER
DECAY_FAST_PCT = 0.3
DECAY_SLOW_PCT = 1.5
DECAY_TARGET = 1e-2
D_HY_IN = (HYENA_ORDER + 1) * D_HYENA
D_PH = D_POOL + D_HY_IN
D_IN_PROJ = D_ATTN + 2 * D_KV + D_PH
D_CAT = D_ATTN + D_POOL + D_HYENA
D_AB = D_ATTN + D_POOL
D_FF = 2816
EPS = 1e-6
NEG_INF = -1e30

LANES = 128
SUBLANES = 8
FFT_N2 = 128
FFT_J = SUBLANES
FFT_TF = 8
FFT_JH = FFT_N2 // FFT_J
HY_CT = D_HYENA // LANES
FFN_CHUNKS = 2
BF16_ROWS = 16
ROW_TILE = 512
MIX_NB = 2
VMEM_LIMIT = 56 * 1024 * 1024


def _params(n_axes):
    return pltpu.CompilerParams(dimension_semantics=("arbitrary",) * n_axes, vmem_limit_bytes=VMEM_LIMIT)


def _rms(x, g):
    return x * lax.rsqrt(jnp.mean(x * x, axis=-1, keepdims=True) + EPS) * g


def _dot(a, b):
    return jnp.dot(a, b, preferred_element_type=F32)


def _store_tiles(ref, lead, row0, val):
    for nl in range(val.shape[0] // FFT_N2):
        for jh in range(FFT_JH):
            for ct in range(val.shape[1] // LANES):
                r0 = nl * FFT_N2 + jh * FFT_J
                ref[lead + (jh, ct, pl.ds(row0 + nl * FFT_J, FFT_J), slice(None))] = (
                    val[r0:r0 + FFT_J, ct * LANES:(ct + 1) * LANES])


def _load_tiles(ref, lead, row0, n_local, n_ct):
    return jnp.concatenate(
        [jnp.concatenate([ref[lead + (jh, ct, pl.ds(row0 + nl * FFT_J, FFT_J), slice(None))]
                          for nl in range(n_local) for jh in range(FFT_JH)], axis=0)
         for ct in range(n_ct)], axis=1)


def _pack_pair(re, im):
    hi = lax.bitcast_convert_type(re.astype(BF16).astype(F32), jnp.uint32)
    lo = lax.bitcast_convert_type(im.astype(BF16).astype(F32), jnp.uint32)
    return hi | (lo >> 16)


def _unpack_pair(w):
    re = lax.bitcast_convert_type(w & jnp.uint32(0xFFFF0000), F32)
    im = lax.bitcast_convert_type(w << 16, F32)
    return jnp.concatenate([re, im], axis=0).astype(BF16)


def _in_proj_kernel(x_ref, g_ref, wq_ref, w_ref, ek_ref, qg_ref, kg_ref, q_ref, kv_ref, ph_ref):
    h = _rms(x_ref[...], g_ref[...]).astype(BF16)
    tm = h.shape[0]

    qt = lax.dot_general(wq_ref[...], h, (((1,), (1,)), ((), ())), preferred_element_type=F32)
    qt = qt.reshape(N_Q_HEADS, HEAD_DIM, tm)
    qt = qt * lax.rsqrt(jnp.mean(qt * qt, axis=1, keepdims=True) + EPS)
    qt = (qt.reshape(D_ATTN, tm) * qg_ref[...]).astype(BF16)
    for j in range(tm // BLOCK):
        q_ref[j] = qt[:, j * BLOCK:(j + 1) * BLOCK]

    p = _dot(h, w_ref[...])
    k = p[:, :D_KV]
    ss = _dot((k * k).astype(BF16), ek_ref[...])
    kv_ref[:, :D_KV] = (k * lax.rsqrt(ss * (1.0 / HEAD_DIM) + EPS) * kg_ref[...]).astype(BF16)
    kv_ref[:, D_KV:] = p[:, D_KV:2 * D_KV].astype(BF16)
    ph_ref[...] = p[:, 2 * D_KV:]


def _in_proj(x2, g, wq_t, w, ek, qg, kg):
    T = x2.shape[0]
    tm = ROW_TILE
    row = lambda i: (i, 0)
    fix = lambda i: (0, 0)
    return pl.pallas_call(
        _in_proj_kernel,
        out_shape=[jax.ShapeDtypeStruct((T // BLOCK, D_ATTN, BLOCK), BF16),
                   jax.ShapeDtypeStruct((T, 2 * D_KV), BF16),
                   jax.ShapeDtypeStruct((T, D_PH), F32)],
        grid=(T // tm,),
        in_specs=[pl.BlockSpec((tm, D_MODEL), row),
                  pl.BlockSpec((1, D_MODEL), fix),
                  pl.BlockSpec((D_ATTN, D_MODEL), fix),
                  pl.BlockSpec((D_MODEL, 2 * D_KV + D_PH), fix),
                  pl.BlockSpec((D_KV, D_KV), fix),
                  pl.BlockSpec((D_ATTN, 1), fix),
                  pl.BlockSpec((1, D_KV), fix)],
        out_specs=[pl.BlockSpec((tm // BLOCK, D_ATTN, BLOCK), lambda i: (i, 0, 0)),
                   pl.BlockSpec((tm, 2 * D_KV), row),
                   pl.BlockSpec((tm, D_PH), row)],
        compiler_params=_params(1),
        name="in_proj",
    )(x2, g, wq_t, w, ek, qg, kg)


def _mix_kernel(sink_ref, *refs, seq_len):
    bias_refs = refs[:MIX_NB]
    (q_ref, kvp_ref, kvc_ref, kvn_ref, phc_ref, php_ref, phn_ref, pw_ref, ps_ref, hw_ref, hb_ref, ga_ref, gb_ref,
     cat_ref, hy_ref, ext_ref, att_ref, s_ref, p_ref, rden_ref) = refs[MIX_NB:]
    l = pl.program_id(1)
    is_first = l == 0
    is_last = l == pl.num_programs(1) - 1
    rows = MIX_NB * BLOCK

    tn = (((0,), (0,)), ((), ()))

    def keys(sb, cols):
        def blk(i):
            if i < 0:
                return kvp_ref[:, cols]
            if i >= MIX_NB:
                return kvn_ref[:, cols]
            return kvc_ref[i * BLOCK:(i + 1) * BLOCK, cols]
        return jnp.concatenate([blk(sb - 1), blk(sb), blk(sb + 1)], axis=0)

    pairs = [(sb, g) for sb in range(MIX_NB) for g in range(N_KV_HEADS)]
    gw = GQA_GROUP * BLOCK
    for i, (sb, g) in enumerate(pairs):
        qg = jnp.concatenate([q_ref[sb, h * HEAD_DIM:(h + 1) * HEAD_DIM, :]
                              for h in range(g * GQA_GROUP, (g + 1) * GQA_GROUP)], axis=1)
        st = _dot(keys(sb, slice(g * HEAD_DIM, (g + 1) * HEAD_DIM)), qg)
        for j in range(GQA_GROUP):
            s_ref[i, :, j * BLOCK:(j + 1) * BLOCK] = st[:, j * BLOCK:(j + 1) * BLOCK] + bias_refs[sb][g * GQA_GROUP + j]
    for i, (sb, g) in enumerate(pairs):
        for j in range(GQA_GROUP):
            cols = slice(j * BLOCK, (j + 1) * BLOCK)
            st = s_ref[i, :, cols]
            sk = sink_ref[g * GQA_GROUP + j]
            m = jnp.maximum(jnp.max(st, axis=0, keepdims=True), sk)
            e = jnp.exp(st - m)
            rden_ref[i, :, cols] = 1.0 / (jnp.sum(e, axis=0, keepdims=True) + jnp.exp(sk - m))
            p_ref[i, :, cols] = e.astype(BF16)
    for i, (sb, g) in enumerate(pairs):
        vh = keys(sb, slice(D_KV + g * HEAD_DIM, D_KV + (g + 1) * HEAD_DIM))
        ot = lax.dot_general(vh, p_ref[i], tn, preferred_element_type=F32) * rden_ref[i]
        for j in range(GQA_GROUP):
            h = g * GQA_GROUP + j
            att_ref[sb, h * HEAD_DIM:(h + 1) * HEAD_DIM, :] = ot[:, j * BLOCK:(j + 1) * BLOCK]
    for sb in range(MIX_NB):
        cat_ref[sb * BLOCK:(sb + 1) * BLOCK, :D_ATTN] = _rms(att_ref[sb].T, ga_ref[...]).astype(BF16)

    ext_ref[0:SUBLANES, :] = jnp.where(is_first, 0.0, php_ref[...])
    ext_ref[SUBLANES:SUBLANES + rows, :] = phc_ref[...]
    ext_ref[SUBLANES + rows:, :] = jnp.where(is_last, 0.0, phn_ref[...])

    t = l * rows + lax.broadcasted_iota(jnp.int32, (rows, 1), 0)
    lane = lax.broadcasted_iota(jnp.int32, (1, LANES), 1)

    def window_mean(w, col0):
        lo = jnp.clip(t - w // 2, 0, seq_len)
        hi = jnp.clip(t - w // 2 + w, 0, seq_len)
        tot = ext_ref[SUBLANES - w // 2:SUBLANES - w // 2 + rows, col0:col0 + LANES]
        for d in range(1 - w // 2, w // 2):
            tot = tot + ext_ref[SUBLANES + d:SUBLANES + d + rows, col0:col0 + LANES]
        return tot / (hi - lo).astype(F32)

    gpt = LANES // POOL_GROUP_DIM
    means = [jnp.where(lane < POOL_GROUP_DIM, window_mean(POOL_WINDOWS[gpt * i], i * LANES),
                       window_mean(POOL_WINDOWS[gpt * i + 1], i * LANES)) for i in range(D_POOL // LANES)]
    u = ext_ref[SUBLANES:SUBLANES + rows, 0:D_POOL]
    pooled = _dot((jnp.concatenate(means, axis=1) - u).astype(BF16), pw_ref[...]) * ps_ref[...]
    cat_ref[:, D_ATTN:] = _rms(pooled, gb_ref[...]).astype(BF16)

    hp = ext_ref[SUBLANES - 1:SUBLANES - 1 + rows, D_POOL:]
    hc = ext_ref[SUBLANES:SUBLANES + rows, D_POOL:]
    hn = ext_ref[SUBLANES + 1:SUBLANES + 1 + rows, D_POOL:]
    _store_tiles(hy_ref, (), 0, hp * hw_ref[0:1, :] + hc * hw_ref[1:2, :] + hn * hw_ref[2:3, :] + hb_ref[...])


def _attn_bias():
    row = jnp.arange(BLOCK, dtype=jnp.int32)[None, :]
    col = jnp.arange(3 * BLOCK, dtype=jnp.int32)[:, None]
    dist = jnp.abs(row + BLOCK - col)
    slopes = 2.0 ** (-8.0 * (jnp.arange(N_Q_HEADS, dtype=F32) + 1.0) / N_Q_HEADS)
    bias = -slopes[:, None, None] * dist.astype(F32)[None]
    out = []
    for case in range(4):
        valid = dist <= WINDOW
        if case & 1:
            valid = valid & (col >= BLOCK)
        if case & 2:
            valid = valid & (col < 2 * BLOCK)
        out.append(jnp.where(valid[None], bias, NEG_INF))
    return jnp.stack(out)


def _mix(q, kv, ph, sink, pw_bd, pool_scale, hy_w, hy_b, g_a, g_b, B, L):
    T = B * L
    rows = MIX_NB * BLOCK
    nb = L // BLOCK
    ns = L // rows
    rps = rows // SUBLANES
    n8 = T // SUBLANES
    cur = lambda b, l: (b * ns + l, 0)
    prev = lambda b, l: (b * nb + jnp.maximum(l * MIX_NB - 1, 0), 0)
    nxt = lambda b, l: (b * nb + jnp.minimum((l + 1) * MIX_NB, nb - 1), 0)
    prev8 = lambda b, l: (jnp.maximum((b * ns + l) * rps - 1, 0), 0)
    next8 = lambda b, l: (jnp.minimum((b * ns + l + 1) * rps, n8 - 1), 0)
    fix = lambda b, l: (0, 0)

    def bias_spec(sb):
        def index(b, l):
            case = jnp.int32(0)
            if sb == 0:
                case = case + (l == 0).astype(jnp.int32)
            if sb == MIX_NB - 1:
                case = case + 2 * (l == ns - 1).astype(jnp.int32)
            return (case, 0, 0, 0)
        return pl.BlockSpec((None, N_Q_HEADS, 3 * BLOCK, BLOCK), index)

    bias = _attn_bias()
    n_pairs = MIX_NB * N_KV_HEADS
    gw = GQA_GROUP * BLOCK
    return pl.pallas_call(
        functools.partial(_mix_kernel, seq_len=L),
        out_shape=[jax.ShapeDtypeStruct((T, D_AB), BF16),
                   jax.ShapeDtypeStruct((B, FFT_JH, D_HY_IN // LANES, nb * FFT_J, LANES), F32)],
        grid=(B, ns),
        in_specs=[pl.BlockSpec(memory_space=pltpu.SMEM)] + [bias_spec(sb) for sb in range(MIX_NB)] + [
                  pl.BlockSpec((MIX_NB, D_ATTN, BLOCK), lambda b, l: (b * ns + l, 0, 0)),
                  pl.BlockSpec((BLOCK, 2 * D_KV), prev),
                  pl.BlockSpec((rows, 2 * D_KV), cur),
                  pl.BlockSpec((BLOCK, 2 * D_KV), nxt),
                  pl.BlockSpec((rows, D_PH), cur),
                  pl.BlockSpec((SUBLANES, D_PH), prev8),
                  pl.BlockSpec((SUBLANES, D_PH), next8),
                  pl.BlockSpec((D_POOL, D_POOL), fix),
                  pl.BlockSpec((1, D_POOL), fix),
                  pl.BlockSpec((3, D_HY_IN), fix),
                  pl.BlockSpec((1, D_HY_IN), fix),
                  pl.BlockSpec((1, D_ATTN), fix),
                  pl.BlockSpec((1, D_POOL), fix)],
        out_specs=[pl.BlockSpec((rows, D_AB), cur),
                   pl.BlockSpec((None, FFT_JH, D_HY_IN // LANES, MIX_NB * FFT_J, LANES),
                                lambda b, l: (b, 0, 0, l, 0))],
        scratch_shapes=[pltpu.VMEM((rows + 2 * SUBLANES, D_PH), F32),
                        pltpu.VMEM((MIX_NB, D_ATTN, BLOCK), F32),
                        pltpu.VMEM((n_pairs, 3 * BLOCK, gw), F32),
                        pltpu.VMEM((n_pairs, 3 * BLOCK, gw), BF16),
                        pltpu.VMEM((n_pairs, 1, gw), F32)],
        compiler_params=_params(2),
        name="mix",
    )(sink, *([bias] * MIX_NB), q, kv, kv, kv, ph, ph, ph, pw_bd, pool_scale, hy_w, hy_b, g_a, g_b)


def _split3(x, axis):
    hi = x.astype(BF16)
    lo = (x - hi.astype(F32)).astype(BF16)
    return jnp.concatenate([hi, lo, hi], axis=axis)


def _stack3(w):
    hi = w.astype(BF16)
    lo = (w - hi.astype(F32)).astype(BF16)
    return jnp.concatenate([hi, hi, lo], axis=0)


def _filter_kernel(z_ref, t_ref, w1_ref, b1_ref, f1_ref, w2_ref, b2_ref, f2_ref, w3_ref, dl_ref, o_ref, *, seq_len):
    h = jnp.sin(f1_ref[...] * (_dot(w1_ref[...], z_ref[...]) + b1_ref[...]))
    h = jnp.sin(f2_ref[...] * (_dot(w2_ref[...], _split3(h, 0)) + b2_ref[...]))
    h = lax.dot_general(_split3(h, 0), w3_ref[...], (((0,), (0,)), ((), ())),
                        preferred_element_type=F32)
    n = pl.program_id(0) * h.shape[0] + lax.broadcasted_iota(jnp.int32, (h.shape[0], 1), 0)
    decay = jnp.where(n == seq_len, 0.0, jnp.exp(-t_ref[...] * dl_ref[...]))
    for o in range(HYENA_ORDER):
        _store_tiles(o_ref, (o,), 0, h[:, o * D_HYENA:(o + 1) * D_HYENA] * decay)


def _filter_kernels(L, w1, b1, fr1, w2, b2, fr2, w3):
    pos = jnp.arange(2 * L, dtype=jnp.int32)
    tpos = jnp.where(pos < L, pos, (2 * L - pos) % L).astype(F32)[:, None]
    t_norm = tpos * (1.0 / (L - 1))
    bands = jnp.linspace(1e-4, FILTER_BANDS - 1, FILTER_BANDS, dtype=F32)[None, :]
    ang = 2.0 * math.pi * tpos * bands / L
    z = jnp.concatenate([t_norm, jnp.cos(ang), -jnp.sin(ang)], axis=-1)
    z3 = jnp.pad(_split3(z, 1), ((0, 0), (0, LANES - 3 * FILTER_EMB))).T
    w1s = jnp.pad(_stack3(w1), ((0, LANES - 3 * FILTER_EMB), (0, 0)))
    w3d = w3.reshape(FILTER_HIDDEN, HYENA_ORDER, 2, D_HYENA).transpose(2, 0, 1, 3).reshape(
        2, FILTER_HIDDEN, HYENA_ORDER * D_HYENA)
    w3s = jnp.stack([_stack3(w3d[0]), _stack3(w3d[1])])
    max_decay = math.log(DECAY_TARGET) / DECAY_FAST_PCT
    min_decay = math.log(DECAY_TARGET) / DECAY_SLOW_PCT
    dl = jnp.abs(jnp.linspace(min_decay, max_decay, D_HYENA, dtype=F32))[None, :]
    tl = 512
    half = L // tl
    row = lambda i: (i, 0)
    fix = lambda i: (0, 0)
    return pl.pallas_call(
        functools.partial(_filter_kernel, seq_len=L),
        out_shape=jax.ShapeDtypeStruct((HYENA_ORDER, FFT_JH, HY_CT, 2 * L // FFT_N2 * FFT_J, LANES), F32),
        grid=(2 * L // tl,),
        in_specs=[pl.BlockSpec((LANES, tl), lambda i: (0, i)),
                  pl.BlockSpec((tl, 1), row),
                  pl.BlockSpec((FILTER_HIDDEN, LANES), fix),
                  pl.BlockSpec((FILTER_HIDDEN, 1), fix),
                  pl.BlockSpec((FILTER_HIDDEN, 1), fix),
                  pl.BlockSpec((FILTER_HIDDEN, 3 * FILTER_HIDDEN), fix),
                  pl.BlockSpec((FILTER_HIDDEN, 1), fix),
                  pl.BlockSpec((FILTER_HIDDEN, 1), fix),
                  pl.BlockSpec((None, 3 * FILTER_HIDDEN, HYENA_ORDER * D_HYENA), lambda i: (i // half, 0, 0)),
                  pl.BlockSpec((1, D_HYENA), fix)],
        out_specs=pl.BlockSpec((HYENA_ORDER, FFT_JH, HY_CT, tl // FFT_N2 * FFT_J, LANES), lambda i: (0, 0, 0, i, 0)),
        compiler_params=_params(1),
        name="hyena_filters",
    )(z3, t_norm, w1s.T, b1[:, None], fr1[:, None], _stack3(w2).T, b2[:, None], fr2[:, None], w3s, dl)


def _dft_mats(L):
    N = 2 * L
    n1n = N // FFT_N2
    r = n1n // 2
    assert r % FFT_TF == 0, "sequence length must be a multiple of FFT_TF * FFT_N2 / 2"
    mh = r + FFT_TF
    f1 = jnp.arange(mh, dtype=jnp.int32)[:, None]
    n1 = jnp.arange(n1n, dtype=jnp.int32)[None, :]
    keep = (f1 <= r).astype(F32)
    th = (2.0 * math.pi / n1n) * ((f1 * n1) % n1n).astype(F32)
    c1, s1 = keep * jnp.cos(th), keep * jnp.sin(th)
    fwd_full = jnp.concatenate([c1, -s1], axis=0).astype(BF16)
    fwd_half = fwd_full[:, :r]
    w = jnp.where((f1 == 0) | (f1 == r), 1.0, 2.0)
    inv_half = jnp.concatenate([(w * c1)[:, :r].T, (-w * s1)[:, :r].T], axis=1).astype(BF16)
    i2 = jnp.arange(2 * FFT_N2, dtype=jnp.int32)
    part, idx = i2 // FFT_N2, i2 % FFT_N2
    pa = (2.0 * math.pi / N) * ((jnp.arange(mh, dtype=jnp.int32)[:, None] * idx[None, :]) % N).astype(F32)
    ca, sa = jnp.cos(pa), jnp.sin(pa)
    mb = (idx[:, None] * idx[None, :] * n1n + (part[:, None] - part[None, :]) * (N // 4)) % N
    pb = (2.0 * math.pi / N) * mb.astype(F32)
    cb, sb = jnp.cos(pb), jnp.sin(pb)
    g = (ca[:, None, :] * cb[None] - sa[:, None, :] * sb[None]).astype(BF16)
    ginv = (ca[:, :, None] * cb.T[None] - sa[:, :, None] * sb.T[None]).astype(BF16)
    return dict(n1=n1n, r=r, mh=mh, fwd_full=fwd_full, fwd_half=fwd_half, inv_half=inv_half, g=g, ginv=ginv)


def _outer_blocks(tb):
    return min(4, max(1, 8 // tb))


def _fft_a_kernel(x_ref, f_ref, a_ref, *, mh):
    f = f_ref[...]
    r = x_ref.shape[3] // FFT_J
    for lead in [(b, jb, ct) for b in range(x_ref.shape[0]) for jb in range(x_ref.shape[1]) for ct in range(HY_CT)]:
        for j in range(FFT_J):
            a = _dot(f, x_ref[lead + (pl.ds(j, r, stride=FFT_J), slice(None))].astype(BF16))
            a_ref[lead + (pl.ds(j, mh, stride=FFT_J), slice(None))] = _pack_pair(a[:mh], a[mh:])


def _fft_a(x5, coff, fmat, mh, tb):
    B, rj = x5.shape[0], x5.shape[3]
    jb = _outer_blocks(tb)
    return pl.pallas_call(
        functools.partial(_fft_a_kernel, mh=mh),
        out_shape=jax.ShapeDtypeStruct((B, FFT_JH, HY_CT, mh * FFT_J, LANES), jnp.uint32),
        grid=(B // tb, FFT_JH // jb),
        in_specs=[pl.BlockSpec((tb, jb, HY_CT, rj, LANES), lambda b, j: (b, j, coff // HY_CT, 0, 0)),
                  pl.BlockSpec((2 * mh, rj // FFT_J), lambda b, j: (0, 0))],
        out_specs=pl.BlockSpec((tb, jb, HY_CT, mh * FFT_J, LANES), lambda b, j: (b, j, 0, 0, 0)),
        compiler_params=_params(2),
        name="fft_outer_fwd",
    )(x5, fmat)


def _fft_spec_kernel(a_ref, g_ref, kr_ref, ki_ref, *, scale):
    for f in range(FFT_TF):
        x = _dot(g_ref[f], _unpack_pair(_load_tiles(a_ref, (), f * FFT_J, 1, HY_CT))) * scale
        kr_ref[0, f] = x[:FFT_N2]
        ki_ref[0, f] = x[FFT_N2:]


def _tile_spec():
    return pl.BlockSpec((None, FFT_JH, HY_CT, FFT_TF * FFT_J, LANES), lambda f, b: (b, 0, 0, f, 0))


def _fft_spec(a, g, scale):
    B, mh = a.shape[0], a.shape[3] // FFT_J
    nat = pl.BlockSpec((1, FFT_TF, FFT_N2, D_HYENA), lambda f, b: (b, f, 0, 0))
    out = jax.ShapeDtypeStruct((B, mh, FFT_N2, D_HYENA), F32)
    return pl.pallas_call(
        functools.partial(_fft_spec_kernel, scale=scale),
        out_shape=[out, out],
        grid=(mh // FFT_TF, B),
        in_specs=[_tile_spec(), pl.BlockSpec((FFT_TF, 2 * FFT_N2, 2 * FFT_N2), lambda f, b: (f, 0, 0))],
        out_specs=[nat, nat],
        compiler_params=_params(2),
        name="fft_inner_fwd",
    )(a, g)


def _fft_b_kernel(a_ref, g_ref, gi_ref, kr_ref, ki_ref, y_ref):
    for b in range(a_ref.shape[0]):
        for f in range(FFT_TF):
            x = _dot(g_ref[f], _unpack_pair(_load_tiles(a_ref, (b,), f * FFT_J, 1, HY_CT)))
            xr, xi = x[:FFT_N2], x[FFT_N2:]
            kr, ki = kr_ref[0, f], ki_ref[0, f]
            z = jnp.concatenate([xr * kr - xi * ki, xr * ki + xi * kr], axis=0).astype(BF16)
            y = _dot(gi_ref[f], z)
            _store_tiles(y_ref, (b,), f * FFT_J, _pack_pair(y[:FFT_N2], y[FFT_N2:]))


def _fft_b(a, g, ginv, kr, ki, o):
    B, mh = a.shape[0], a.shape[3] // FFT_J
    tb = math.gcd(B, 4)
    dat = pl.BlockSpec((tb, FFT_JH, HY_CT, FFT_TF * FFT_J, LANES), lambda f, b: (b, 0, 0, f, 0))
    mat = pl.BlockSpec((FFT_TF, 2 * FFT_N2, 2 * FFT_N2), lambda f, b: (f, 0, 0))
    spec = pl.BlockSpec((1, FFT_TF, FFT_N2, D_HYENA), lambda f, b: (o, f, 0, 0))
    return pl.pallas_call(
        _fft_b_kernel,
        out_shape=jax.ShapeDtypeStruct(a.shape, jnp.uint32),
        grid=(mh // FFT_TF, B // tb),
        in_specs=[dat, mat, mat, spec, spec],
        out_specs=dat,
        compiler_params=_params(2),
        name="fft_inner",
    )(a, g, ginv, kr, ki)


def _fft_c_kernel(y_ref, c_ref, u_ref, gate_ref, bias_ref, o_ref):
    cm = c_ref[...]
    mh, r = y_ref.shape[3] // FFT_J, u_ref.shape[3] // FFT_J
    for lead in [(b, jb, ct) for b in range(y_ref.shape[0]) for jb in range(y_ref.shape[1]) for ct in range(HY_CT)]:
        bias = bias_ref[:, lead[2] * LANES:(lead[2] + 1) * LANES]
        for j in range(FFT_J):
            y = _dot(cm, _unpack_pair(y_ref[lead + (pl.ds(j, mh, stride=FFT_J), slice(None))]))
            rows = lead + (pl.ds(j, r, stride=FFT_J), slice(None))
            o_ref[rows] = gate_ref[rows] * (y + bias * u_ref[rows])


def _fft_c(y, cmat, u5, uoff, gate5, goff, bias, tb):
    B, mhj = y.shape[0], y.shape[3]
    rj = cmat.shape[0] * FFT_J
    jb = _outer_blocks(tb)
    blk = lambda b, j: (b, j, 0, 0, 0)
    return pl.pallas_call(
        _fft_c_kernel,
        out_shape=jax.ShapeDtypeStruct((B, FFT_JH, HY_CT, rj, LANES), F32),
        grid=(B // tb, FFT_JH // jb),
        in_specs=[pl.BlockSpec((tb, jb, HY_CT, mhj, LANES), blk),
                  pl.BlockSpec(cmat.shape, lambda b, j: (0, 0)),
                  pl.BlockSpec((tb, jb, HY_CT, rj, LANES), lambda b, j: (b, j, uoff // HY_CT, 0, 0)),
                  pl.BlockSpec((tb, jb, HY_CT, rj, LANES), lambda b, j: (b, j, goff // HY_CT, 0, 0)),
                  pl.BlockSpec((1, D_HYENA), lambda b, j: (0, 0))],
        out_specs=pl.BlockSpec((tb, jb, HY_CT, rj, LANES), blk),
        compiler_params=_params(2),
        name="fft_outer_inv",
    )(y, cmat, u5, gate5, bias)


def _filter_spectra(kern, mats, L):
    return _fft_spec(_fft_a(kern, 0, mats["fwd_full"], mats["mh"], 1), mats["g"], 1.0 / (2 * L))


def _hyena(hy5, kr, ki, hy_bias, mats, tb):
    z5 = hy5
    for o in range(HYENA_ORDER):
        a = _fft_a(z5, 0, mats["fwd_half"], mats["mh"], tb)
        y = _fft_b(a, mats["g"], mats["ginv"], kr, ki, o)
        z5 = _fft_c(y, mats["inv_half"], z5, 0, hy5, (o + 1) * HY_CT, hy_bias[o][None, :], tb)
    return z5


def _out_proj_kernel(x_ref, cab_ref, c_ref, gc_ref, wab_ref, wc_ref, xn_ref):
    c = _load_tiles(c_ref, (), 0, c_ref.shape[2] // FFT_J, HY_CT)
    cn = _rms(c, gc_ref[...]).astype(BF16)
    xn_ref[...] = x_ref[...] + _dot(cab_ref[...], wab_ref[...]) + _dot(cn, wc_ref[...])


def _out_proj(x2, cab, c5, g_c, w_ab, w_c):
    T = x2.shape[0]
    tm = ROW_TILE
    tps = c5.shape[3] // FFT_J * FFT_N2 // tm
    row = lambda i: (i, 0)
    fix = lambda i: (0, 0)
    return pl.pallas_call(
        _out_proj_kernel,
        out_shape=jax.ShapeDtypeStruct((T, D_MODEL), F32),
        grid=(T // tm,),
        in_specs=[pl.BlockSpec((tm, D_MODEL), row),
                  pl.BlockSpec((tm, D_AB), row),
                  pl.BlockSpec((None, FFT_JH, HY_CT, tm // FFT_N2 * FFT_J, LANES),
                               lambda i: (i // tps, 0, 0, i % tps, 0)),
                  pl.BlockSpec((1, D_HYENA), fix),
                  pl.BlockSpec((D_AB, D_MODEL), fix),
                  pl.BlockSpec((D_HYENA, D_MODEL), fix)],
        out_specs=pl.BlockSpec((tm, D_MODEL), row),
        compiler_params=_params(1),
        name="out_proj",
    )(x2, cab, c5, g_c, w_ab, w_c)


def _ffn_kernel(xn_ref, xp_ref, xq_ref, g2_ref, wi_ref, cw_ref, cb_ref, wo_ref, o_ref,
                acc_ref, hx_ref, ug_ref, uv_ref, *, tiles_per_seq):
    i = pl.program_id(0)
    is_first = (i % tiles_per_seq) == 0
    is_last = (i % tiles_per_seq) == tiles_per_seq - 1
    tm = xn_ref.shape[0]
    row = lax.broadcasted_iota(jnp.int32, (tm, 1), 0)
    hx_ref[0:tm, :] = _rms(xn_ref[...], g2_ref[...]).astype(BF16)
    hx_ref[tm:tm + BF16_ROWS, :] = _rms(xp_ref[...], g2_ref[...]).astype(BF16)
    hx_ref[tm + BF16_ROWS:, :] = _rms(xq_ref[...], g2_ref[...]).astype(BF16)

    def up(j):
        ug_ref[j % 2] = _dot(hx_ref[...], wi_ref[j])
        uv_ref[j % 2] = _dot(hx_ref[...], wi_ref[FFN_CHUNKS + j])

    def conv(u_ref, j, k):
        uc = u_ref[j % 2, 0:tm, :]
        before = jnp.where(is_first, 0.0, u_ref[j % 2, tm + BF16_ROWS - 1:tm + BF16_ROWS, :])
        after = jnp.where(is_last, 0.0, u_ref[j % 2, tm + BF16_ROWS:tm + BF16_ROWS + 1, :])
        um = jnp.where(row == 0, before, pltpu.roll(uc, 1, axis=0))
        un = jnp.where(row == tm - 1, after, pltpu.roll(uc, tm - 1, axis=0))
        w = cw_ref[k]
        return um * w[0:1] + uc * w[1:2] + un * w[2:3] + cb_ref[k]

    def down(j):
        gate = conv(ug_ref, j, j)
        half_val = conv(uv_ref, j, FFN_CHUNKS + j)
        act = gate * (1.0 + lax.erf(gate * math.sqrt(0.5))) * half_val
        acc_ref[...] += _dot(act.astype(BF16), wo_ref[j])

    acc_ref[...] = xn_ref[...]
    up(0)
    for j in range(FFN_CHUNKS):
        if j + 1 < FFN_CHUNKS:
            up(j + 1)
        down(j)
    o_ref[...] = acc_ref[...]


def _ffn(xn, g2, w_in, cw, cb, w_out, L):
    T = xn.shape[0]
    tm = ROW_TILE
    ch = D_FF // FFN_CHUNKS
    hpt = tm // BF16_ROWS
    n16 = T // BF16_ROWS
    wi = w_in.reshape(D_MODEL, 2 * FFN_CHUNKS, ch).transpose(1, 0, 2)
    cwc = cw.reshape(3, 2 * FFN_CHUNKS, ch).transpose(1, 0, 2)
    cbc = cb.reshape(2 * FFN_CHUNKS, 1, ch)
    wo = w_out.reshape(FFN_CHUNKS, ch, D_MODEL)
    row = lambda i: (i, 0)
    fix3 = lambda i: (0, 0, 0)
    once = pl.Buffered(1)
    return pl.pallas_call(
        functools.partial(_ffn_kernel, tiles_per_seq=L // tm),
        out_shape=jax.ShapeDtypeStruct((T, D_MODEL), F32),
        grid=(T // tm,),
        in_specs=[pl.BlockSpec((tm, D_MODEL), row),
                  pl.BlockSpec((BF16_ROWS, D_MODEL), lambda i: (jnp.maximum(i * hpt - 1, 0), 0)),
                  pl.BlockSpec((BF16_ROWS, D_MODEL), lambda i: (jnp.minimum((i + 1) * hpt, n16 - 1), 0)),
                  pl.BlockSpec((1, D_MODEL), lambda i: (0, 0)),
                  pl.BlockSpec((2 * FFN_CHUNKS, D_MODEL, ch), fix3, pipeline_mode=once),
                  pl.BlockSpec((2 * FFN_CHUNKS, 3, ch), fix3),
                  pl.BlockSpec((2 * FFN_CHUNKS, 1, ch), fix3),
                  pl.BlockSpec((FFN_CHUNKS, ch, D_MODEL), fix3, pipeline_mode=once)],
        out_specs=pl.BlockSpec((tm, D_MODEL), row),
        scratch_shapes=[pltpu.VMEM((tm, D_MODEL), F32),
                        pltpu.VMEM((tm + 2 * BF16_ROWS, D_MODEL), BF16),
                        pltpu.VMEM((2, tm + 2 * BF16_ROWS, ch), F32),
                        pltpu.VMEM((2, tm + 2 * BF16_ROWS, ch), F32)],
        compiler_params=_params(1),
        name="ffn",
    )(xn, xn, xn, g2, wi, cwc, cbc, wo)


def _head_ones(n_heads):
    return jnp.kron(jnp.eye(n_heads, dtype=F32), jnp.ones((HEAD_DIM, HEAD_DIM), F32)).astype(BF16)


def _layer(x2, B, L, mats, spectra, lw):
    q, kv, ph = _in_proj(x2, lw["norm1_g"], lw["wq_t"], lw["w_in"], lw["ek"], lw["qg"], lw["kg"])
    cab, hy = _mix(q, kv, ph, lw["sink"], lw["pw_bd"], lw["pool_scale"], lw["hy_w"], lw["hy_b"],
                   lw["g_a"], lw["g_b"], B, L)
    tb = math.gcd(B, 8)
    c5 = _hyena(hy, spectra[0], spectra[1], lw["hy_bias"], mats, tb)
    xn = _out_proj(x2, cab, c5, lw["g_c"], lw["w_ab"], lw["w_c"])
    return _ffn(xn, lw["norm2_g"], lw["w_ffn_in"], lw["ffn_cw"], lw["ffn_cb"], lw["w_ffn_out"], L)


def kernel(x_prompt, x_sample, norm1_g, w_in, q_norm_g, k_norm_g, attn_sink, pool_w, pool_scale, hy_conv_w,
           hy_conv_b, filt_w1, filt_b1, filt_freq1, filt_w2, filt_b2, filt_freq2, filt_w3, hy_bias, out_norm_g,
           w_out, norm2_g, w_ffn_in, ffn_conv_w, ffn_conv_b, w_ffn_out):
    groups = [x_prompt, x_sample]
    lens = sorted({g.shape[1] for g in groups})
    mats = {L: _dft_mats(L) for L in lens}
    ek = _head_ones(N_KV_HEADS)
    xs = [g.reshape(-1, D_MODEL) for g in groups]
    glu_scale = jnp.concatenate([jnp.ones((D_FF,), F32), jnp.full((D_FF,), 0.5, F32)])[None, :]
    for l in range(DEPTH):
        lw = dict(
            norm1_g=norm1_g[l][None, :], wq_t=w_in[l, :, :D_ATTN].T.astype(BF16), w_in=w_in[l, :, D_ATTN:].astype(BF16),
            ek=ek, kg=jnp.tile(k_norm_g[l], N_KV_HEADS)[None, :],
            qg=(jnp.tile(q_norm_g[l], N_Q_HEADS) * (1.0 / math.sqrt(HEAD_DIM)))[:, None],
            sink=attn_sink[l],
            pw_bd=jax.scipy.linalg.block_diag(*[pool_w[l, g] for g in range(len(POOL_WINDOWS))]).astype(BF16),
            pool_scale=pool_scale[l][None, :], hy_w=hy_conv_w[l], hy_b=hy_conv_b[l][None, :],
            g_a=out_norm_g[l, :D_ATTN][None, :], g_b=out_norm_g[l, D_ATTN:D_AB][None, :],
            g_c=out_norm_g[l, D_AB:][None, :], hy_bias=hy_bias[l],
            w_ab=w_out[l, :D_AB].astype(BF16), w_c=w_out[l, D_AB:].astype(BF16),
            norm2_g=norm2_g[l][None, :], w_ffn_in=w_ffn_in[l].astype(BF16),
            ffn_cw=ffn_conv_w[l] * glu_scale, ffn_cb=(ffn_conv_b[l] * glu_scale[0])[None, :],
            w_ffn_out=w_ffn_out[l].astype(BF16),
        )
        spectra = {}
        for L in lens:
            kern = _filter_kernels(L, filt_w1[l], filt_b1[l], filt_freq1[l], filt_w2[l], filt_b2[l], filt_freq2[l],
                                   filt_w3[l])
            spectra[L] = _filter_spectra(kern, mats[L], L)
        xs = [_layer(x2, g.shape[0], g.shape[1], mats[g.shape[1]], spectra[g.shape[1]], lw)
              for x2, g in zip(xs, groups)]
    return tuple(x2.reshape(g.shape) for x2, g in zip(xs, groups))
```

```python
import functools
import math

import jax
import jax.numpy as jnp
from jax import lax
from jax.experimental import pallas as pl
from jax.experimental.pallas import tpu as pltpu

F32 = jnp.float32
BF16 = jnp.bfloat16

D_MODEL = 1024
DEPTH = 2
HEAD_DIM = 64
N_Q_HEADS = 8
N_KV_HEADS = 2
GQA_GROUP = N_Q_HEADS // N_KV_HEADS
D_ATTN = N_Q_HEADS * HEAD_DIM
D_KV = N_KV_HEADS * HEAD_DIM
WINDOW = 128
BLOCK = 128
POOL_WINDOWS = (2, 4, 8, 16)
D_POOL = 256
POOL_GROUP_DIM = D_POOL // len(POOL_WINDOWS)
D_HYENA = 256
HYENA_ORDER = 2
FILTER_BANDS = 16
FILTER_EMB = 1 + 2 * FILTER_BANDS
FILTER_HIDDEN = 64
N_FILTERS = 2 * HYENA_ORDER
DECAY_FAST_PCT = 0.3
DECAY_SLOW_PCT = 1.5
DECAY_TARGET = 1e-2
D_HY_IN = (HYENA_ORDER + 1) * D_HYENA
D_PH = D_POOL + D_HY_IN
D_IN_PROJ = D_ATTN + 2 * D_KV + D_PH
D_CAT = D_ATTN + D_POOL + D_HYENA
D_AB = D_ATTN + D_POOL
D_FF = 2816
EPS = 1e-6
NEG_INF = -1e30
LOG2E = math.log2(math.e)

LANES = 128
SUBLANES = 8
FFT_N2 = 128
FFT_J = SUBLANES
FFT_TF = 8
FFT_JH = FFT_N2 // FFT_J
HY_CT = D_HYENA // LANES
FFN_CHUNKS = 2
BF16_ROWS = 16
ROW_TILE = 512
MIX_NB = 2
VMEM_LIMIT = 56 * 1024 * 1024


def _params(n_axes):
    return pltpu.CompilerParams(dimension_semantics=("arbitrary",) * n_axes, vmem_limit_bytes=VMEM_LIMIT)


def _rms(x, g):
    return x * lax.rsqrt(jnp.mean(x * x, axis=-1, keepdims=True) + EPS) * g


def _dot(a, b):
    return jnp.dot(a, b, preferred_element_type=F32)


def _store_tiles(ref, lead, row0, val):
    for nl in range(val.shape[0] // FFT_N2):
        for jh in range(FFT_JH):
            for ct in range(val.shape[1] // LANES):
                r0 = nl * FFT_N2 + jh * FFT_J
                ref[lead + (jh, ct, pl.ds(row0 + nl * FFT_J, FFT_J), slice(None))] = (
                    val[r0:r0 + FFT_J, ct * LANES:(ct + 1) * LANES])


def _load_tiles(ref, lead, row0, n_local, n_ct):
    return jnp.concatenate(
        [jnp.concatenate([ref[lead + (jh, ct, pl.ds(row0 + nl * FFT_J, FFT_J), slice(None))]
                          for nl in range(n_local) for jh in range(FFT_JH)], axis=0)
         for ct in range(n_ct)], axis=1)


def _pack_pair(re, im):
    hi = lax.bitcast_convert_type(re.astype(BF16).astype(F32), jnp.uint32)
    lo = lax.bitcast_convert_type(im.astype(BF16).astype(F32), jnp.uint32)
    return hi | (lo >> 16)


def _unpack_pair(w):
    re = lax.bitcast_convert_type(w & jnp.uint32(0xFFFF0000), F32)
    im = lax.bitcast_convert_type(w << 16, F32)
    return jnp.concatenate([re, im], axis=0).astype(BF16)


def _in_proj_kernel(x_ref, g_ref, wq_ref, w_ref, ek_ref, qg_ref, kg_ref, q_ref, kv_ref, ph_ref):
    h = _rms(x_ref[...], g_ref[...]).astype(BF16)
    tm = h.shape[0]

    qt = lax.dot_general(wq_ref[...], h, (((1,), (1,)), ((), ())), preferred_element_type=F32)
    qt = qt.reshape(N_Q_HEADS, HEAD_DIM, tm)
    qt = qt * lax.rsqrt(jnp.mean(qt * qt, axis=1, keepdims=True) + EPS)
    qt = (qt.reshape(D_ATTN, tm) * qg_ref[...]).astype(BF16)
    for j in range(tm // BLOCK):
        q_ref[j] = qt[:, j * BLOCK:(j + 1) * BLOCK]

    p = _dot(h, w_ref[...])
    k = p[:, :D_KV]
    ss = _dot((k * k).astype(BF16), ek_ref[...])
    kv_ref[:, :D_KV] = (k * lax.rsqrt(ss * (1.0 / HEAD_DIM) + EPS) * kg_ref[...]).astype(BF16)
    kv_ref[:, D_KV:] = p[:, D_KV:2 * D_KV].astype(BF16)
    ph_ref[...] = p[:, 2 * D_KV:]


def _in_proj(x2, g, wq_t, w, ek, qg, kg):
    T = x2.shape[0]
    tm = ROW_TILE
    row = lambda i: (i, 0)
    fix = lambda i: (0, 0)
    return pl.pallas_call(
        _in_proj_kernel,
        out_shape=[jax.ShapeDtypeStruct((T // BLOCK, D_ATTN, BLOCK), BF16),
                   jax.ShapeDtypeStruct((T, 2 * D_KV), BF16),
                   jax.ShapeDtypeStruct((T, D_PH), F32)],
        grid=(T // tm,),
        in_specs=[pl.BlockSpec((tm, D_MODEL), row),
                  pl.BlockSpec((1, D_MODEL), fix),
                  pl.BlockSpec((D_ATTN, D_MODEL), fix),
                  pl.BlockSpec((D_MODEL, 2 * D_KV + D_PH), fix),
                  pl.BlockSpec((D_KV, D_KV), fix),
                  pl.BlockSpec((D_ATTN, 1), fix),
                  pl.BlockSpec((1, D_KV), fix)],
        out_specs=[pl.BlockSpec((tm // BLOCK, D_ATTN, BLOCK), lambda i: (i, 0, 0)),
                   pl.BlockSpec((tm, 2 * D_KV), row),
                   pl.BlockSpec((tm, D_PH), row)],
        compiler_params=_params(1),
        name="in_proj",
    )(x2, g, wq_t, w, ek, qg, kg)


def _mix_kernel(sink_ref, *refs, seq_len):
    bias_refs = refs[:MIX_NB]
    (q_ref, kvp_ref, kvc_ref, kvn_ref, phc_ref, php_ref, phn_ref, pw_ref, ps_ref, hw_ref, hb_ref, ga_ref, gb_ref,
     cat_ref, hy_ref, att_ref, s_ref, p_ref, rden_ref) = refs[MIX_NB:]
    l = pl.program_id(1)
    is_first = l == 0
    is_last = l == pl.num_programs(1) - 1
    rows = MIX_NB * BLOCK

    tn = (((0,), (0,)), ((), ()))

    def keys(sb, cols):
        def blk(i):
            if i < 0:
                return kvp_ref[:, cols]
            if i >= MIX_NB:
                return kvn_ref[:, cols]
            return kvc_ref[i * BLOCK:(i + 1) * BLOCK, cols]
        return jnp.concatenate([blk(sb - 1), blk(sb), blk(sb + 1)], axis=0)

    pairs = [(sb, g) for sb in range(MIX_NB) for g in range(N_KV_HEADS)]
    gw = GQA_GROUP * BLOCK
    for i, (sb, g) in enumerate(pairs):
        qg = jnp.concatenate([q_ref[sb, h * HEAD_DIM:(h + 1) * HEAD_DIM, :]
                              for h in range(g * GQA_GROUP, (g + 1) * GQA_GROUP)], axis=1)
        st = _dot(keys(sb, slice(g * HEAD_DIM, (g + 1) * HEAD_DIM)), qg)
        for j in range(GQA_GROUP):
            s_ref[i, :, j * BLOCK:(j + 1) * BLOCK] = st[:, j * BLOCK:(j + 1) * BLOCK] + bias_refs[sb][g * GQA_GROUP + j]
    for i, (sb, g) in enumerate(pairs):
        for j in range(GQA_GROUP):
            cols = slice(j * BLOCK, (j + 1) * BLOCK)
            st = s_ref[i, :, cols]
            sk = sink_ref[g * GQA_GROUP + j]
            m = jnp.maximum(jnp.max(st, axis=0, keepdims=True), sk)
            e = jnp.exp2(st - m)
            rden_ref[i, :, cols] = 1.0 / (jnp.sum(e, axis=0, keepdims=True) + jnp.exp2(sk - m))
            p_ref[i, :, cols] = e.astype(BF16)
    for i, (sb, g) in enumerate(pairs):
        vh = keys(sb, slice(D_KV + g * HEAD_DIM, D_KV + (g + 1) * HEAD_DIM))
        ot = lax.dot_general(vh, p_ref[i], tn, preferred_element_type=F32) * rden_ref[i]
        for j in range(GQA_GROUP):
            h = g * GQA_GROUP + j
            att_ref[sb, h * HEAD_DIM:(h + 1) * HEAD_DIM, :] = ot[:, j * BLOCK:(j + 1) * BLOCK]
    for sb in range(MIX_NB):
        cat_ref[sb * BLOCK:(sb + 1) * BLOCK, :D_ATTN] = _rms(att_ref[sb].T, ga_ref[...]).astype(BF16)

    def halo_ext(cols):
        return jnp.concatenate([jnp.where(is_first, 0.0, php_ref[:, cols]), phc_ref[:, cols],
                                jnp.where(is_last, 0.0, phn_ref[:, cols])], axis=0)

    t = l * rows + lax.broadcasted_iota(jnp.int32, (rows, 1), 0)
    lane = lax.broadcasted_iota(jnp.int32, (1, LANES), 1)

    def count(w):
        lo = jnp.clip(t - w // 2, 0, seq_len)
        hi = jnp.clip(t - w // 2 + w, 0, seq_len)
        return (hi - lo).astype(F32)

    def window_sums(col0, widths):
        n = rows + 2 * SUBLANES
        p, have, out = halo_ext(slice(col0, col0 + LANES)), 1, []
        for w in widths:
            while have < w:
                p = p + pltpu.roll(p, n - have, axis=0)
                have *= 2
            start = SUBLANES - w // 2
            out.append((p if start == 0 else pltpu.roll(p, n - start, axis=0))[0:rows])
        return out

    gpt = LANES // POOL_GROUP_DIM
    means = []
    for i in range(D_POOL // LANES):
        wa, wb = POOL_WINDOWS[gpt * i], POOL_WINDOWS[gpt * i + 1]
        sa, sb = window_sums(i * LANES, (wa, wb))
        means.append(jnp.where(lane < POOL_GROUP_DIM, sa / count(wa), sb / count(wb)))
    u = phc_ref[:, 0:D_POOL]
    pooled = _dot((jnp.concatenate(means, axis=1) - u).astype(BF16), pw_ref[...]) * ps_ref[...]
    cat_ref[:, D_ATTN:] = _rms(pooled, gb_ref[...]).astype(BF16)

    ridx = lax.broadcasted_iota(jnp.int32, (rows, 1), 0)
    hc = phc_ref[:, D_POOL:]
    hp = jnp.where(ridx == 0, jnp.where(is_first, 0.0, php_ref[SUBLANES - 1:SUBLANES, D_POOL:]),
                   pltpu.roll(hc, 1, axis=0))
    hn = jnp.where(ridx == rows - 1, jnp.where(is_last, 0.0, phn_ref[0:1, D_POOL:]),
                   pltpu.roll(hc, rows - 1, axis=0))
    _store_tiles(hy_ref, (), 0, hp * hw_ref[0:1, :] + hc * hw_ref[1:2, :] + hn * hw_ref[2:3, :] + hb_ref[...])


def _attn_bias():
    row = jnp.arange(BLOCK, dtype=jnp.int32)[None, :]
    col = jnp.arange(3 * BLOCK, dtype=jnp.int32)[:, None]
    dist = jnp.abs(row + BLOCK - col)
    slopes = 2.0 ** (-8.0 * (jnp.arange(N_Q_HEADS, dtype=F32) + 1.0) / N_Q_HEADS)
    bias = -(LOG2E * slopes)[:, None, None] * dist.astype(F32)[None]
    out = []
    for case in range(4):
        valid = dist <= WINDOW
        if case & 1:
            valid = valid & (col >= BLOCK)
        if case & 2:
            valid = valid & (col < 2 * BLOCK)
        out.append(jnp.where(valid[None], bias, NEG_INF))
    return jnp.stack(out)


def _mix(q, kv, ph, sink, pw_bd, pool_scale, hy_w, hy_b, g_a, g_b, B, L):
    T = B * L
    rows = MIX_NB * BLOCK
    nb = L // BLOCK
    ns = L // rows
    rps = rows // SUBLANES
    n8 = T // SUBLANES
    cur = lambda b, l: (b * ns + l, 0)
    prev = lambda b, l: (b * nb + jnp.maximum(l * MIX_NB - 1, 0), 0)
    nxt = lambda b, l: (b * nb + jnp.minimum((l + 1) * MIX_NB, nb - 1), 0)
    prev8 = lambda b, l: (jnp.maximum((b * ns + l) * rps - 1, 0), 0)
    next8 = lambda b, l: (jnp.minimum((b * ns + l + 1) * rps, n8 - 1), 0)
    fix = lambda b, l: (0, 0)

    def bias_spec(sb):
        def index(b, l):
            case = jnp.int32(0)
            if sb == 0:
                case = case + (l == 0).astype(jnp.int32)
            if sb == MIX_NB - 1:
                case = case + 2 * (l == ns - 1).astype(jnp.int32)
            return (case, 0, 0, 0)
        return pl.BlockSpec((None, N_Q_HEADS, 3 * BLOCK, BLOCK), index)

    bias = _attn_bias()
    n_pairs = MIX_NB * N_KV_HEADS
    gw = GQA_GROUP * BLOCK
    return pl.pallas_call(
        functools.partial(_mix_kernel, seq_len=L),
        out_shape=[jax.ShapeDtypeStruct((T, D_AB), BF16),
                   jax.ShapeDtypeStruct((B, FFT_JH, D_HY_IN // LANES, nb * FFT_J, LANES), F32)],
        grid=(B, ns),
        in_specs=[pl.BlockSpec(memory_space=pltpu.SMEM)] + [bias_spec(sb) for sb in range(MIX_NB)] + [
                  pl.BlockSpec((MIX_NB, D_ATTN, BLOCK), lambda b, l: (b * ns + l, 0, 0)),
                  pl.BlockSpec((BLOCK, 2 * D_KV), prev),
                  pl.BlockSpec((rows, 2 * D_KV), cur),
                  pl.BlockSpec((BLOCK, 2 * D_KV), nxt),
                  pl.BlockSpec((rows, D_PH), cur),
                  pl.BlockSpec((SUBLANES, D_PH), prev8),
                  pl.BlockSpec((SUBLANES, D_PH), next8),
                  pl.BlockSpec((D_POOL, D_POOL), fix),
                  pl.BlockSpec((1, D_POOL), fix),
                  pl.BlockSpec((3, D_HY_IN), fix),
                  pl.BlockSpec((1, D_HY_IN), fix),
                  pl.BlockSpec((1, D_ATTN), fix),
                  pl.BlockSpec((1, D_POOL), fix)],
        out_specs=[pl.BlockSpec((rows, D_AB), cur),
                   pl.BlockSpec((None, FFT_JH, D_HY_IN // LANES, MIX_NB * FFT_J, LANES),
                                lambda b, l: (b, 0, 0, l, 0))],
        scratch_shapes=[pltpu.VMEM((MIX_NB, D_ATTN, BLOCK), F32),
                        pltpu.VMEM((n_pairs, 3 * BLOCK, gw), F32),
                        pltpu.VMEM((n_pairs, 3 * BLOCK, gw), BF16),
                        pltpu.VMEM((n_pairs, 1, gw), F32)],
        compiler_params=_params(2),
        name="mix",
    )(sink, *([bias] * MIX_NB), q, kv, kv, kv, ph, ph, ph, pw_bd, pool_scale, hy_w, hy_b, g_a, g_b)


def _split3(x, axis):
    hi = x.astype(BF16)
    lo = (x - hi.astype(F32)).astype(BF16)
    return jnp.concatenate([hi, lo, hi], axis=axis)


def _stack3(w):
    hi = w.astype(BF16)
    lo = (w - hi.astype(F32)).astype(BF16)
    return jnp.concatenate([hi, hi, lo], axis=0)


def _filter_kernel(z_ref, t_ref, w1_ref, b1_ref, f1_ref, w2_ref, b2_ref, f2_ref, w3_ref, dl_ref, o_ref, *, seq_len):
    h = jnp.sin(f1_ref[...] * (_dot(w1_ref[...], z_ref[...]) + b1_ref[...]))
    h = jnp.sin(f2_ref[...] * (_dot(w2_ref[...], _split3(h, 0)) + b2_ref[...]))
    h = lax.dot_general(_split3(h, 0), w3_ref[...], (((0,), (0,)), ((), ())),
                        preferred_element_type=F32)
    n = pl.program_id(0) * h.shape[0] + lax.broadcasted_iota(jnp.int32, (h.shape[0], 1), 0)
    decay = jnp.where(n == seq_len, 0.0, jnp.exp(-t_ref[...] * dl_ref[...]))
    for o in range(HYENA_ORDER):
        _store_tiles(o_ref, (o,), 0, h[:, o * D_HYENA:(o + 1) * D_HYENA] * decay)


def _filter_kernels(L, w1, b1, fr1, w2, b2, fr2, w3):
    pos = jnp.arange(2 * L, dtype=jnp.int32)
    tpos = jnp.where(pos < L, pos, (2 * L - pos) % L).astype(F32)[:, None]
    t_norm = tpos * (1.0 / (L - 1))
    bands = jnp.linspace(1e-4, FILTER_BANDS - 1, FILTER_BANDS, dtype=F32)[None, :]
    ang = 2.0 * math.pi * tpos * bands / L
    z = jnp.concatenate([t_norm, jnp.cos(ang), -jnp.sin(ang)], axis=-1)
    z3 = jnp.pad(_split3(z, 1), ((0, 0), (0, LANES - 3 * FILTER_EMB))).T
    w1s = jnp.pad(_stack3(w1), ((0, LANES - 3 * FILTER_EMB), (0, 0)))
    w3d = w3.reshape(FILTER_HIDDEN, HYENA_ORDER, 2, D_HYENA).transpose(2, 0, 1, 3).reshape(
        2, FILTER_HIDDEN, HYENA_ORDER * D_HYENA)
    w3s = jnp.stack([_stack3(w3d[0]), _stack3(w3d[1])])
    max_decay = math.log(DECAY_TARGET) / DECAY_FAST_PCT
    min_decay = math.log(DECAY_TARGET) / DECAY_SLOW_PCT
    dl = jnp.abs(jnp.linspace(min_decay, max_decay, D_HYENA, dtype=F32))[None, :]
    tl = 512
    half = L // tl
    row = lambda i: (i, 0)
    fix = lambda i: (0, 0)
    return pl.pallas_call(
        functools.partial(_filter_kernel, seq_len=L),
        out_shape=jax.ShapeDtypeStruct((HYENA_ORDER, FFT_JH, HY_CT, 2 * L // FFT_N2 * FFT_J, LANES), F32),
        grid=(2 * L // tl,),
        in_specs=[pl.BlockSpec((LANES, tl), lambda i: (0, i)),
                  pl.BlockSpec((tl, 1), row),
                  pl.BlockSpec((FILTER_HIDDEN, LANES), fix),
                  pl.BlockSpec((FILTER_HIDDEN, 1), fix),
                  pl.BlockSpec((FILTER_HIDDEN, 1), fix),
                  pl.BlockSpec((FILTER_HIDDEN, 3 * FILTER_HIDDEN), fix),
                  pl.BlockSpec((FILTER_HIDDEN, 1), fix),
                  pl.BlockSpec((FILTER_HIDDEN, 1), fix),
                  pl.BlockSpec((None, 3 * FILTER_HIDDEN, HYENA_ORDER * D_HYENA), lambda i: (i // half, 0, 0)),
                  pl.BlockSpec((1, D_HYENA), fix)],
        out_specs=pl.BlockSpec((HYENA_ORDER, FFT_JH, HY_CT, tl // FFT_N2 * FFT_J, LANES), lambda i: (0, 0, 0, i, 0)),
        compiler_params=_params(1),
        name="hyena_filters",
    )(z3, t_norm, w1s.T, b1[:, None], fr1[:, None], _stack3(w2).T, b2[:, None], fr2[:, None], w3s, dl)


def _dft_mats(L):
    N = 2 * L
    n1n = N // FFT_N2
    r = n1n // 2
    assert r % FFT_TF == 0, "sequence length must be a multiple of FFT_TF * FFT_N2 / 2"
    mh = r + FFT_TF
    f1 = jnp.arange(mh, dtype=jnp.int32)[:, None]
    n1 = jnp.arange(n1n, dtype=jnp.int32)[None, :]
    keep = (f1 <= r).astype(F32)
    th = (2.0 * math.pi / n1n) * ((f1 * n1) % n1n).astype(F32)
    c1, s1 = keep * jnp.cos(th), keep * jnp.sin(th)
    fwd_full = jnp.concatenate([c1, -s1], axis=0).astype(BF16)
    fwd_half = fwd_full[:, :r]
    w = jnp.where((f1 == 0) | (f1 == r), 1.0, 2.0)
    inv_half = jnp.concatenate([(w * c1)[:, :r].T, (-w * s1)[:, :r].T], axis=1).astype(BF16)
    i2 = jnp.arange(2 * FFT_N2, dtype=jnp.int32)
    part, idx = i2 // FFT_N2, i2 % FFT_N2
    pa = (2.0 * math.pi / N) * ((jnp.arange(mh, dtype=jnp.int32)[:, None] * idx[None, :]) % N).astype(F32)
    ca, sa = jnp.cos(pa), jnp.sin(pa)
    mb = (idx[:, None] * idx[None, :] * n1n + (part[:, None] - part[None, :]) * (N // 4)) % N
    pb = (2.0 * math.pi / N) * mb.astype(F32)
    cb, sb = jnp.cos(pb), jnp.sin(pb)
    g = (ca[:, None, :] * cb[None] - sa[:, None, :] * sb[None]).astype(BF16)
    return dict(n1=n1n, r=r, mh=mh, fwd_full=fwd_full, fwd_half=fwd_half, inv_half=inv_half, g=g)


def _outer_blocks(tb):
    return min(4, max(1, 8 // tb))


def _fft_a_kernel(x_ref, f_ref, a_ref, *, mh):
    f = f_ref[...]
    r = x_ref.shape[3] // FFT_J
    for lead in [(b, jb, ct) for b in range(x_ref.shape[0]) for jb in range(x_ref.shape[1]) for ct in range(HY_CT)]:
        for j in range(FFT_J):
            a = _dot(f, x_ref[lead + (pl.ds(j, r, stride=FFT_J), slice(None))].astype(BF16))
            a_ref[lead + (pl.ds(j, mh, stride=FFT_J), slice(None))] = _pack_pair(a[:mh], a[mh:])


def _fft_a(x5, coff, fmat, mh, tb):
    B, rj = x5.shape[0], x5.shape[3]
    jb = _outer_blocks(tb)
    return pl.pallas_call(
        functools.partial(_fft_a_kernel, mh=mh),
        out_shape=jax.ShapeDtypeStruct((B, FFT_JH, HY_CT, mh * FFT_J, LANES), jnp.uint32),
        grid=(B // tb, FFT_JH // jb),
        in_specs=[pl.BlockSpec((tb, jb, HY_CT, rj, LANES), lambda b, j: (b, j, coff // HY_CT, 0, 0)),
                  pl.BlockSpec((2 * mh, rj // FFT_J), lambda b, j: (0, 0))],
        out_specs=pl.BlockSpec((tb, jb, HY_CT, mh * FFT_J, LANES), lambda b, j: (b, j, 0, 0, 0)),
        compiler_params=_params(2),
        name="fft_outer_fwd",
    )(x5, fmat)


def _fft_spec_kernel(a_ref, g_ref, k_ref, *, scale):
    for f in range(FFT_TF):
        x = _dot(g_ref[f], _unpack_pair(_load_tiles(a_ref, (), f * FFT_J, 1, HY_CT))) * scale
        k_ref[0, f] = _pack_pair(x[:FFT_N2], x[FFT_N2:])


def _tile_spec():
    return pl.BlockSpec((None, FFT_JH, HY_CT, FFT_TF * FFT_J, LANES), lambda f, b: (b, 0, 0, f, 0))


def _fft_spec(a, g, scale):
    B, mh = a.shape[0], a.shape[3] // FFT_J
    nat = pl.BlockSpec((1, FFT_TF, FFT_N2, D_HYENA), lambda f, b: (b, f, 0, 0))
    return pl.pallas_call(
        functools.partial(_fft_spec_kernel, scale=scale),
        out_shape=jax.ShapeDtypeStruct((B, mh, FFT_N2, D_HYENA), jnp.uint32),
        grid=(mh // FFT_TF, B),
        in_specs=[_tile_spec(), pl.BlockSpec((FFT_TF, 2 * FFT_N2, 2 * FFT_N2), lambda f, b: (f, 0, 0))],
        out_specs=nat,
        compiler_params=_params(2),
        name="fft_inner_fwd",
    )(a, g)


def _fft_b_kernel(a_ref, g_ref, k_ref, y_ref):
    tn = (((0,), (0,)), ((), ()))
    for f in range(FFT_TF):
        kw = k_ref[0, f]
        kr = lax.bitcast_convert_type(kw & jnp.uint32(0xFFFF0000), F32)
        ki = lax.bitcast_convert_type(kw << 16, F32)
        for b in range(a_ref.shape[0]):
            x = _dot(g_ref[f], _unpack_pair(_load_tiles(a_ref, (b,), f * FFT_J, 1, HY_CT)))
            xr, xi = x[:FFT_N2], x[FFT_N2:]
            z = jnp.concatenate([xr * kr - xi * ki, xr * ki + xi * kr], axis=0).astype(BF16)
            y = lax.dot_general(g_ref[f], z, tn, preferred_element_type=F32)
            _store_tiles(y_ref, (b,), f * FFT_J, _pack_pair(y[:FFT_N2], y[FFT_N2:]))


def _fft_b(a, g, k, o):
    B, mh = a.shape[0], a.shape[3] // FFT_J
    tb = math.gcd(B, 4)
    dat = pl.BlockSpec((tb, FFT_JH, HY_CT, FFT_TF * FFT_J, LANES), lambda f, b: (b, 0, 0, f, 0))
    mat = pl.BlockSpec((FFT_TF, 2 * FFT_N2, 2 * FFT_N2), lambda f, b: (f, 0, 0))
    spec = pl.BlockSpec((1, FFT_TF, FFT_N2, D_HYENA), lambda f, b: (o, f, 0, 0))
    return pl.pallas_call(
        _fft_b_kernel,
        out_shape=jax.ShapeDtypeStruct(a.shape, jnp.uint32),
        grid=(mh // FFT_TF, B // tb),
        in_specs=[dat, mat, spec],
        out_specs=dat,
        compiler_params=_params(2),
        name="fft_inner",
    )(a, g, k)


def _fft_c_kernel(y_ref, c_ref, u_ref, gate_ref, bias_ref, o_ref):
    cm = c_ref[...]
    mh, r = y_ref.shape[3] // FFT_J, u_ref.shape[3] // FFT_J
    for lead in [(b, jb, ct) for b in range(y_ref.shape[0]) for jb in range(y_ref.shape[1]) for ct in range(HY_CT)]:
        bias = bias_ref[:, lead[2] * LANES:(lead[2] + 1) * LANES]
        for j in range(FFT_J):
            y = _dot(cm, _unpack_pair(y_ref[lead + (pl.ds(j, mh, stride=FFT_J), slice(None))]))
            rows = lead + (pl.ds(j, r, stride=FFT_J), slice(None))
            o_ref[rows] = gate_ref[rows] * (y + bias * u_ref[rows])


def _fft_c(y, cmat, u5, uoff, gate5, goff, bias, tb):
    B, mhj = y.shape[0], y.shape[3]
    rj = cmat.shape[0] * FFT_J
    jb = _outer_blocks(tb)
    blk = lambda b, j: (b, j, 0, 0, 0)
    return pl.pallas_call(
        _fft_c_kernel,
        out_shape=jax.ShapeDtypeStruct((B, FFT_JH, HY_CT, rj, LANES), F32),
        grid=(B // tb, FFT_JH // jb),
        in_specs=[pl.BlockSpec((tb, jb, HY_CT, mhj, LANES), blk),
                  pl.BlockSpec(cmat.shape, lambda b, j: (0, 0)),
                  pl.BlockSpec((tb, jb, HY_CT, rj, LANES), lambda b, j: (b, j, uoff // HY_CT, 0, 0)),
                  pl.BlockSpec((tb, jb, HY_CT, rj, LANES), lambda b, j: (b, j, goff // HY_CT, 0, 0)),
                  pl.BlockSpec((1, D_HYENA), lambda b, j: (0, 0))],
        out_specs=pl.BlockSpec((tb, jb, HY_CT, rj, LANES), blk),
        compiler_params=_params(2),
        name="fft_outer_inv",
    )(y, cmat, u5, gate5, bias)


def _filter_spectra(kern, mats, L):
    return _fft_spec(_fft_a(kern, 0, mats["fwd_full"], mats["mh"], 1), mats["g"], 1.0 / (2 * L))


def _hyena(hy5, k, hy_bias, mats, tb):
    z5 = hy5
    for o in range(HYENA_ORDER):
        a = _fft_a(z5, 0, mats["fwd_half"], mats["mh"], tb)
        y = _fft_b(a, mats["g"], k, o)
        z5 = _fft_c(y, mats["inv_half"], z5, 0, hy5, (o + 1) * HY_CT, hy_bias[o][None, :], tb)
    return z5


def _out_proj_kernel(x_ref, cab_ref, c_ref, gc_ref, wab_ref, wc_ref, xn_ref):
    c = _load_tiles(c_ref, (), 0, c_ref.shape[2] // FFT_J, HY_CT)
    cn = _rms(c, gc_ref[...]).astype(BF16)
    xn_ref[...] = x_ref[...] + _dot(cab_ref[...], wab_ref[...]) + _dot(cn, wc_ref[...])


def _out_proj(x2, cab, c5, g_c, w_ab, w_c):
    T = x2.shape[0]
    tm = ROW_TILE
    tps = c5.shape[3] // FFT_J * FFT_N2 // tm
    row = lambda i: (i, 0)
    fix = lambda i: (0, 0)
    return pl.pallas_call(
        _out_proj_kernel,
        out_shape=jax.ShapeDtypeStruct((T, D_MODEL), F32),
        grid=(T // tm,),
        in_specs=[pl.BlockSpec((tm, D_MODEL), row),
                  pl.BlockSpec((tm, D_AB), row),
                  pl.BlockSpec((None, FFT_JH, HY_CT, tm // FFT_N2 * FFT_J, LANES),
                               lambda i: (i // tps, 0, 0, i % tps, 0)),
                  pl.BlockSpec((1, D_HYENA), fix),
                  pl.BlockSpec((D_AB, D_MODEL), fix),
                  pl.BlockSpec((D_HYENA, D_MODEL), fix)],
        out_specs=pl.BlockSpec((tm, D_MODEL), row),
        compiler_params=_params(1),
        name="out_proj",
    )(x2, cab, c5, g_c, w_ab, w_c)


def _ffn_kernel(xn_ref, xp_ref, xq_ref, g2_ref, wi_ref, cw_ref, cb_ref, wo_ref, o_ref,
                acc_ref, hx_ref, ug_ref, uv_ref, *, tiles_per_seq):
    i = pl.program_id(0)
    is_first = (i % tiles_per_seq) == 0
    is_last = (i % tiles_per_seq) == tiles_per_seq - 1
    tm = xn_ref.shape[0]
    row = lax.broadcasted_iota(jnp.int32, (tm, 1), 0)
    hx_ref[0:tm, :] = _rms(xn_ref[...], g2_ref[...]).astype(BF16)
    hx_ref[tm:tm + BF16_ROWS, :] = _rms(xp_ref[...], g2_ref[...]).astype(BF16)
    hx_ref[tm + BF16_ROWS:, :] = _rms(xq_ref[...], g2_ref[...]).astype(BF16)

    def up(j):
        ug_ref[j % 2] = _dot(hx_ref[...], wi_ref[j])
        uv_ref[j % 2] = _dot(hx_ref[...], wi_ref[FFN_CHUNKS + j])

    def conv(u_ref, j, k):
        uc = u_ref[j % 2, 0:tm, :]
        before = jnp.where(is_first, 0.0, u_ref[j % 2, tm + BF16_ROWS - 1:tm + BF16_ROWS, :])
        after = jnp.where(is_last, 0.0, u_ref[j % 2, tm + BF16_ROWS:tm + BF16_ROWS + 1, :])
        um = jnp.where(row == 0, before, pltpu.roll(uc, 1, axis=0))
        un = jnp.where(row == tm - 1, after, pltpu.roll(uc, tm - 1, axis=0))
        w = cw_ref[k]
        return um * w[0:1] + uc * w[1:2] + un * w[2:3] + cb_ref[k]

    def down(j):
        gate = conv(ug_ref, j, j)
        half_val = conv(uv_ref, j, FFN_CHUNKS + j)
        act = gate * (1.0 + lax.erf(gate * math.sqrt(0.5))) * half_val
        acc_ref[...] += _dot(act.astype(BF16), wo_ref[j])

    acc_ref[...] = xn_ref[...]
    up(0)
    for j in range(FFN_CHUNKS):
        if j + 1 < FFN_CHUNKS:
            up(j + 1)
        down(j)
    o_ref[...] = acc_ref[...]


def _ffn(xn, g2, w_in, cw, cb, w_out, L):
    T = xn.shape[0]
    tm = ROW_TILE
    ch = D_FF // FFN_CHUNKS
    hpt = tm // BF16_ROWS
    n16 = T // BF16_ROWS
    wi = w_in.reshape(D_MODEL, 2 * FFN_CHUNKS, ch).transpose(1, 0, 2)
    cwc = cw.reshape(3, 2 * FFN_CHUNKS, ch).transpose(1, 0, 2)
    cbc = cb.reshape(2 * FFN_CHUNKS, 1, ch)
    wo = w_out.reshape(FFN_CHUNKS, ch, D_MODEL)
    row = lambda i: (i, 0)
    fix3 = lambda i: (0, 0, 0)
    once = pl.Buffered(1)
    return pl.pallas_call(
        functools.partial(_ffn_kernel, tiles_per_seq=L // tm),
        out_shape=jax.ShapeDtypeStruct((T, D_MODEL), F32),
        grid=(T // tm,),
        in_specs=[pl.BlockSpec((tm, D_MODEL), row),
                  pl.BlockSpec((BF16_ROWS, D_MODEL), lambda i: (jnp.maximum(i * hpt - 1, 0), 0)),
                  pl.BlockSpec((BF16_ROWS, D_MODEL), lambda i: (jnp.minimum((i + 1) * hpt, n16 - 1), 0)),
                  pl.BlockSpec((1, D_MODEL), lambda i: (0, 0)),
                  pl.BlockSpec((2 * FFN_CHUNKS, D_MODEL, ch), fix3, pipeline_mode=once),
                  pl.BlockSpec((2 * FFN_CHUNKS, 3, ch), fix3),
                  pl.BlockSpec((2 * FFN_CHUNKS, 1, ch), fix3),
                  pl.BlockSpec((FFN_CHUNKS, ch, D_MODEL), fix3, pipeline_mode=once)],
        out_specs=pl.BlockSpec((tm, D_MODEL), row),
        scratch_shapes=[pltpu.VMEM((tm, D_MODEL), F32),
                        pltpu.VMEM((tm + 2 * BF16_ROWS, D_MODEL), BF16),
                        pltpu.VMEM((2, tm + 2 * BF16_ROWS, ch), F32),
                        pltpu.VMEM((2, tm + 2 * BF16_ROWS, ch), F32)],
        compiler_params=_params(1),
        name="ffn",
    )(xn, xn, xn, g2, wi, cwc, cbc, wo)


def _head_ones(n_heads):
    return jnp.kron(jnp.eye(n_heads, dtype=F32), jnp.ones((HEAD_DIM, HEAD_DIM), F32)).astype(BF16)


def _layer(x2, B, L, mats, spectra, lw):
    q, kv, ph = _in_proj(x2, lw["norm1_g"], lw["wq_t"], lw["w_in"], lw["ek"], lw["qg"], lw["kg"])
    cab, hy = _mix(q, kv, ph, lw["sink"], lw["pw_bd"], lw["pool_scale"], lw["hy_w"], lw["hy_b"],
                   lw["g_a"], lw["g_b"], B, L)
    tb = math.gcd(B, 8)
    c5 = _hyena(hy, spectra, lw["hy_bias"], mats, tb)
    xn = _out_proj(x2, cab, c5, lw["g_c"], lw["w_ab"], lw["w_c"])
    return _ffn(xn, lw["norm2_g"], lw["w_ffn_in"], lw["ffn_cw"], lw["ffn_cb"], lw["w_ffn_out"], L)


def kernel(x_prompt, x_sample, norm1_g, w_in, q_norm_g, k_norm_g, attn_sink, pool_w, pool_scale, hy_conv_w,
           hy_conv_b, filt_w1, filt_b1, filt_freq1, filt_w2, filt_b2, filt_freq2, filt_w3, hy_bias, out_norm_g,
           w_out, norm2_g, w_ffn_in, ffn_conv_w, ffn_conv_b, w_ffn_out):
    groups = [x_prompt, x_sample]
    lens = sorted({g.shape[1] for g in groups})
    mats = {L: _dft_mats(L) for L in lens}
    ek = _head_ones(N_KV_HEADS)
    xs = [g.reshape(-1, D_MODEL) for g in groups]
    glu_scale = jnp.concatenate([jnp.ones((D_FF,), F32), jnp.full((D_FF,), 0.5, F32)])[None, :]
    for l in range(DEPTH):
        lw = dict(
            norm1_g=norm1_g[l][None, :], wq_t=w_in[l, :, :D_ATTN].T.astype(BF16), w_in=w_in[l, :, D_ATTN:].astype(BF16),
            ek=ek, kg=jnp.tile(k_norm_g[l], N_KV_HEADS)[None, :],
            qg=(jnp.tile(q_norm_g[l], N_Q_HEADS) * (LOG2E / math.sqrt(HEAD_DIM)))[:, None],
            sink=attn_sink[l] * LOG2E,
            pw_bd=jax.scipy.linalg.block_diag(*[pool_w[l, g] for g in range(len(POOL_WINDOWS))]).astype(BF16),
            pool_scale=pool_scale[l][None, :], hy_w=hy_conv_w[l], hy_b=hy_conv_b[l][None, :],
            g_a=out_norm_g[l, :D_ATTN][None, :], g_b=out_norm_g[l, D_ATTN:D_AB][None, :],
            g_c=out_norm_g[l, D_AB:][None, :], hy_bias=hy_bias[l],
            w_ab=w_out[l, :D_AB].astype(BF16), w_c=w_out[l, D_AB:].astype(BF16),
            norm2_g=norm2_g[l][None, :], w_ffn_in=w_ffn_in[l].astype(BF16),
            ffn_cw=ffn_conv_w[l] * glu_scale, ffn_cb=(ffn_conv_b[l] * glu_scale[0])[None, :],
            w_ffn_out=w_ffn_out[l].astype(BF16),
        )
        spectra = {}
        for L in lens:
            kern = _filter_kernels(L, filt_w1[l], filt_b1[l], filt_freq1[l], filt_w2[l], filt_b2[l], filt_freq2[l],
                                   filt_w3[l])
            spectra[L] = _filter_spectra(kern, mats[L], L)
        xs = [_layer(x2, g.shape[0], g.shape[1], mats[g.shape[1]], spectra[g.shape[1]], lw)
              for x2, g in zip(xs, groups)]
    return tuple(x2.reshape(g.shape) for x2, g in zip(xs, groups))
```

```python
import functools
import math

import jax
import jax.numpy as jnp
from jax import lax
from jax.experimental import pallas as pl
from jax.experimental.pallas import tpu as pltpu

F32 = jnp.float32
BF16 = jnp.bfloat16

D_MODEL = 1024
DEPTH = 2
HEAD_DIM = 64
N_Q_HEADS = 8
N_KV_HEADS = 2
GQA_GROUP = N_Q_HEADS // N_KV_HEADS
D_ATTN = N_Q_HEADS * HEAD_DIM
D_KV = N_KV_HEADS * HEAD_DIM
WINDOW = 128
BLOCK = 128
POOL_WINDOWS = (2, 4, 8, 16)
D_POOL = 256
POOL_GROUP_DIM = D_POOL // len(POOL_WINDOWS)
D_HYENA = 256
HYENA_ORDER = 2
FILTER_BANDS = 16
FILTER_EMB = 1 + 2 * FILTER_BANDS
FILTER_HIDDEN = 64
N_FILTERS = 2 * HYENA_ORDER
DECAY_FAST_PCT = 0.3
DECAY_SLOW_PCT = 1.5
DECAY_TARGET = 1e-2
D_HY_IN = (HYENA_ORDER + 1) * D_HYENA
D_PH = D_POOL + D_HY_IN
D_IN_PROJ = D_ATTN + 2 * D_KV + D_PH
D_CAT = D_ATTN + D_POOL + D_HYENA
D_AB = D_ATTN + D_POOL
D_FF = 2816
EPS = 1e-6
NEG_INF = -1e30
LOG2E = math.log2(math.e)

LANES = 128
SUBLANES = 8
FFT_N2 = 128
FFT_J = SUBLANES
FFT_TF = 8
FFT_JH = FFT_N2 // FFT_J
HY_CT = D_HYENA // LANES
FFN_CHUNKS = 2
BF16_ROWS = 16
ROW_TILE = 512
MIX_NB = 2
VMEM_LIMIT = 56 * 1024 * 1024


def _params(n_axes):
    return pltpu.CompilerParams(dimension_semantics=("arbitrary",) * n_axes, vmem_limit_bytes=VMEM_LIMIT)


def _rms(x, g):
    return x * lax.rsqrt(jnp.mean(x * x, axis=-1, keepdims=True) + EPS) * g


def _dot(a, b):
    return jnp.dot(a, b, preferred_element_type=F32)


def _store_tiles(ref, lead, row0, val):
    for nl in range(val.shape[0] // FFT_N2):
        for jh in range(FFT_JH):
            for ct in range(val.shape[1] // LANES):
                r0 = nl * FFT_N2 + jh * FFT_J
                ref[lead + (jh, ct, pl.ds(row0 + nl * FFT_J, FFT_J), slice(None))] = (
                    val[r0:r0 + FFT_J, ct * LANES:(ct + 1) * LANES])


def _load_tiles(ref, lead, row0, n_local, n_ct):
    return jnp.concatenate(
        [jnp.concatenate([ref[lead + (jh, ct, pl.ds(row0 + nl * FFT_J, FFT_J), slice(None))]
                          for nl in range(n_local) for jh in range(FFT_JH)], axis=0)
         for ct in range(n_ct)], axis=1)


def _pack_pair(re, im):
    hi = lax.bitcast_convert_type(re.astype(BF16).astype(F32), jnp.uint32)
    lo = lax.bitcast_convert_type(im.astype(BF16).astype(F32), jnp.uint32)
    return hi | (lo >> 16)


def _unpack_pair(w):
    re = lax.bitcast_convert_type(w & jnp.uint32(0xFFFF0000), F32)
    im = lax.bitcast_convert_type(w << 16, F32)
    return jnp.concatenate([re, im], axis=0).astype(BF16)


def _in_proj_kernel(x_ref, g_ref, wq_ref, w_ref, ek_ref, qg_ref, kg_ref, q_ref, kv_ref, ph_ref):
    h = _rms(x_ref[...], g_ref[...]).astype(BF16)
    tm = h.shape[0]

    qt = lax.dot_general(wq_ref[...], h, (((1,), (1,)), ((), ())), preferred_element_type=F32)
    qt = qt.reshape(N_Q_HEADS, HEAD_DIM, tm)
    qt = qt * lax.rsqrt(jnp.mean(qt * qt, axis=1, keepdims=True) + EPS)
    qt = (qt.reshape(D_ATTN, tm) * qg_ref[...]).astype(BF16)
    for j in range(tm // BLOCK):
        q_ref[j] = qt[:, j * BLOCK:(j + 1) * BLOCK]

    p = _dot(h, w_ref[...])
    k = p[:, :D_KV]
    ss = _dot((k * k).astype(BF16), ek_ref[...])
    kv_ref[:, :D_KV] = (k * lax.rsqrt(ss * (1.0 / HEAD_DIM) + EPS) * kg_ref[...]).astype(BF16)
    kv_ref[:, D_KV:] = p[:, D_KV:2 * D_KV].astype(BF16)
    ph_ref[...] = p[:, 2 * D_KV:]


def _in_proj(x2, g, wq_t, w, ek, qg, kg):
    T = x2.shape[0]
    tm = ROW_TILE
    row = lambda i: (i, 0)
    fix = lambda i: (0, 0)
    return pl.pallas_call(
        _in_proj_kernel,
        out_shape=[jax.ShapeDtypeStruct((T // BLOCK, D_ATTN, BLOCK), BF16),
                   jax.ShapeDtypeStruct((T, 2 * D_KV), BF16),
                   jax.ShapeDtypeStruct((T, D_PH), F32)],
        grid=(T // tm,),
        in_specs=[pl.BlockSpec((tm, D_MODEL), row),
                  pl.BlockSpec((1, D_MODEL), fix),
                  pl.BlockSpec((D_ATTN, D_MODEL), fix),
                  pl.BlockSpec((D_MODEL, 2 * D_KV + D_PH), fix),
                  pl.BlockSpec((D_KV, D_KV), fix),
                  pl.BlockSpec((D_ATTN, 1), fix),
                  pl.BlockSpec((1, D_KV), fix)],
        out_specs=[pl.BlockSpec((tm // BLOCK, D_ATTN, BLOCK), lambda i: (i, 0, 0)),
                   pl.BlockSpec((tm, 2 * D_KV), row),
                   pl.BlockSpec((tm, D_PH), row)],
        compiler_params=_params(1),
        name="in_proj",
    )(x2, g, wq_t, w, ek, qg, kg)


def _mix_kernel(sink_ref, *refs, seq_len):
    bias_refs = refs[:MIX_NB]
    (q_ref, kvp_ref, kvc_ref, kvn_ref, phc_ref, php_ref, phn_ref, pw_ref, ps_ref, hw_ref, hb_ref, ga_ref, gb_ref,
     cat_ref, hy_ref, att_ref, s_ref, p_ref, rden_ref) = refs[MIX_NB:]
    l = pl.program_id(1)
    is_first = l == 0
    is_last = l == pl.num_programs(1) - 1
    rows = MIX_NB * BLOCK

    tn = (((0,), (0,)), ((), ()))

    def keys(sb, cols):
        def blk(i):
            if i < 0:
                return kvp_ref[:, cols]
            if i >= MIX_NB:
                return kvn_ref[:, cols]
            return kvc_ref[i * BLOCK:(i + 1) * BLOCK, cols]
        return jnp.concatenate([blk(sb - 1), blk(sb), blk(sb + 1)], axis=0)

    pairs = [(sb, g) for sb in range(MIX_NB) for g in range(N_KV_HEADS)]
    gw = GQA_GROUP * BLOCK
    for i, (sb, g) in enumerate(pairs):
        qg = jnp.concatenate([q_ref[sb, h * HEAD_DIM:(h + 1) * HEAD_DIM, :]
                              for h in range(g * GQA_GROUP, (g + 1) * GQA_GROUP)], axis=1)
        st = _dot(keys(sb, slice(g * HEAD_DIM, (g + 1) * HEAD_DIM)), qg)
        for j in range(GQA_GROUP):
            s_ref[i, :, j * BLOCK:(j + 1) * BLOCK] = st[:, j * BLOCK:(j + 1) * BLOCK] + bias_refs[sb][g * GQA_GROUP + j]
    for i, (sb, g) in enumerate(pairs):
        for j in range(GQA_GROUP):
            cols = slice(j * BLOCK, (j + 1) * BLOCK)
            st = s_ref[i, :, cols]
            sk = sink_ref[g * GQA_GROUP + j]
            m = jnp.maximum(jnp.max(st, axis=0, keepdims=True), sk)
            e = jnp.exp2(st - m)
            rden_ref[i, :, cols] = 1.0 / (jnp.sum(e, axis=0, keepdims=True) + jnp.exp2(sk - m))
            p_ref[i, :, cols] = e.astype(BF16)
    for i, (sb, g) in enumerate(pairs):
        vh = keys(sb, slice(D_KV + g * HEAD_DIM, D_KV + (g + 1) * HEAD_DIM))
        ot = lax.dot_general(vh, p_ref[i], tn, preferred_element_type=F32) * rden_ref[i]
        for j in range(GQA_GROUP):
            h = g * GQA_GROUP + j
            att_ref[sb, h * HEAD_DIM:(h + 1) * HEAD_DIM, :] = ot[:, j * BLOCK:(j + 1) * BLOCK]
    for sb in range(MIX_NB):
        cat_ref[sb * BLOCK:(sb + 1) * BLOCK, :D_ATTN] = _rms(att_ref[sb].T, ga_ref[...]).astype(BF16)

    def halo_ext(cols):
        return jnp.concatenate([jnp.where(is_first, 0.0, php_ref[:, cols]), phc_ref[:, cols],
                                jnp.where(is_last, 0.0, phn_ref[:, cols])], axis=0)

    t = l * rows + lax.broadcasted_iota(jnp.int32, (rows, 1), 0)
    lane = lax.broadcasted_iota(jnp.int32, (1, LANES), 1)

    def count(w):
        lo = jnp.clip(t - w // 2, 0, seq_len)
        hi = jnp.clip(t - w // 2 + w, 0, seq_len)
        return (hi - lo).astype(F32)

    def window_sums(col0, widths):
        n = rows + 2 * SUBLANES
        p, have, out = halo_ext(slice(col0, col0 + LANES)), 1, []
        for w in widths:
            while have < w:
                p = p + pltpu.roll(p, n - have, axis=0)
                have *= 2
            start = SUBLANES - w // 2
            out.append((p if start == 0 else pltpu.roll(p, n - start, axis=0))[0:rows])
        return out

    gpt = LANES // POOL_GROUP_DIM
    means = []
    for i in range(D_POOL // LANES):
        wa, wb = POOL_WINDOWS[gpt * i], POOL_WINDOWS[gpt * i + 1]
        sa, sb = window_sums(i * LANES, (wa, wb))
        means.append(jnp.where(lane < POOL_GROUP_DIM, sa / count(wa), sb / count(wb)))
    u = phc_ref[:, 0:D_POOL]
    pooled = _dot((jnp.concatenate(means, axis=1) - u).astype(BF16), pw_ref[...]) * ps_ref[...]
    cat_ref[:, D_ATTN:] = _rms(pooled, gb_ref[...]).astype(BF16)

    ridx = lax.broadcasted_iota(jnp.int32, (rows, 1), 0)
    hc = phc_ref[:, D_POOL:]
    hp = jnp.where(ridx == 0, jnp.where(is_first, 0.0, php_ref[SUBLANES - 1:SUBLANES, D_POOL:]),
                   pltpu.roll(hc, 1, axis=0))
    hn = jnp.where(ridx == rows - 1, jnp.where(is_last, 0.0, phn_ref[0:1, D_POOL:]),
                   pltpu.roll(hc, rows - 1, axis=0))
    _store_tiles(hy_ref, (), 0, hp * hw_ref[0:1, :] + hc * hw_ref[1:2, :] + hn * hw_ref[2:3, :] + hb_ref[...])


def _attn_bias():
    row = jnp.arange(BLOCK, dtype=jnp.int32)[None, :]
    col = jnp.arange(3 * BLOCK, dtype=jnp.int32)[:, None]
    dist = jnp.abs(row + BLOCK - col)
    slopes = 2.0 ** (-8.0 * (jnp.arange(N_Q_HEADS, dtype=F32) + 1.0) / N_Q_HEADS)
    bias = -(LOG2E * slopes)[:, None, None] * dist.astype(F32)[None]
    out = []
    for case in range(4):
        valid = dist <= WINDOW
        if case & 1:
            valid = valid & (col >= BLOCK)
        if case & 2:
            valid = valid & (col < 2 * BLOCK)
        out.append(jnp.where(valid[None], bias, NEG_INF))
    return jnp.stack(out)


def _mix(q, kv, ph, sink, pw_bd, pool_scale, hy_w, hy_b, g_a, g_b, B, L):
    T = B * L
    rows = MIX_NB * BLOCK
    nb = L // BLOCK
    ns = L // rows
    rps = rows // SUBLANES
    n8 = T // SUBLANES
    cur = lambda b, l: (b * ns + l, 0)
    prev = lambda b, l: (b * nb + jnp.maximum(l * MIX_NB - 1, 0), 0)
    nxt = lambda b, l: (b * nb + jnp.minimum((l + 1) * MIX_NB, nb - 1), 0)
    prev8 = lambda b, l: (jnp.maximum((b * ns + l) * rps - 1, 0), 0)
    next8 = lambda b, l: (jnp.minimum((b * ns + l + 1) * rps, n8 - 1), 0)
    fix = lambda b, l: (0, 0)

    def bias_spec(sb):
        def index(b, l):
            case = jnp.int32(0)
            if sb == 0:
                case = case + (l == 0).astype(jnp.int32)
            if sb == MIX_NB - 1:
                case = case + 2 * (l == ns - 1).astype(jnp.int32)
            return (case, 0, 0, 0)
        return pl.BlockSpec((None, N_Q_HEADS, 3 * BLOCK, BLOCK), index)

    bias = _attn_bias()
    n_pairs = MIX_NB * N_KV_HEADS
    gw = GQA_GROUP * BLOCK
    return pl.pallas_call(
        functools.partial(_mix_kernel, seq_len=L),
        out_shape=[jax.ShapeDtypeStruct((T, D_AB), BF16),
                   jax.ShapeDtypeStruct((B, FFT_JH, D_HY_IN // LANES, nb * FFT_J, LANES), F32)],
        grid=(B, ns),
        in_specs=[pl.BlockSpec(memory_space=pltpu.SMEM)] + [bias_spec(sb) for sb in range(MIX_NB)] + [
                  pl.BlockSpec((MIX_NB, D_ATTN, BLOCK), lambda b, l: (b * ns + l, 0, 0)),
                  pl.BlockSpec((BLOCK, 2 * D_KV), prev),
                  pl.BlockSpec((rows, 2 * D_KV), cur),
                  pl.BlockSpec((BLOCK, 2 * D_KV), nxt),
                  pl.BlockSpec((rows, D_PH), cur),
                  pl.BlockSpec((SUBLANES, D_PH), prev8),
                  pl.BlockSpec((SUBLANES, D_PH), next8),
                  pl.BlockSpec((D_POOL, D_POOL), fix),
                  pl.BlockSpec((1, D_POOL), fix),
                  pl.BlockSpec((3, D_HY_IN), fix),
                  pl.BlockSpec((1, D_HY_IN), fix),
                  pl.BlockSpec((1, D_ATTN), fix),
                  pl.BlockSpec((1, D_POOL), fix)],
        out_specs=[pl.BlockSpec((rows, D_AB), cur),
                   pl.BlockSpec((None, FFT_JH, D_HY_IN // LANES, MIX_NB * FFT_J, LANES),
                                lambda b, l: (b, 0, 0, l, 0))],
        scratch_shapes=[pltpu.VMEM((MIX_NB, D_ATTN, BLOCK), F32),
                        pltpu.VMEM((n_pairs, 3 * BLOCK, gw), F32),
                        pltpu.VMEM((n_pairs, 3 * BLOCK, gw), BF16),
                        pltpu.VMEM((n_pairs, 1, gw), F32)],
        compiler_params=_params(2),
        name="mix",
    )(sink, *([bias] * MIX_NB), q, kv, kv, kv, ph, ph, ph, pw_bd, pool_scale, hy_w, hy_b, g_a, g_b)


def _split3(x, axis):
    hi = x.astype(BF16)
    lo = (x - hi.astype(F32)).astype(BF16)
    return jnp.concatenate([hi, lo, hi], axis=axis)


def _stack3(w):
    hi = w.astype(BF16)
    lo = (w - hi.astype(F32)).astype(BF16)
    return jnp.concatenate([hi, hi, lo], axis=0)


def _filter_kernel(z_ref, t_ref, w1_ref, b1_ref, f1_ref, w2_ref, b2_ref, f2_ref, w3_ref, dl_ref, o_ref, *, seq_len):
    h = jnp.sin(f1_ref[...] * (_dot(w1_ref[...], z_ref[...]) + b1_ref[...]))
    h = jnp.sin(f2_ref[...] * (_dot(w2_ref[...], _split3(h, 0)) + b2_ref[...]))
    h = lax.dot_general(_split3(h, 0), w3_ref[...], (((0,), (0,)), ((), ())),
                        preferred_element_type=F32)
    n = pl.program_id(0) * h.shape[0] + lax.broadcasted_iota(jnp.int32, (h.shape[0], 1), 0)
    decay = jnp.where(n == seq_len, 0.0, jnp.exp(-t_ref[...] * dl_ref[...]))
    for o in range(HYENA_ORDER):
        _store_tiles(o_ref, (o,), 0, h[:, o * D_HYENA:(o + 1) * D_HYENA] * decay)


def _filter_kernels(L, w1, b1, fr1, w2, b2, fr2, w3):
    pos = jnp.arange(2 * L, dtype=jnp.int32)
    tpos = jnp.where(pos < L, pos, (2 * L - pos) % L).astype(F32)[:, None]
    t_norm = tpos * (1.0 / (L - 1))
    bands = jnp.linspace(1e-4, FILTER_BANDS - 1, FILTER_BANDS, dtype=F32)[None, :]
    ang = 2.0 * math.pi * tpos * bands / L
    z = jnp.concatenate([t_norm, jnp.cos(ang), -jnp.sin(ang)], axis=-1)
    z3 = jnp.pad(_split3(z, 1), ((0, 0), (0, LANES - 3 * FILTER_EMB))).T
    w1s = jnp.pad(_stack3(w1), ((0, LANES - 3 * FILTER_EMB), (0, 0)))
    w3d = w3.reshape(FILTER_HIDDEN, HYENA_ORDER, 2, D_HYENA).transpose(2, 0, 1, 3).reshape(
        2, FILTER_HIDDEN, HYENA_ORDER * D_HYENA)
    w3s = jnp.stack([_stack3(w3d[0]), _stack3(w3d[1])])
    max_decay = math.log(DECAY_TARGET) / DECAY_FAST_PCT
    min_decay = math.log(DECAY_TARGET) / DECAY_SLOW_PCT
    dl = jnp.abs(jnp.linspace(min_decay, max_decay, D_HYENA, dtype=F32))[None, :]
    tl = 1024
    half = L // tl
    row = lambda i: (i, 0)
    fix = lambda i: (0, 0)
    return pl.pallas_call(
        functools.partial(_filter_kernel, seq_len=L),
        out_shape=jax.ShapeDtypeStruct((HYENA_ORDER, FFT_JH, HY_CT, 2 * L // FFT_N2 * FFT_J, LANES), F32),
        grid=(2 * L // tl,),
        in_specs=[pl.BlockSpec((LANES, tl), lambda i: (0, i)),
                  pl.BlockSpec((tl, 1), row),
                  pl.BlockSpec((FILTER_HIDDEN, LANES), fix),
                  pl.BlockSpec((FILTER_HIDDEN, 1), fix),
                  pl.BlockSpec((FILTER_HIDDEN, 1), fix),
                  pl.BlockSpec((FILTER_HIDDEN, 3 * FILTER_HIDDEN), fix),
                  pl.BlockSpec((FILTER_HIDDEN, 1), fix),
                  pl.BlockSpec((FILTER_HIDDEN, 1), fix),
                  pl.BlockSpec((None, 3 * FILTER_HIDDEN, HYENA_ORDER * D_HYENA), lambda i: (i // half, 0, 0)),
                  pl.BlockSpec((1, D_HYENA), fix)],
        out_specs=pl.BlockSpec((HYENA_ORDER, FFT_JH, HY_CT, tl // FFT_N2 * FFT_J, LANES), lambda i: (0, 0, 0, i, 0)),
        compiler_params=_params(1),
        name="hyena_filters",
    )(z3, t_norm, w1s.T, b1[:, None], fr1[:, None], _stack3(w2).T, b2[:, None], fr2[:, None], w3s, dl)


def _dft_mats(L):
    N = 2 * L
    n1n = N // FFT_N2
    r = n1n // 2
    assert r % FFT_TF == 0, "sequence length must be a multiple of FFT_TF * FFT_N2 / 2"
    mh = r + FFT_TF
    f1 = jnp.arange(mh, dtype=jnp.int32)[:, None]
    n1 = jnp.arange(n1n, dtype=jnp.int32)[None, :]
    keep = (f1 <= r).astype(F32)
    th = (2.0 * math.pi / n1n) * ((f1 * n1) % n1n).astype(F32)
    c1, s1 = keep * jnp.cos(th), keep * jnp.sin(th)
    fwd_full = jnp.concatenate([c1, -s1], axis=0).astype(BF16)
    fwd_half = fwd_full[:, :r]
    w = jnp.where((f1 == 0) | (f1 == r), 1.0, 2.0)
    inv_half = jnp.concatenate([(w * c1)[:, :r].T, (-w * s1)[:, :r].T], axis=1).astype(BF16)
    i2 = jnp.arange(2 * FFT_N2, dtype=jnp.int32)
    part, idx = i2 // FFT_N2, i2 % FFT_N2
    pa = (2.0 * math.pi / N) * ((jnp.arange(mh, dtype=jnp.int32)[:, None] * idx[None, :]) % N).astype(F32)
    ca, sa = jnp.cos(pa), jnp.sin(pa)
    mb = (idx[:, None] * idx[None, :] * n1n + (part[:, None] - part[None, :]) * (N // 4)) % N
    pb = (2.0 * math.pi / N) * mb.astype(F32)
    cb, sb = jnp.cos(pb), jnp.sin(pb)
    g = (ca[:, None, :] * cb[None] - sa[:, None, :] * sb[None]).astype(BF16)
    return dict(n1=n1n, r=r, mh=mh, fwd_full=fwd_full, fwd_half=fwd_half, inv_half=inv_half, g=g)


def _outer_blocks(tb):
    return min(4, max(1, 16 // tb))


def _fft_a_kernel(x_ref, f_ref, a_ref, *, mh):
    f = f_ref[...]
    r = x_ref.shape[3] // FFT_J
    for lead in [(b, jb, ct) for b in range(x_ref.shape[0]) for jb in range(x_ref.shape[1]) for ct in range(HY_CT)]:
        for j in range(FFT_J):
            a = _dot(f, x_ref[lead + (pl.ds(j, r, stride=FFT_J), slice(None))].astype(BF16))
            a_ref[lead + (pl.ds(j, mh, stride=FFT_J), slice(None))] = _pack_pair(a[:mh], a[mh:])


def _fft_a(x5, coff, fmat, mh, tb):
    B, rj = x5.shape[0], x5.shape[3]
    jb = _outer_blocks(tb)
    return pl.pallas_call(
        functools.partial(_fft_a_kernel, mh=mh),
        out_shape=jax.ShapeDtypeStruct((B, FFT_JH, HY_CT, mh * FFT_J, LANES), jnp.uint32),
        grid=(B // tb, FFT_JH // jb),
        in_specs=[pl.BlockSpec((tb, jb, HY_CT, rj, LANES), lambda b, j: (b, j, coff // HY_CT, 0, 0)),
                  pl.BlockSpec((2 * mh, rj // FFT_J), lambda b, j: (0, 0))],
        out_specs=pl.BlockSpec((tb, jb, HY_CT, mh * FFT_J, LANES), lambda b, j: (b, j, 0, 0, 0)),
        compiler_params=_params(2),
        name="fft_outer_fwd",
    )(x5, fmat)


def _fft_spec_kernel(a_ref, g_ref, k_ref, *, scale):
    for f in range(FFT_TF):
        x = _dot(g_ref[f], _unpack_pair(_load_tiles(a_ref, (), f * FFT_J, 1, HY_CT))) * scale
        k_ref[0, f] = _pack_pair(x[:FFT_N2], x[FFT_N2:])


def _tile_spec():
    return pl.BlockSpec((None, FFT_JH, HY_CT, FFT_TF * FFT_J, LANES), lambda f, b: (b, 0, 0, f, 0))


def _fft_spec(a, g, scale):
    B, mh = a.shape[0], a.shape[3] // FFT_J
    nat = pl.BlockSpec((1, FFT_TF, FFT_N2, D_HYENA), lambda f, b: (b, f, 0, 0))
    return pl.pallas_call(
        functools.partial(_fft_spec_kernel, scale=scale),
        out_shape=jax.ShapeDtypeStruct((B, mh, FFT_N2, D_HYENA), jnp.uint32),
        grid=(mh // FFT_TF, B),
        in_specs=[_tile_spec(), pl.BlockSpec((FFT_TF, 2 * FFT_N2, 2 * FFT_N2), lambda f, b: (f, 0, 0))],
        out_specs=nat,
        compiler_params=_params(2),
        name="fft_inner_fwd",
    )(a, g)


def _fft_b_kernel(a_ref, g_ref, k_ref, y_ref):
    tn = (((0,), (0,)), ((), ()))
    for f in range(FFT_TF):
        kw = k_ref[0, f]
        kr = lax.bitcast_convert_type(kw & jnp.uint32(0xFFFF0000), F32)
        ki = lax.bitcast_convert_type(kw << 16, F32)
        for b in range(a_ref.shape[0]):
            x = _dot(g_ref[f], _unpack_pair(_load_tiles(a_ref, (b,), f * FFT_J, 1, HY_CT)))
            xr, xi = x[:FFT_N2], x[FFT_N2:]
            z = jnp.concatenate([xr * kr - xi * ki, xr * ki + xi * kr], axis=0).astype(BF16)
            y = lax.dot_general(g_ref[f], z, tn, preferred_element_type=F32)
            _store_tiles(y_ref, (b,), f * FFT_J, _pack_pair(y[:FFT_N2], y[FFT_N2:]))


def _fft_b(a, g, k, o):
    B, mh = a.shape[0], a.shape[3] // FFT_J
    tb = math.gcd(B, 4)
    dat = pl.BlockSpec((tb, FFT_JH, HY_CT, FFT_TF * FFT_J, LANES), lambda f, b: (b, 0, 0, f, 0))
    mat = pl.BlockSpec((FFT_TF, 2 * FFT_N2, 2 * FFT_N2), lambda f, b: (f, 0, 0))
    spec = pl.BlockSpec((1, FFT_TF, FFT_N2, D_HYENA), lambda f, b: (o, f, 0, 0))
    return pl.pallas_call(
        _fft_b_kernel,
        out_shape=jax.ShapeDtypeStruct(a.shape, jnp.uint32),
        grid=(mh // FFT_TF, B // tb),
        in_specs=[dat, mat, spec],
        out_specs=dat,
        compiler_params=_params(2),
        name="fft_inner",
    )(a, g, k)


def _fft_c_kernel(y_ref, c_ref, u_ref, gate_ref, bias_ref, o_ref):
    cm = c_ref[...]
    mh, r = y_ref.shape[3] // FFT_J, u_ref.shape[3] // FFT_J
    for lead in [(b, jb, ct) for b in range(y_ref.shape[0]) for jb in range(y_ref.shape[1]) for ct in range(HY_CT)]:
        bias = bias_ref[:, lead[2] * LANES:(lead[2] + 1) * LANES]
        for j in range(FFT_J):
            y = _dot(cm, _unpack_pair(y_ref[lead + (pl.ds(j, mh, stride=FFT_J), slice(None))]))
            rows = lead + (pl.ds(j, r, stride=FFT_J), slice(None))
            o_ref[rows] = gate_ref[rows] * (y + bias * u_ref[rows])


def _fft_c(y, cmat, u5, uoff, gate5, goff, bias, tb):
    B, mhj = y.shape[0], y.shape[3]
    rj = cmat.shape[0] * FFT_J
    jb = _outer_blocks(tb)
    blk = lambda b, j: (b, j, 0, 0, 0)
    return pl.pallas_call(
        _fft_c_kernel,
        out_shape=jax.ShapeDtypeStruct((B, FFT_JH, HY_CT, rj, LANES), F32),
        grid=(B // tb, FFT_JH // jb),
        in_specs=[pl.BlockSpec((tb, jb, HY_CT, mhj, LANES), blk),
                  pl.BlockSpec(cmat.shape, lambda b, j: (0, 0)),
                  pl.BlockSpec((tb, jb, HY_CT, rj, LANES), lambda b, j: (b, j, uoff // HY_CT, 0, 0)),
                  pl.BlockSpec((tb, jb, HY_CT, rj, LANES), lambda b, j: (b, j, goff // HY_CT, 0, 0)),
                  pl.BlockSpec((1, D_HYENA), lambda b, j: (0, 0))],
        out_specs=pl.BlockSpec((tb, jb, HY_CT, rj, LANES), blk),
        compiler_params=_params(2),
        name="fft_outer_inv",
    )(y, cmat, u5, gate5, bias)


def _filter_spectra(kern, mats, L):
    return _fft_spec(_fft_a(kern, 0, mats["fwd_full"], mats["mh"], 1), mats["g"], 1.0 / (2 * L))


def _hyena(hy5, k, hy_bias, mats, tb):
    z5 = hy5
    for o in range(HYENA_ORDER):
        a = _fft_a(z5, 0, mats["fwd_half"], mats["mh"], tb)
        y = _fft_b(a, mats["g"], k, o)
        z5 = _fft_c(y, mats["inv_half"], z5, 0, hy5, (o + 1) * HY_CT, hy_bias[o][None, :], tb)
    return z5


def _out_proj_kernel(x_ref, cab_ref, c_ref, gc_ref, wab_ref, wc_ref, xn_ref):
    c = _load_tiles(c_ref, (), 0, c_ref.shape[2] // FFT_J, HY_CT)
    cn = _rms(c, gc_ref[...]).astype(BF16)
    xn_ref[...] = x_ref[...] + _dot(cab_ref[...], wab_ref[...]) + _dot(cn, wc_ref[...])


def _out_proj(x2, cab, c5, g_c, w_ab, w_c):
    T = x2.shape[0]
    tm = ROW_TILE
    tps = c5.shape[3] // FFT_J * FFT_N2 // tm
    row = lambda i: (i, 0)
    fix = lambda i: (0, 0)
    return pl.pallas_call(
        _out_proj_kernel,
        out_shape=jax.ShapeDtypeStruct((T, D_MODEL), F32),
        grid=(T // tm,),
        in_specs=[pl.BlockSpec((tm, D_MODEL), row),
                  pl.BlockSpec((tm, D_AB), row),
                  pl.BlockSpec((None, FFT_JH, HY_CT, tm // FFT_N2 * FFT_J, LANES),
                               lambda i: (i // tps, 0, 0, i % tps, 0)),
                  pl.BlockSpec((1, D_HYENA), fix),
                  pl.BlockSpec((D_AB, D_MODEL), fix),
                  pl.BlockSpec((D_HYENA, D_MODEL), fix)],
        out_specs=pl.BlockSpec((tm, D_MODEL), row),
        compiler_params=_params(1),
        name="out_proj",
    )(x2, cab, c5, g_c, w_ab, w_c)


def _ffn_kernel(xn_ref, xp_ref, xq_ref, g2_ref, wi_ref, cw_ref, cb_ref, wo_ref, o_ref,
                acc_ref, hx_ref, ug_ref, uv_ref, *, tiles_per_seq):
    i = pl.program_id(0)
    is_first = (i % tiles_per_seq) == 0
    is_last = (i % tiles_per_seq) == tiles_per_seq - 1
    tm = xn_ref.shape[0]
    row = lax.broadcasted_iota(jnp.int32, (tm, 1), 0)
    hx_ref[0:tm, :] = _rms(xn_ref[...], g2_ref[...]).astype(BF16)
    hx_ref[tm:tm + BF16_ROWS, :] = _rms(xp_ref[...], g2_ref[...]).astype(BF16)
    hx_ref[tm + BF16_ROWS:, :] = _rms(xq_ref[...], g2_ref[...]).astype(BF16)

    def up(j):
        ug_ref[j % 2] = _dot(hx_ref[...], wi_ref[j])
        uv_ref[j % 2] = _dot(hx_ref[...], wi_ref[FFN_CHUNKS + j])

    def conv(u_ref, j, k):
        uc = u_ref[j % 2, 0:tm, :]
        before = jnp.where(is_first, 0.0, u_ref[j % 2, tm + BF16_ROWS - 1:tm + BF16_ROWS, :])
        after = jnp.where(is_last, 0.0, u_ref[j % 2, tm + BF16_ROWS:tm + BF16_ROWS + 1, :])
        um = jnp.where(row == 0, before, pltpu.roll(uc, 1, axis=0))
        un = jnp.where(row == tm - 1, after, pltpu.roll(uc, tm - 1, axis=0))
        w = cw_ref[k]
        return um * w[0:1] + uc * w[1:2] + un * w[2:3] + cb_ref[k]

    def down(j):
        gate = conv(ug_ref, j, j)
        half_val = conv(uv_ref, j, FFN_CHUNKS + j)
        act = gate * (1.0 + lax.erf(gate * math.sqrt(0.5))) * half_val
        acc_ref[...] += _dot(act.astype(BF16), wo_ref[j])

    acc_ref[...] = xn_ref[...]
    up(0)
    for j in range(FFN_CHUNKS):
        if j + 1 < FFN_CHUNKS:
            up(j + 1)
        down(j)
    o_ref[...] = acc_ref[...]


def _ffn(xn, g2, w_in, cw, cb, w_out, L):
    T = xn.shape[0]
    tm = ROW_TILE
    ch = D_FF // FFN_CHUNKS
    hpt = tm // BF16_ROWS
    n16 = T // BF16_ROWS
    wi = w_in.reshape(D_MODEL, 2 * FFN_CHUNKS, ch).transpose(1, 0, 2)
    cwc = cw.reshape(3, 2 * FFN_CHUNKS, ch).transpose(1, 0, 2)
    cbc = cb.reshape(2 * FFN_CHUNKS, 1, ch)
    wo = w_out.reshape(FFN_CHUNKS, ch, D_MODEL)
    row = lambda i: (i, 0)
    fix3 = lambda i: (0, 0, 0)
    once = pl.Buffered(1)
    return pl.pallas_call(
        functools.partial(_ffn_kernel, tiles_per_seq=L // tm),
        out_shape=jax.ShapeDtypeStruct((T, D_MODEL), F32),
        grid=(T // tm,),
        in_specs=[pl.BlockSpec((tm, D_MODEL), row),
                  pl.BlockSpec((BF16_ROWS, D_MODEL), lambda i: (jnp.maximum(i * hpt - 1, 0), 0)),
                  pl.BlockSpec((BF16_ROWS, D_MODEL), lambda i: (jnp.minimum((i + 1) * hpt, n16 - 1), 0)),
                  pl.BlockSpec((1, D_MODEL), lambda i: (0, 0)),
                  pl.BlockSpec((2 * FFN_CHUNKS, D_MODEL, ch), fix3, pipeline_mode=once),
                  pl.BlockSpec((2 * FFN_CHUNKS, 3, ch), fix3),
                  pl.BlockSpec((2 * FFN_CHUNKS, 1, ch), fix3),
                  pl.BlockSpec((FFN_CHUNKS, ch, D_MODEL), fix3, pipeline_mode=once)],
        out_specs=pl.BlockSpec((tm, D_MODEL), row),
        scratch_shapes=[pltpu.VMEM((tm, D_MODEL), F32),
                        pltpu.VMEM((tm + 2 * BF16_ROWS, D_MODEL), BF16),
                        pltpu.VMEM((2, tm + 2 * BF16_ROWS, ch), F32),
                        pltpu.VMEM((2, tm + 2 * BF16_ROWS, ch), F32)],
        compiler_params=_params(1),
        name="ffn",
    )(xn, xn, xn, g2, wi, cwc, cbc, wo)


def _head_ones(n_heads):
    return jnp.kron(jnp.eye(n_heads, dtype=F32), jnp.ones((HEAD_DIM, HEAD_DIM), F32)).astype(BF16)


def _layer(x2, B, L, mats, spectra, lw):
    q, kv, ph = _in_proj(x2, lw["norm1_g"], lw["wq_t"], lw["w_in"], lw["ek"], lw["qg"], lw["kg"])
    cab, hy = _mix(q, kv, ph, lw["sink"], lw["pw_bd"], lw["pool_scale"], lw["hy_w"], lw["hy_b"],
                   lw["g_a"], lw["g_b"], B, L)
    tb = math.gcd(B, 8)
    c5 = _hyena(hy, spectra, lw["hy_bias"], mats, tb)
    xn = _out_proj(x2, cab, c5, lw["g_c"], lw["w_ab"], lw["w_c"])
    return _ffn(xn, lw["norm2_g"], lw["w_ffn_in"], lw["ffn_cw"], lw["ffn_cb"], lw["w_ffn_out"], L)


def kernel(x_prompt, x_sample, norm1_g, w_in, q_norm_g, k_norm_g, attn_sink, pool_w, pool_scale, hy_conv_w,
           hy_conv_b, filt_w1, filt_b1, filt_freq1, filt_w2, filt_b2, filt_freq2, filt_w3, hy_bias, out_norm_g,
           w_out, norm2_g, w_ffn_in, ffn_conv_w, ffn_conv_b, w_ffn_out):
    groups = [x_prompt, x_sample]
    lens = sorted({g.shape[1] for g in groups})
    mats = {L: _dft_mats(L) for L in lens}
    ek = _head_ones(N_KV_HEADS)
    xs = [g.reshape(-1, D_MODEL) for g in groups]
    glu_scale = jnp.concatenate([jnp.ones((D_FF,), F32), jnp.full((D_FF,), 0.5, F32)])[None, :]
    for l in range(DEPTH):
        lw = dict(
            norm1_g=norm1_g[l][None, :], wq_t=w_in[l, :, :D_ATTN].T.astype(BF16), w_in=w_in[l, :, D_ATTN:].astype(BF16),
            ek=ek, kg=jnp.tile(k_norm_g[l], N_KV_HEADS)[None, :],
            qg=(jnp.tile(q_norm_g[l], N_Q_HEADS) * (LOG2E / math.sqrt(HEAD_DIM)))[:, None],
            sink=attn_sink[l] * LOG2E,
            pw_bd=jax.scipy.linalg.block_diag(*[pool_w[l, g] for g in range(len(POOL_WINDOWS))]).astype(BF16),
            pool_scale=pool_scale[l][None, :], hy_w=hy_conv_w[l], hy_b=hy_conv_b[l][None, :],
            g_a=out_norm_g[l, :D_ATTN][None, :], g_b=out_norm_g[l, D_ATTN:D_AB][None, :],
            g_c=out_norm_g[l, D_AB:][None, :], hy_bias=hy_bias[l],
            w_ab=w_out[l, :D_AB].astype(BF16), w_c=w_out[l, D_AB:].astype(BF16),
            norm2_g=norm2_g[l][None, :], w_ffn_in=w_ffn_in[l].astype(BF16),
            ffn_cw=ffn_conv_w[l] * glu_scale, ffn_cb=(ffn_conv_b[l] * glu_scale[0])[None, :],
            w_ffn_out=w_ffn_out[l].astype(BF16),
        )
        spectra = {}
        for L in lens:
            kern = _filter_kernels(L, filt_w1[l], filt_b1[l], filt_freq1[l], filt_w2[l], filt_b2[l], filt_freq2[l],
                                   filt_w3[l])
            spectra[L] = _filter_spectra(kern, mats[L], L)
        xs = [_layer(x2, g.shape[0], g.shape[1], mats[g.shape[1]], spectra[g.shape[1]], lw)
              for x2, g in zip(xs, groups)]
    return tuple(x2.reshape(g.shape) for x2, g in zip(xs, groups))
```

```python
import functools
import math

import jax
import jax.numpy as jnp
from jax import lax
from jax.experimental import pallas as pl
from jax.experimental.pallas import tpu as pltpu

F32 = jnp.float32
BF16 = jnp.bfloat16

D_MODEL = 1024
DEPTH = 2
HEAD_DIM = 64
N_Q_HEADS = 8
N_KV_HEADS = 2
GQA_GROUP = N_Q_HEADS // N_KV_HEADS
D_ATTN = N_Q_HEADS * HEAD_DIM
D_KV = N_KV_HEADS * HEAD_DIM
WINDOW = 128
BLOCK = 128
POOL_WINDOWS = (2, 4, 8, 16)
D_POOL = 256
POOL_GROUP_DIM = D_POOL // len(POOL_WINDOWS)
D_HYENA = 256
HYENA_ORDER = 2
FILTER_BANDS = 16
FILTER_EMB = 1 + 2 * FILTER_BANDS
FILTER_HIDDEN = 64
N_FILTERS = 2 * HYENA_ORDER
DECAY_FAST_PCT = 0.3
DECAY_SLOW_PCT = 1.5
DECAY_TARGET = 1e-2
D_HY_IN = (HYENA_ORDER + 1) * D_HYENA
D_PH = D_POOL + D_HY_IN
D_IN_PROJ = D_ATTN + 2 * D_KV + D_PH
D_CAT = D_ATTN + D_POOL + D_HYENA
D_AB = D_ATTN + D_POOL
D_FF = 2816
EPS = 1e-6
NEG_INF = -1e30
LOG2E = math.log2(math.e)

LANES = 128
SUBLANES = 8
FFT_N2 = 128
FFT_J = SUBLANES
FFT_TF = 8
FFT_JH = FFT_N2 // FFT_J
HY_CT = D_HYENA // LANES
FFN_CHUNKS = 2
BF16_ROWS = 16
ROW_TILE = 512
MIX_NB = 2
VMEM_LIMIT = 56 * 1024 * 1024


def _params(n_axes):
    return pltpu.CompilerParams(dimension_semantics=("arbitrary",) * n_axes, vmem_limit_bytes=VMEM_LIMIT)


def _rms(x, g):
    return x * lax.rsqrt(jnp.mean(x * x, axis=-1, keepdims=True) + EPS) * g


def _dot(a, b):
    return jnp.dot(a, b, preferred_element_type=F32)


def _store_tiles(ref, lead, row0, val):
    for nl in range(val.shape[0] // FFT_N2):
        for jh in range(FFT_JH):
            for ct in range(val.shape[1] // LANES):
                r0 = nl * FFT_N2 + jh * FFT_J
                ref[lead + (jh, ct, pl.ds(row0 + nl * FFT_J, FFT_J), slice(None))] = (
                    val[r0:r0 + FFT_J, ct * LANES:(ct + 1) * LANES])


def _load_tiles(ref, lead, row0, n_local, n_ct):
    return jnp.concatenate(
        [jnp.concatenate([ref[lead + (jh, ct, pl.ds(row0 + nl * FFT_J, FFT_J), slice(None))]
                          for nl in range(n_local) for jh in range(FFT_JH)], axis=0)
         for ct in range(n_ct)], axis=1)


def _pack_pair(re, im):
    hi = lax.bitcast_convert_type(re.astype(BF16).astype(F32), jnp.uint32)
    lo = lax.bitcast_convert_type(im.astype(BF16).astype(F32), jnp.uint32)
    return hi | (lo >> 16)


def _unpack_pair(w):
    re = lax.bitcast_convert_type(w & jnp.uint32(0xFFFF0000), F32)
    im = lax.bitcast_convert_type(w << 16, F32)
    return jnp.concatenate([re, im], axis=0).astype(BF16)


def _in_proj_kernel(x_ref, g_ref, wq_ref, w_ref, ek_ref, qg_ref, kg_ref, q_ref, kv_ref, ph_ref):
    h = _rms(x_ref[...], g_ref[...]).astype(BF16)
    tm = h.shape[0]

    qt = lax.dot_general(wq_ref[...], h, (((1,), (1,)), ((), ())), preferred_element_type=F32)
    qt = qt.reshape(N_Q_HEADS, HEAD_DIM, tm)
    qt = qt * lax.rsqrt(jnp.mean(qt * qt, axis=1, keepdims=True) + EPS)
    qt = (qt.reshape(D_ATTN, tm) * qg_ref[...]).astype(BF16)
    for j in range(tm // BLOCK):
        q_ref[j] = qt[:, j * BLOCK:(j + 1) * BLOCK]

    p = _dot(h, w_ref[...])
    k = p[:, :D_KV]
    ss = _dot((k * k).astype(BF16), ek_ref[...])
    kv_ref[:, :D_KV] = (k * lax.rsqrt(ss * (1.0 / HEAD_DIM) + EPS) * kg_ref[...]).astype(BF16)
    kv_ref[:, D_KV:] = p[:, D_KV:2 * D_KV].astype(BF16)
    ph_ref[...] = p[:, 2 * D_KV:]


def _in_proj(x2, g, wq_t, w, ek, qg, kg):
    T = x2.shape[0]
    tm = ROW_TILE
    row = lambda i: (i, 0)
    fix = lambda i: (0, 0)
    return pl.pallas_call(
        _in_proj_kernel,
        out_shape=[jax.ShapeDtypeStruct((T // BLOCK, D_ATTN, BLOCK), BF16),
                   jax.ShapeDtypeStruct((T, 2 * D_KV), BF16),
                   jax.ShapeDtypeStruct((T, D_PH), F32)],
        grid=(T // tm,),
        in_specs=[pl.BlockSpec((tm, D_MODEL), row),
                  pl.BlockSpec((1, D_MODEL), fix),
                  pl.BlockSpec((D_ATTN, D_MODEL), fix),
                  pl.BlockSpec((D_MODEL, 2 * D_KV + D_PH), fix),
                  pl.BlockSpec((D_KV, D_KV), fix),
                  pl.BlockSpec((D_ATTN, 1), fix),
                  pl.BlockSpec((1, D_KV), fix)],
        out_specs=[pl.BlockSpec((tm // BLOCK, D_ATTN, BLOCK), lambda i: (i, 0, 0)),
                   pl.BlockSpec((tm, 2 * D_KV), row),
                   pl.BlockSpec((tm, D_PH), row)],
        compiler_params=_params(1),
        name="in_proj",
    )(x2, g, wq_t, w, ek, qg, kg)


def _mix_kernel(sink_ref, *refs, seq_len):
    bias_refs = refs[:MIX_NB]
    (q_ref, kvp_ref, kvc_ref, kvn_ref, phc_ref, php_ref, phn_ref, pw_ref, ps_ref, hw_ref, hb_ref, ga_ref, gb_ref,
     cat_ref, hy_ref, att_ref, s_ref, p_ref, rden_ref) = refs[MIX_NB:]
    l = pl.program_id(1)
    is_first = l == 0
    is_last = l == pl.num_programs(1) - 1
    rows = MIX_NB * BLOCK

    tn = (((0,), (0,)), ((), ()))

    def keys(sb, cols):
        def blk(i):
            if i < 0:
                return kvp_ref[:, cols]
            if i >= MIX_NB:
                return kvn_ref[:, cols]
            return kvc_ref[i * BLOCK:(i + 1) * BLOCK, cols]
        return jnp.concatenate([blk(sb - 1), blk(sb), blk(sb + 1)], axis=0)

    pairs = [(sb, g) for sb in range(MIX_NB) for g in range(N_KV_HEADS)]
    gw = GQA_GROUP * BLOCK
    for i, (sb, g) in enumerate(pairs):
        qg = jnp.concatenate([q_ref[sb, h * HEAD_DIM:(h + 1) * HEAD_DIM, :]
                              for h in range(g * GQA_GROUP, (g + 1) * GQA_GROUP)], axis=1)
        st = _dot(keys(sb, slice(g * HEAD_DIM, (g + 1) * HEAD_DIM)), qg)
        for j in range(GQA_GROUP):
            s_ref[i, :, j * BLOCK:(j + 1) * BLOCK] = st[:, j * BLOCK:(j + 1) * BLOCK] + bias_refs[sb][g * GQA_GROUP + j]
    for i, (sb, g) in enumerate(pairs):
        for j in range(GQA_GROUP):
            cols = slice(j * BLOCK, (j + 1) * BLOCK)
            st = s_ref[i, :, cols]
            sk = sink_ref[g * GQA_GROUP + j]
            m = jnp.maximum(jnp.max(st, axis=0, keepdims=True), sk)
            e = jnp.exp2(st - m)
            rden_ref[i, :, cols] = 1.0 / (jnp.sum(e, axis=0, keepdims=True) + jnp.exp2(sk - m))
            p_ref[i, :, cols] = e.astype(BF16)
    for i, (sb, g) in enumerate(pairs):
        vh = keys(sb, slice(D_KV + g * HEAD_DIM, D_KV + (g + 1) * HEAD_DIM))
        ot = lax.dot_general(vh, p_ref[i], tn, preferred_element_type=F32) * rden_ref[i]
        for j in range(GQA_GROUP):
            h = g * GQA_GROUP + j
            att_ref[sb, h * HEAD_DIM:(h + 1) * HEAD_DIM, :] = ot[:, j * BLOCK:(j + 1) * BLOCK]
    for sb in range(MIX_NB):
        cat_ref[sb * BLOCK:(sb + 1) * BLOCK, :D_ATTN] = _rms(att_ref[sb].T, ga_ref[...]).astype(BF16)

    def halo_ext(cols):
        return jnp.concatenate([jnp.where(is_first, 0.0, php_ref[:, cols]), phc_ref[:, cols],
                                jnp.where(is_last, 0.0, phn_ref[:, cols])], axis=0)

    t = l * rows + lax.broadcasted_iota(jnp.int32, (rows, 1), 0)
    lane = lax.broadcasted_iota(jnp.int32, (1, LANES), 1)

    def count(w):
        lo = jnp.clip(t - w // 2, 0, seq_len)
        hi = jnp.clip(t - w // 2 + w, 0, seq_len)
        return (hi - lo).astype(F32)

    def window_sums(col0, widths):
        n = rows + 2 * SUBLANES
        p, have, out = halo_ext(slice(col0, col0 + LANES)), 1, []
        for w in widths:
            while have < w:
                p = p + pltpu.roll(p, n - have, axis=0)
                have *= 2
            start = SUBLANES - w // 2
            out.append((p if start == 0 else pltpu.roll(p, n - start, axis=0))[0:rows])
        return out

    gpt = LANES // POOL_GROUP_DIM
    means = []
    for i in range(D_POOL // LANES):
        wa, wb = POOL_WINDOWS[gpt * i], POOL_WINDOWS[gpt * i + 1]
        sa, sb = window_sums(i * LANES, (wa, wb))
        means.append(jnp.where(lane < POOL_GROUP_DIM, sa / count(wa), sb / count(wb)))
    u = phc_ref[:, 0:D_POOL]
    pooled = _dot((jnp.concatenate(means, axis=1) - u).astype(BF16), pw_ref[...]) * ps_ref[...]
    cat_ref[:, D_ATTN:] = _rms(pooled, gb_ref[...]).astype(BF16)

    ridx = lax.broadcasted_iota(jnp.int32, (rows, 1), 0)
    hc = phc_ref[:, D_POOL:]
    hp = jnp.where(ridx == 0, jnp.where(is_first, 0.0, php_ref[SUBLANES - 1:SUBLANES, D_POOL:]),
                   pltpu.roll(hc, 1, axis=0))
    hn = jnp.where(ridx == rows - 1, jnp.where(is_last, 0.0, phn_ref[0:1, D_POOL:]),
                   pltpu.roll(hc, rows - 1, axis=0))
    _store_tiles(hy_ref, (), 0, hp * hw_ref[0:1, :] + hc * hw_ref[1:2, :] + hn * hw_ref[2:3, :] + hb_ref[...])


def _attn_bias():
    row = jnp.arange(BLOCK, dtype=jnp.int32)[None, :]
    col = jnp.arange(3 * BLOCK, dtype=jnp.int32)[:, None]
    dist = jnp.abs(row + BLOCK - col)
    slopes = 2.0 ** (-8.0 * (jnp.arange(N_Q_HEADS, dtype=F32) + 1.0) / N_Q_HEADS)
    bias = -(LOG2E * slopes)[:, None, None] * dist.astype(F32)[None]
    out = []
    for case in range(4):
        valid = dist <= WINDOW
        if case & 1:
            valid = valid & (col >= BLOCK)
        if case & 2:
            valid = valid & (col < 2 * BLOCK)
        out.append(jnp.where(valid[None], bias, NEG_INF))
    return jnp.stack(out)


def _mix(q, kv, ph, sink, pw_bd, pool_scale, hy_w, hy_b, g_a, g_b, B, L):
    T = B * L
    rows = MIX_NB * BLOCK
    nb = L // BLOCK
    ns = L // rows
    rps = rows // SUBLANES
    n8 = T // SUBLANES
    cur = lambda b, l: (b * ns + l, 0)
    prev = lambda b, l: (b * nb + jnp.maximum(l * MIX_NB - 1, 0), 0)
    nxt = lambda b, l: (b * nb + jnp.minimum((l + 1) * MIX_NB, nb - 1), 0)
    prev8 = lambda b, l: (jnp.maximum((b * ns + l) * rps - 1, 0), 0)
    next8 = lambda b, l: (jnp.minimum((b * ns + l + 1) * rps, n8 - 1), 0)
    fix = lambda b, l: (0, 0)

    def bias_spec(sb):
        def index(b, l):
            case = jnp.int32(0)
            if sb == 0:
                case = case + (l == 0).astype(jnp.int32)
            if sb == MIX_NB - 1:
                case = case + 2 * (l == ns - 1).astype(jnp.int32)
            return (case, 0, 0, 0)
        return pl.BlockSpec((None, N_Q_HEADS, 3 * BLOCK, BLOCK), index)

    bias = _attn_bias()
    n_pairs = MIX_NB * N_KV_HEADS
    gw = GQA_GROUP * BLOCK
    return pl.pallas_call(
        functools.partial(_mix_kernel, seq_len=L),
        out_shape=[jax.ShapeDtypeStruct((T, D_AB), BF16),
                   jax.ShapeDtypeStruct((B, FFT_JH, D_HY_IN // LANES, nb * FFT_J, LANES), F32)],
        grid=(B, ns),
        in_specs=[pl.BlockSpec(memory_space=pltpu.SMEM)] + [bias_spec(sb) for sb in range(MIX_NB)] + [
                  pl.BlockSpec((MIX_NB, D_ATTN, BLOCK), lambda b, l: (b * ns + l, 0, 0)),
                  pl.BlockSpec((BLOCK, 2 * D_KV), prev),
                  pl.BlockSpec((rows, 2 * D_KV), cur),
                  pl.BlockSpec((BLOCK, 2 * D_KV), nxt),
                  pl.BlockSpec((rows, D_PH), cur),
                  pl.BlockSpec((SUBLANES, D_PH), prev8),
                  pl.BlockSpec((SUBLANES, D_PH), next8),
                  pl.BlockSpec((D_POOL, D_POOL), fix),
                  pl.BlockSpec((1, D_POOL), fix),
                  pl.BlockSpec((3, D_HY_IN), fix),
                  pl.BlockSpec((1, D_HY_IN), fix),
                  pl.BlockSpec((1, D_ATTN), fix),
                  pl.BlockSpec((1, D_POOL), fix)],
        out_specs=[pl.BlockSpec((rows, D_AB), cur),
                   pl.BlockSpec((None, FFT_JH, D_HY_IN // LANES, MIX_NB * FFT_J, LANES),
                                lambda b, l: (b, 0, 0, l, 0))],
        scratch_shapes=[pltpu.VMEM((MIX_NB, D_ATTN, BLOCK), F32),
                        pltpu.VMEM((n_pairs, 3 * BLOCK, gw), F32),
                        pltpu.VMEM((n_pairs, 3 * BLOCK, gw), BF16),
                        pltpu.VMEM((n_pairs, 1, gw), F32)],
        compiler_params=_params(2),
        name="mix",
    )(sink, *([bias] * MIX_NB), q, kv, kv, kv, ph, ph, ph, pw_bd, pool_scale, hy_w, hy_b, g_a, g_b)


def _split3(x, axis):
    hi = x.astype(BF16)
    lo = (x - hi.astype(F32)).astype(BF16)
    return jnp.concatenate([hi, lo, hi], axis=axis)


def _stack3(w):
    hi = w.astype(BF16)
    lo = (w - hi.astype(F32)).astype(BF16)
    return jnp.concatenate([hi, hi, lo], axis=0)


def _filter_kernel(z_ref, t_ref, w1_ref, b1_ref, f1_ref, w2_ref, b2_ref, f2_ref, w3_ref, dl_ref, o_ref, *, seq_len):
    h = jnp.sin(f1_ref[...] * (_dot(w1_ref[...], z_ref[...]) + b1_ref[...]))
    h = jnp.sin(f2_ref[...] * (_dot(w2_ref[...], _split3(h, 0)) + b2_ref[...]))
    h = lax.dot_general(_split3(h, 0), w3_ref[...], (((0,), (0,)), ((), ())),
                        preferred_element_type=F32)
    n = pl.program_id(0) * h.shape[0] + lax.broadcasted_iota(jnp.int32, (h.shape[0], 1), 0)
    decay = jnp.where(n == seq_len, 0.0, jnp.exp(-t_ref[...] * dl_ref[...]))
    for o in range(HYENA_ORDER):
        _store_tiles(o_ref, (o,), 0, h[:, o * D_HYENA:(o + 1) * D_HYENA] * decay)


def _filter_kernels(L, w1, b1, fr1, w2, b2, fr2, w3):
    pos = jnp.arange(2 * L, dtype=jnp.int32)
    tpos = jnp.where(pos < L, pos, (2 * L - pos) % L).astype(F32)[:, None]
    t_norm = tpos * (1.0 / (L - 1))
    bands = jnp.linspace(1e-4, FILTER_BANDS - 1, FILTER_BANDS, dtype=F32)[None, :]
    ang = 2.0 * math.pi * tpos * bands / L
    z = jnp.concatenate([t_norm, jnp.cos(ang), -jnp.sin(ang)], axis=-1)
    z3 = jnp.pad(_split3(z, 1), ((0, 0), (0, LANES - 3 * FILTER_EMB))).T
    w1s = jnp.pad(_stack3(w1), ((0, LANES - 3 * FILTER_EMB), (0, 0)))
    w3d = w3.reshape(FILTER_HIDDEN, HYENA_ORDER, 2, D_HYENA).transpose(2, 0, 1, 3).reshape(
        2, FILTER_HIDDEN, HYENA_ORDER * D_HYENA)
    w3s = jnp.stack([_stack3(w3d[0]), _stack3(w3d[1])])
    max_decay = math.log(DECAY_TARGET) / DECAY_FAST_PCT
    min_decay = math.log(DECAY_TARGET) / DECAY_SLOW_PCT
    dl = jnp.abs(jnp.linspace(min_decay, max_decay, D_HYENA, dtype=F32))[None, :]
    tl = 1024
    half = L // tl
    row = lambda i: (i, 0)
    fix = lambda i: (0, 0)
    return pl.pallas_call(
        functools.partial(_filter_kernel, seq_len=L),
        out_shape=jax.ShapeDtypeStruct((HYENA_ORDER, FFT_JH, HY_CT, 2 * L // FFT_N2 * FFT_J, LANES), F32),
        grid=(2 * L // tl,),
        in_specs=[pl.BlockSpec((LANES, tl), lambda i: (0, i)),
                  pl.BlockSpec((tl, 1), row),
                  pl.BlockSpec((FILTER_HIDDEN, LANES), fix),
                  pl.BlockSpec((FILTER_HIDDEN, 1), fix),
                  pl.BlockSpec((FILTER_HIDDEN, 1), fix),
                  pl.BlockSpec((FILTER_HIDDEN, 3 * FILTER_HIDDEN), fix),
                  pl.BlockSpec((FILTER_HIDDEN, 1), fix),
                  pl.BlockSpec((FILTER_HIDDEN, 1), fix),
                  pl.BlockSpec((None, 3 * FILTER_HIDDEN, HYENA_ORDER * D_HYENA), lambda i: (i // half, 0, 0)),
                  pl.BlockSpec((1, D_HYENA), fix)],
        out_specs=pl.BlockSpec((HYENA_ORDER, FFT_JH, HY_CT, tl // FFT_N2 * FFT_J, LANES), lambda i: (0, 0, 0, i, 0)),
        compiler_params=_params(1),
        name="hyena_filters",
    )(z3, t_norm, w1s.T, b1[:, None], fr1[:, None], _stack3(w2).T, b2[:, None], fr2[:, None], w3s, dl)


def _dft_mats(L):
    N = 2 * L
    n1n = N // FFT_N2
    r = n1n // 2
    assert r % FFT_TF == 0, "sequence length must be a multiple of FFT_TF * FFT_N2 / 2"
    mh = r + FFT_TF
    f1 = jnp.arange(mh, dtype=jnp.int32)[:, None]
    n1 = jnp.arange(n1n, dtype=jnp.int32)[None, :]
    keep = (f1 <= r).astype(F32)
    th = (2.0 * math.pi / n1n) * ((f1 * n1) % n1n).astype(F32)
    c1, s1 = keep * jnp.cos(th), keep * jnp.sin(th)
    fwd_full = jnp.concatenate([c1, -s1], axis=0).astype(BF16)
    fwd_half = fwd_full[:, :r]
    w = jnp.where((f1 == 0) | (f1 == r), 1.0, 2.0)
    inv_half = jnp.concatenate([(w * c1)[:, :r].T, (-w * s1)[:, :r].T], axis=1).astype(BF16)
    i2 = jnp.arange(2 * FFT_N2, dtype=jnp.int32)
    part, idx = i2 // FFT_N2, i2 % FFT_N2
    pa = (2.0 * math.pi / N) * ((jnp.arange(mh, dtype=jnp.int32)[:, None] * idx[None, :]) % N).astype(F32)
    ca, sa = jnp.cos(pa), jnp.sin(pa)
    mb = (idx[:, None] * idx[None, :] * n1n + (part[:, None] - part[None, :]) * (N // 4)) % N
    pb = (2.0 * math.pi / N) * mb.astype(F32)
    cb, sb = jnp.cos(pb), jnp.sin(pb)
    g = (ca[:, None, :] * cb[None] - sa[:, None, :] * sb[None]).astype(BF16)
    return dict(n1=n1n, r=r, mh=mh, fwd_full=fwd_full, fwd_half=fwd_half, inv_half=inv_half, g=g)


def _outer_blocks(tb):
    return min(4, max(1, 16 // tb))


def _fft_a_kernel(x_ref, f_ref, a_ref, *, mh):
    f = f_ref[...]
    r = x_ref.shape[3] // FFT_J
    for lead in [(b, jb, ct) for b in range(x_ref.shape[0]) for jb in range(x_ref.shape[1]) for ct in range(HY_CT)]:
        for j in range(FFT_J):
            a = _dot(f, x_ref[lead + (pl.ds(j, r, stride=FFT_J), slice(None))].astype(BF16))
            a_ref[lead + (pl.ds(j, mh, stride=FFT_J), slice(None))] = _pack_pair(a[:mh], a[mh:])


def _fft_a(x5, coff, fmat, mh, tb):
    B, rj = x5.shape[0], x5.shape[3]
    jb = _outer_blocks(tb)
    return pl.pallas_call(
        functools.partial(_fft_a_kernel, mh=mh),
        out_shape=jax.ShapeDtypeStruct((B, FFT_JH, HY_CT, mh * FFT_J, LANES), jnp.uint32),
        grid=(B // tb, FFT_JH // jb),
        in_specs=[pl.BlockSpec((tb, jb, HY_CT, rj, LANES), lambda b, j: (b, j, coff // HY_CT, 0, 0)),
                  pl.BlockSpec((2 * mh, rj // FFT_J), lambda b, j: (0, 0))],
        out_specs=pl.BlockSpec((tb, jb, HY_CT, mh * FFT_J, LANES), lambda b, j: (b, j, 0, 0, 0)),
        compiler_params=_params(2),
        name="fft_outer_fwd",
    )(x5, fmat)


def _fft_spec_kernel(a_ref, g_ref, k_ref, *, scale):
    for f in range(FFT_TF):
        x = _dot(g_ref[f], _unpack_pair(_load_tiles(a_ref, (), f * FFT_J, 1, HY_CT))) * scale
        k_ref[0, f] = _pack_pair(x[:FFT_N2], x[FFT_N2:])


def _tile_spec():
    return pl.BlockSpec((None, FFT_JH, HY_CT, FFT_TF * FFT_J, LANES), lambda f, b: (b, 0, 0, f, 0))


def _fft_spec(a, g, scale):
    B, mh = a.shape[0], a.shape[3] // FFT_J
    nat = pl.BlockSpec((1, FFT_TF, FFT_N2, D_HYENA), lambda f, b: (b, f, 0, 0))
    return pl.pallas_call(
        functools.partial(_fft_spec_kernel, scale=scale),
        out_shape=jax.ShapeDtypeStruct((B, mh, FFT_N2, D_HYENA), jnp.uint32),
        grid=(mh // FFT_TF, B),
        in_specs=[_tile_spec(), pl.BlockSpec((FFT_TF, 2 * FFT_N2, 2 * FFT_N2), lambda f, b: (f, 0, 0))],
        out_specs=nat,
        compiler_params=_params(2),
        name="fft_inner_fwd",
    )(a, g)


def _fft_b_kernel(a_ref, g_ref, k_ref, y_ref):
    tn = (((0,), (0,)), ((), ()))
    for f in range(FFT_TF):
        kw = k_ref[0, f]
        kr = lax.bitcast_convert_type(kw & jnp.uint32(0xFFFF0000), F32)
        ki = lax.bitcast_convert_type(kw << 16, F32)
        for b in range(a_ref.shape[0]):
            x = _dot(g_ref[f], _unpack_pair(_load_tiles(a_ref, (b,), f * FFT_J, 1, HY_CT)))
            xr, xi = x[:FFT_N2], x[FFT_N2:]
            z = jnp.concatenate([xr * kr - xi * ki, xr * ki + xi * kr], axis=0).astype(BF16)
            y = lax.dot_general(g_ref[f], z, tn, preferred_element_type=F32)
            _store_tiles(y_ref, (b,), f * FFT_J, _pack_pair(y[:FFT_N2], y[FFT_N2:]))


def _fft_b(a, g, k, o):
    B, mh = a.shape[0], a.shape[3] // FFT_J
    tb = math.gcd(B, 4)
    dat = pl.BlockSpec((tb, FFT_JH, HY_CT, FFT_TF * FFT_J, LANES), lambda f, b: (b, 0, 0, f, 0))
    mat = pl.BlockSpec((FFT_TF, 2 * FFT_N2, 2 * FFT_N2), lambda f, b: (f, 0, 0))
    spec = pl.BlockSpec((1, FFT_TF, FFT_N2, D_HYENA), lambda f, b: (o, f, 0, 0))
    return pl.pallas_call(
        _fft_b_kernel,
        out_shape=jax.ShapeDtypeStruct(a.shape, jnp.uint32),
        grid=(mh // FFT_TF, B // tb),
        in_specs=[dat, mat, spec],
        out_specs=dat,
        compiler_params=_params(2),
        name="fft_inner",
    )(a, g, k)


def _fft_c_kernel(y_ref, c_ref, u_ref, gate_ref, bias_ref, o_ref):
    cm = c_ref[...]
    mh, r = y_ref.shape[3] // FFT_J, u_ref.shape[3] // FFT_J
    for lead in [(b, jb, ct) for b in range(y_ref.shape[0]) for jb in range(y_ref.shape[1]) for ct in range(HY_CT)]:
        bias = bias_ref[:, lead[2] * LANES:(lead[2] + 1) * LANES]
        for j in range(FFT_J):
            y = _dot(cm, _unpack_pair(y_ref[lead + (pl.ds(j, mh, stride=FFT_J), slice(None))]))
            rows = lead + (pl.ds(j, r, stride=FFT_J), slice(None))
            o_ref[rows] = gate_ref[rows] * (y + bias * u_ref[rows])


def _fft_c(y, cmat, u5, uoff, gate5, goff, bias, tb):
    B, mhj = y.shape[0], y.shape[3]
    rj = cmat.shape[0] * FFT_J
    jb = _outer_blocks(tb)
    blk = lambda b, j: (b, j, 0, 0, 0)
    return pl.pallas_call(
        _fft_c_kernel,
        out_shape=jax.ShapeDtypeStruct((B, FFT_JH, HY_CT, rj, LANES), F32),
        grid=(B // tb, FFT_JH // jb),
        in_specs=[pl.BlockSpec((tb, jb, HY_CT, mhj, LANES), blk),
                  pl.BlockSpec(cmat.shape, lambda b, j: (0, 0)),
                  pl.BlockSpec((tb, jb, HY_CT, rj, LANES), lambda b, j: (b, j, uoff // HY_CT, 0, 0)),
                  pl.BlockSpec((tb, jb, HY_CT, rj, LANES), lambda b, j: (b, j, goff // HY_CT, 0, 0)),
                  pl.BlockSpec((1, D_HYENA), lambda b, j: (0, 0))],
        out_specs=pl.BlockSpec((tb, jb, HY_CT, rj, LANES), blk),
        compiler_params=_params(2),
        name="fft_outer_inv",
    )(y, cmat, u5, gate5, bias)


def _filter_spectra(kern, mats, L):
    return _fft_spec(_fft_a(kern, 0, mats["fwd_full"], mats["mh"], 1), mats["g"], 1.0 / (2 * L))


def _hyena(hy5, k, hy_bias, mats, tb):
    z5 = hy5
    for o in range(HYENA_ORDER):
        a = _fft_a(z5, 0, mats["fwd_half"], mats["mh"], tb)
        y = _fft_b(a, mats["g"], k, o)
        z5 = _fft_c(y, mats["inv_half"], z5, 0, hy5, (o + 1) * HY_CT, hy_bias[o][None, :], tb)
    return z5


def _out_proj_kernel(x_ref, cab_ref, c_ref, gc_ref, wab_ref, wc_ref, xn_ref):
    c = _load_tiles(c_ref, (), 0, c_ref.shape[2] // FFT_J, HY_CT)
    cn = _rms(c, gc_ref[...]).astype(BF16)
    xn_ref[...] = x_ref[...] + _dot(cab_ref[...], wab_ref[...]) + _dot(cn, wc_ref[...])


def _out_proj(x2, cab, c5, g_c, w_ab, w_c):
    T = x2.shape[0]
    tm = ROW_TILE
    tps = c5.shape[3] // FFT_J * FFT_N2 // tm
    row = lambda i: (i, 0)
    fix = lambda i: (0, 0)
    return pl.pallas_call(
        _out_proj_kernel,
        out_shape=jax.ShapeDtypeStruct((T, D_MODEL), F32),
        grid=(T // tm,),
        in_specs=[pl.BlockSpec((tm, D_MODEL), row),
                  pl.BlockSpec((tm, D_AB), row),
                  pl.BlockSpec((None, FFT_JH, HY_CT, tm // FFT_N2 * FFT_J, LANES),
                               lambda i: (i // tps, 0, 0, i % tps, 0)),
                  pl.BlockSpec((1, D_HYENA), fix),
                  pl.BlockSpec((D_AB, D_MODEL), fix),
                  pl.BlockSpec((D_HYENA, D_MODEL), fix)],
        out_specs=pl.BlockSpec((tm, D_MODEL), row),
        compiler_params=_params(1),
        name="out_proj",
    )(x2, cab, c5, g_c, w_ab, w_c)


def _ffn_kernel(xn_ref, xp_ref, xq_ref, g2_ref, wi_ref, cw_ref, cb_ref, wo_ref, o_ref,
                acc_ref, hx_ref, ug_ref, uv_ref, *, tiles_per_seq):
    i = pl.program_id(0)
    is_first = (i % tiles_per_seq) == 0
    is_last = (i % tiles_per_seq) == tiles_per_seq - 1
    tm = xn_ref.shape[0]
    row = lax.broadcasted_iota(jnp.int32, (tm, 1), 0)
    hx_ref[0:tm, :] = _rms(xn_ref[...], g2_ref[...]).astype(BF16)
    hx_ref[tm:, :] = _rms(jnp.concatenate([xp_ref[...], xq_ref[...]], axis=0), g2_ref[...]).astype(BF16)

    def up(j):
        ug_ref[j % 2] = _dot(hx_ref[...], wi_ref[j])
        uv_ref[j % 2] = _dot(hx_ref[...], wi_ref[FFN_CHUNKS + j])

    def conv(u_ref, j, k):
        uc = u_ref[j % 2, 0:tm, :]
        before = jnp.where(is_first, 0.0, u_ref[j % 2, tm + SUBLANES - 1:tm + SUBLANES, :])
        after = jnp.where(is_last, 0.0, u_ref[j % 2, tm + SUBLANES:tm + SUBLANES + 1, :])
        um = jnp.where(row == 0, before, pltpu.roll(uc, 1, axis=0))
        un = jnp.where(row == tm - 1, after, pltpu.roll(uc, tm - 1, axis=0))
        w = cw_ref[k]
        return um * w[0:1] + uc * w[1:2] + un * w[2:3] + cb_ref[k]

    def down(j):
        gate = conv(ug_ref, j, j)
        half_val = conv(uv_ref, j, FFN_CHUNKS + j)
        act = gate * (1.0 + lax.erf(gate * math.sqrt(0.5))) * half_val
        acc_ref[...] += _dot(act.astype(BF16), wo_ref[j])

    acc_ref[...] = xn_ref[...]
    up(0)
    for j in range(FFN_CHUNKS):
        if j + 1 < FFN_CHUNKS:
            up(j + 1)
        down(j)
    o_ref[...] = acc_ref[...]


def _ffn(xn, g2, w_in, cw, cb, w_out, L):
    T = xn.shape[0]
    tm = ROW_TILE
    ch = D_FF // FFN_CHUNKS
    hpt = tm // SUBLANES
    n8 = T // SUBLANES
    wi = w_in.reshape(D_MODEL, 2 * FFN_CHUNKS, ch).transpose(1, 0, 2)
    cwc = cw.reshape(3, 2 * FFN_CHUNKS, ch).transpose(1, 0, 2)
    cbc = cb.reshape(2 * FFN_CHUNKS, 1, ch)
    wo = w_out.reshape(FFN_CHUNKS, ch, D_MODEL)
    row = lambda i: (i, 0)
    fix3 = lambda i: (0, 0, 0)
    once = pl.Buffered(1)
    return pl.pallas_call(
        functools.partial(_ffn_kernel, tiles_per_seq=L // tm),
        out_shape=jax.ShapeDtypeStruct((T, D_MODEL), F32),
        grid=(T // tm,),
        in_specs=[pl.BlockSpec((tm, D_MODEL), row),
                  pl.BlockSpec((SUBLANES, D_MODEL), lambda i: (jnp.maximum(i * hpt - 1, 0), 0)),
                  pl.BlockSpec((SUBLANES, D_MODEL), lambda i: (jnp.minimum((i + 1) * hpt, n8 - 1), 0)),
                  pl.BlockSpec((1, D_MODEL), lambda i: (0, 0)),
                  pl.BlockSpec((2 * FFN_CHUNKS, D_MODEL, ch), fix3, pipeline_mode=once),
                  pl.BlockSpec((2 * FFN_CHUNKS, 3, ch), fix3),
                  pl.BlockSpec((2 * FFN_CHUNKS, 1, ch), fix3),
                  pl.BlockSpec((FFN_CHUNKS, ch, D_MODEL), fix3, pipeline_mode=once)],
        out_specs=pl.BlockSpec((tm, D_MODEL), row),
        scratch_shapes=[pltpu.VMEM((tm, D_MODEL), F32),
                        pltpu.VMEM((tm + BF16_ROWS, D_MODEL), BF16),
                        pltpu.VMEM((2, tm + BF16_ROWS, ch), F32),
                        pltpu.VMEM((2, tm + BF16_ROWS, ch), F32)],
        compiler_params=_params(1),
        name="ffn",
    )(xn, xn, xn, g2, wi, cwc, cbc, wo)


def _head_ones(n_heads):
    return jnp.kron(jnp.eye(n_heads, dtype=F32), jnp.ones((HEAD_DIM, HEAD_DIM), F32)).astype(BF16)


def _layer(x2, B, L, mats, spectra, lw):
    q, kv, ph = _in_proj(x2, lw["norm1_g"], lw["wq_t"], lw["w_in"], lw["ek"], lw["qg"], lw["kg"])
    cab, hy = _mix(q, kv, ph, lw["sink"], lw["pw_bd"], lw["pool_scale"], lw["hy_w"], lw["hy_b"],
                   lw["g_a"], lw["g_b"], B, L)
    tb = math.gcd(B, 8)
    c5 = _hyena(hy, spectra, lw["hy_bias"], mats, tb)
    xn = _out_proj(x2, cab, c5, lw["g_c"], lw["w_ab"], lw["w_c"])
    return _ffn(xn, lw["norm2_g"], lw["w_ffn_in"], lw["ffn_cw"], lw["ffn_cb"], lw["w_ffn_out"], L)


def kernel(x_prompt, x_sample, norm1_g, w_in, q_norm_g, k_norm_g, attn_sink, pool_w, pool_scale, hy_conv_w,
           hy_conv_b, filt_w1, filt_b1, filt_freq1, filt_w2, filt_b2, filt_freq2, filt_w3, hy_bias, out_norm_g,
           w_out, norm2_g, w_ffn_in, ffn_conv_w, ffn_conv_b, w_ffn_out):
    groups = [x_prompt, x_sample]
    lens = sorted({g.shape[1] for g in groups})
    mats = {L: _dft_mats(L) for L in lens}
    ek = _head_ones(N_KV_HEADS)
    xs = [g.reshape(-1, D_MODEL) for g in groups]
    glu_scale = jnp.concatenate([jnp.ones((D_FF,), F32), jnp.full((D_FF,), 0.5, F32)])[None, :]
    for l in range(DEPTH):
        lw = dict(
            norm1_g=norm1_g[l][None, :], wq_t=w_in[l, :, :D_ATTN].T.astype(BF16), w_in=w_in[l, :, D_ATTN:].astype(BF16),
            ek=ek, kg=jnp.tile(k_norm_g[l], N_KV_HEADS)[None, :],
            qg=(jnp.tile(q_norm_g[l], N_Q_HEADS) * (LOG2E / math.sqrt(HEAD_DIM)))[:, None],
            sink=attn_sink[l] * LOG2E,
            pw_bd=jax.scipy.linalg.block_diag(*[pool_w[l, g] for g in range(len(POOL_WINDOWS))]).astype(BF16),
            pool_scale=pool_scale[l][None, :], hy_w=hy_conv_w[l], hy_b=hy_conv_b[l][None, :],
            g_a=out_norm_g[l, :D_ATTN][None, :], g_b=out_norm_g[l, D_ATTN:D_AB][None, :],
            g_c=out_norm_g[l, D_AB:][None, :], hy_bias=hy_bias[l],
            w_ab=w_out[l, :D_AB].astype(BF16), w_c=w_out[l, D_AB:].astype(BF16),
            norm2_g=norm2_g[l][None, :], w_ffn_in=w_ffn_in[l].astype(BF16),
            ffn_cw=ffn_conv_w[l] * glu_scale, ffn_cb=(ffn_conv_b[l] * glu_scale[0])[None, :],
            w_ffn_out=w_ffn_out[l].astype(BF16),
        )
        spectra = {}
        for L in lens:
            kern = _filter_kernels(L, filt_w1[l], filt_b1[l], filt_freq1[l], filt_w2[l], filt_b2[l], filt_freq2[l],
                                   filt_w3[l])
            spectra[L] = _filter_spectra(kern, mats[L], L)
        xs = [_layer(x2, g.shape[0], g.shape[1], mats[g.shape[1]], spectra[g.shape[1]], lw)
              for x2, g in zip(xs, groups)]
    return tuple(x2.reshape(g.shape) for x2, g in zip(xs, groups))
```
